```python
import jax, jax.numpy as jnp
from jax import lax
import numpy as np

D_MODEL = 1024
BATCH = 2
SEQ = 8192
DEPTH = 1

GRID_W = 64
CTX_LEN = 256
EPS = 1e-6
LRU_WIDTH = 1024
LRU_HEADS = 16
LRU_HEAD_DIM = LRU_WIDTH // LRU_HEADS
LRU_CONV = 4
LRU_C = 8.0
SSD_WIDTH = 1024
SSD_HEAD_DIM = 64
SSD_HEADS = SSD_WIDTH // SSD_HEAD_DIM
SSD_GROUPS = 2
SSD_STATE = 128
SSD_CONV = 4
SSD_CHUNK = 128
SSD_CONV_DIM = SSD_WIDTH + 2 * SSD_GROUPS * SSD_STATE
D_MIX = LRU_WIDTH + SSD_WIDTH
D_IN = 2 * LRU_WIDTH + SSD_WIDTH + SSD_CONV_DIM + 2 * SSD_HEADS
D_FF = 3 * D_MODEL
FFN_CONV = 3

kernel_name = "hybrid_rglru_ssd_convffn_prefix_ctx"


def rms_norm(x, g):
    xf = x.astype(jnp.float32)
    y = xf * lax.rsqrt(jnp.mean(xf * xf, axis=-1, keepdims=True) + EPS)
    return (y * g.astype(jnp.float32)).astype(x.dtype)


def modulate(x, shift, scale):
    return x * (1 + scale) + shift


def dw_conv(x, w, b):
    k = w.shape[0]
    pad_l = k // 2
    y = lax.conv_general_dilated(
        x, w[:, None, :].astype(x.dtype), window_strides=(1,),
        padding=[(pad_l, k - 1 - pad_l)], dimension_numbers=('NWC', 'WIO', 'NWC'),
        feature_group_count=x.shape[-1])
    return y + b.astype(x.dtype)


def _flip(t, rev):
    return jnp.flip(t, axis=1) if rev else t


def raster_to_column(x):
    b, l, ch = x.shape
    rows = l // GRID_W
    return x.reshape(b, rows, GRID_W, ch).transpose(0, 2, 1, 3).reshape(b, l, ch)


def column_to_raster(x):
    b, l, ch = x.shape
    rows = l // GRID_W
    return x.reshape(b, GRID_W, rows, ch).transpose(0, 2, 1, 3).reshape(b, l, ch)


def linear_scan(a, u, h0):
    u = u.at[:, 0].add(a[:, 0] * h0)

    def combine(left, right):
        a_l, u_l = left
        a_r, u_r = right
        return a_l * a_r, a_r * u_l + u_r

    return lax.associative_scan(combine, (a, u), axis=1)[1]


def rg_lru_scan(xc, wa, ba, wx, bx, lam, h0):
    b, l, w = xc.shape
    f32 = jnp.float32
    xf = xc.astype(f32)
    xh = xf.reshape(b, l, LRU_HEADS, LRU_HEAD_DIM)
    gate_r = jax.nn.sigmoid(jnp.einsum('blhi,hij->blhj', xh, wa.astype(f32)).reshape(b, l, w) + ba.astype(f32))
    gate_i = jax.nn.sigmoid(jnp.einsum('blhi,hij->blhj', xh, wx.astype(f32)).reshape(b, l, w) + bx.astype(f32))
    log_a = -LRU_C * gate_r * jax.nn.softplus(-lam.astype(f32))
    a = jnp.exp(log_a)
    u = jnp.sqrt(-jnp.expm1(2.0 * log_a)) * (gate_i * xf)
    return linear_scan(a, u, h0)


def rglru_mixer(u_lat, gate_lat, u_ctx, gate_ctx, conv_w, conv_b, wa, ba, wx, bx, lam, with_ctx_out):
    xc_lat = dw_conv(u_lat, conv_w, conv_b)
    xc_ctx = dw_conv(u_ctx, conv_w, conv_b)
    zero = jnp.zeros((u_lat.shape[0], LRU_WIDTH), jnp.float32)
    y_lat = 0.0
    h_ctx_dirs = []
    for d in (0, 1):
        rev = d == 1
        h_ctx = rg_lru_scan(_flip(xc_ctx, rev), wa[d], ba[d], wx[d], bx[d], lam[d], zero)
        h_lat = rg_lru_scan(_flip(xc_lat, rev), wa[d], ba[d], wx[d], bx[d], lam[d], h_ctx[:, -1])
        y_lat = y_lat + _flip(h_lat, rev)
        h_ctx_dirs.append(_flip(h_ctx, rev))
    out_lat = y_lat.astype(u_lat.dtype) * jax.nn.gelu(gate_lat)
    out_ctx = None
    if with_ctx_out:
        out_ctx = (h_ctx_dirs[0] + h_ctx_dirs[1]).astype(u_ctx.dtype) * jax.nn.gelu(gate_ctx)
    return out_lat, out_ctx


def segsum(x):
    t = x.shape[-1]
    x_rep = jnp.broadcast_to(x[..., :, None], x.shape + (t,))
    x_rep = jnp.where(jnp.tril(jnp.ones((t, t), dtype=bool), -1), x_rep, 0.0)
    ss = jnp.cumsum(x_rep, axis=-2)
    return jnp.where(jnp.tril(jnp.ones((t, t), dtype=bool)), ss, -jnp.inf)


def ssd_chunked(x, dt, a, bm, cm, h0):
    b, l, h, p = x.shape
    g, n = bm.shape[-2:]
    r = h // g
    t = SSD_CHUNK
    nc = l // t
    xs = (x * dt[..., None]).reshape(b, nc, t, g, r, p)
    da = (dt * a).reshape(b, nc, t, g, r).transpose(0, 3, 4, 1, 2)
    bm = bm.reshape(b, nc, t, g, n)
    cm = cm.reshape(b, nc, t, g, n)
    da_cs = jnp.cumsum(da, axis=-1)
    decay = jnp.exp(segsum(da))
    scores = jnp.einsum('bctgn,bcsgn->bgcts', cm, bm)
    y_diag = jnp.einsum('bgcts,bgrcts,bcsgrp->bctgrp', scores, decay, xs)
    decay_states = jnp.exp(da_cs[..., -1:] - da_cs)
    states = jnp.einsum('bcsgn,bgrcs,bcsgrp->bcgrpn', bm, decay_states, xs)
    chunk_decay = jnp.exp(da_cs[..., -1]).transpose(0, 3, 1, 2)
    h0g = h0.reshape(b, g, r, p, n)
    h_end = linear_scan(chunk_decay[..., None, None], states, h0g)
    h_start = jnp.concatenate([h0g[:, None], h_end[:, :-1]], axis=1)
    y_off = jnp.einsum('bctgn,bcgrpn,bgrct->bctgrp', cm, h_start, jnp.exp(da_cs))
    y = (y_diag + y_off).reshape(b, l, h, p)
    return y, h_end[:, -1].reshape(b, h, p, n)


def ssd_scan(xh, dt_raw, bm, cm, a_log, dt_bias, h0, rev):
    f32 = jnp.float32
    xh, dt_raw, bm, cm = (_flip(v, rev) for v in (xh, dt_raw, bm, cm))
    dt = jax.nn.softplus(dt_raw.astype(f32) + dt_bias.astype(f32))
    y, h_fin = ssd_chunked(xh.astype(f32), dt, -jnp.exp(a_log.astype(f32)), bm.astype(f32), cm.astype(f32), h0)
    return _flip(y, rev), h_fin


def _ssd_split(xbc):
    b, l, _ = xbc.shape
    xs, bm, cm = jnp.split(xbc, [SSD_WIDTH, SSD_WIDTH + SSD_GROUPS * SSD_STATE], axis=-1)
    return (xs.reshape(b, l, SSD_HEADS, SSD_HEAD_DIM),
            bm.reshape(b, l, SSD_GROUPS, SSD_STATE),
            cm.reshape(b, l, SSD_GROUPS, SSD_STATE))


def ssd_mixer(xbc_lat, dt_lat, z_lat, xbc_ctx, dt_ctx, z_ctx, conv_w, conv_b, a_log, dt_bias, d_skip, norm_g, with_ctx_out):
    b, l, _ = xbc_lat.shape
    xbc_lat = raster_to_column(xbc_lat)
    dt_lat = raster_to_column(dt_lat)
    xl, bl, cl = _ssd_split(jax.nn.silu(dw_conv(xbc_lat, conv_w, conv_b)))
    xc, bc, cc = _ssd_split(jax.nn.silu(dw_conv(xbc_ctx, conv_w, conv_b)))
    h0 = jnp.zeros((b, SSD_HEADS, SSD_HEAD_DIM, SSD_STATE), jnp.float32)
    dsk = d_skip.astype(jnp.float32)[:, None]
    y_lat = xl.astype(jnp.float32) * dsk
    y_ctx = xc.astype(jnp.float32) * dsk
    for d in (0, 1):
        rev = d == 1
        cols = slice(d * SSD_HEADS, (d + 1) * SSD_HEADS)
        yc, hc = ssd_scan(xc, dt_ctx[..., cols], bc, cc, a_log[d], dt_bias[d], h0, rev)
        yl, _ = ssd_scan(xl, dt_lat[..., cols], bl, cl, a_log[d], dt_bias[d], hc, rev)
        y_lat = y_lat + yl
        y_ctx = y_ctx + yc
    y_lat = column_to_raster(y_lat.reshape(b, l, SSD_WIDTH)).astype(z_lat.dtype)
    out_lat = rms_norm(y_lat * jax.nn.silu(z_lat), norm_g)
    out_ctx = None
    if with_ctx_out:
        y_ctx = y_ctx.reshape(b, xbc_ctx.shape[1], SSD_WIDTH).astype(z_ctx.dtype)
        out_ctx = rms_norm(y_ctx * jax.nn.silu(z_ctx), norm_g)
    return out_lat, out_ctx


def conv_ffn(h, w_up, conv_w, conv_b, w_down):
    u = dw_conv(h @ w_up, conv_w, conv_b)
    val, gate = jnp.split(u, 2, axis=-1)
    return (jax.nn.gelu(gate) * val) @ w_down


def _split_in_proj(p):
    i1 = LRU_WIDTH
    i2 = i1 + LRU_WIDTH
    i3 = i2 + SSD_WIDTH
    i4 = i3 + SSD_CONV_DIM
    return jnp.split(p, [i1, i2, i3, i4], axis=-1)


def hybrid_layer(x, xc, mod_lat, mod_ctx, lp, update_ctx):
    sh1, sc1, g1, sh2, sc2, g2 = jnp.split(mod_lat, 6, axis=-1)
    csh1, csc1, cg1, csh2, csc2, cg2 = jnp.split(mod_ctx, 6, axis=-1)
    h_lat = modulate(rms_norm(x, lp['norm1_g']), sh1, sc1)
    h_ctx = modulate(rms_norm(xc, lp['norm1_g']), csh1, csc1)
    lu_l, lg_l, z_l, xbc_l, dt_l = _split_in_proj(h_lat @ lp['w_in'])
    lu_c, lg_c, z_c, xbc_c, dt_c = _split_in_proj(h_ctx @ lp['w_in'])
    lru_l, lru_c = rglru_mixer(lu_l, lg_l, lu_c, lg_c, lp['lru_conv_w'], lp['lru_conv_b'],
                               lp['lru_wa'], lp['lru_ba'], lp['lru_wx'], lp['lru_bx'], lp['lru_lambda'], update_ctx)
    ssd_l, ssd_c = ssd_mixer(xbc_l, dt_l, z_l, xbc_c, dt_c, z_c, lp['ssd_conv_w'], lp['ssd_conv_b'],
                             lp['ssd_a_log'], lp['ssd_dt_bias'], lp['ssd_d'], lp['ssd_norm_g'], update_ctx)
    x = x + g1 * (jnp.concatenate([lru_l, ssd_l], axis=-1) @ lp['w_out'])
    f_lat = modulate(rms_norm(x, lp['norm2_g']), sh2, sc2)
    x = x + g2 * conv_ffn(f_lat, lp['ffn_w_up'], lp['ffn_conv_w'], lp['ffn_conv_b'], lp['ffn_w_down'])
    if update_ctx:
        xc = xc + cg1 * (jnp.concatenate([lru_c, ssd_c], axis=-1) @ lp['w_out'])
        f_ctx = modulate(rms_norm(xc, lp['norm2_g']), csh2, csc2)
        xc = xc + cg2 * conv_ffn(f_ctx, lp['ffn_w_up'], lp['ffn_conv_w'], lp['ffn_conv_b'], lp['ffn_w_down'])
    return x, xc


def setup_inputs(seed: int = 0) -> dict:
    key = jax.random.key(seed)
    ks = iter(jax.random.split(key, 40))
    f32 = jnp.float32
    nrm = lambda shape, s: jax.random.normal(next(ks), shape, f32) * s
    L = DEPTH
    a_c = jax.random.uniform(next(ks), (L, 2, LRU_WIDTH), f32, 0.9, 0.999)
    s = a_c ** (1.0 / LRU_C)
    lru_lambda = jnp.log(s) - jnp.log1p(-s)
    ssd_a_log = jnp.log(jax.random.uniform(next(ks), (L, 2, SSD_HEADS), f32, 1.0, 16.0))
    dt0 = jnp.exp(jax.random.uniform(next(ks), (L, 2, SSD_HEADS), f32, np.log(1e-3), np.log(0.1)))
    ssd_dt_bias = dt0 + jnp.log(-jnp.expm1(-dt0))
    return {
        'x': nrm((BATCH, SEQ, D_MODEL), 1.0),
        'c': nrm((BATCH, D_MODEL), 1.0),
        'ctx': nrm((BATCH, CTX_LEN, D_MODEL), 1.0),
        'c_ctx': nrm((D_MODEL,), 1.0),
        'ada_w': nrm((L, D_MODEL, 6 * D_MODEL), 0.5 * D_MODEL ** -0.5),
        'ada_b': nrm((L, 6 * D_MODEL), 0.01),
        'norm1_g': 1.0 + nrm((L, D_MODEL), 0.02),
        'w_in': nrm((L, D_MODEL, D_IN), D_MODEL ** -0.5),
        'lru_conv_w': nrm((L, LRU_CONV, LRU_WIDTH), LRU_CONV ** -0.5),
        'lru_conv_b': nrm((L, LRU_WIDTH), 0.01),
        'lru_wa': nrm((L, 2, LRU_HEADS, LRU_HEAD_DIM, LRU_HEAD_DIM), LRU_HEAD_DIM ** -0.5),
        'lru_ba': nrm((L, 2, LRU_WIDTH), 0.01),
        'lru_wx': nrm((L, 2, LRU_HEADS, LRU_HEAD_DIM, LRU_HEAD_DIM), LRU_HEAD_DIM ** -0.5),
        'lru_bx': nrm((L, 2, LRU_WIDTH), 0.01),
        'lru_lambda': lru_lambda,
        'ssd_conv_w': nrm((L, SSD_CONV, SSD_CONV_DIM), SSD_CONV ** -0.5),
        'ssd_conv_b': nrm((L, SSD_CONV_DIM), 0.01),
        'ssd_a_log': ssd_a_log,
        'ssd_dt_bias': ssd_dt_bias,
        'ssd_d': 1.0 + nrm((L, SSD_HEADS), 0.1),
        'ssd_norm_g': 1.0 + nrm((L, SSD_WIDTH), 0.02),
        'w_out': nrm((L, D_MIX, D_MODEL), D_MIX ** -0.5),
        'norm2_g': 1.0 + nrm((L, D_MODEL), 0.02),
        'ffn_w_up': nrm((L, D_MODEL, 2 * D_FF), D_MODEL ** -0.5),
        'ffn_conv_w': nrm((L, FFN_CONV, 2 * D_FF), FFN_CONV ** -0.5),
        'ffn_conv_b': nrm((L, 2 * D_FF), 0.01),
        'ffn_w_down': nrm((L, D_FF, D_MODEL), D_FF ** -0.5),
        'final_norm_g': 1.0 + nrm((D_MODEL,), 0.02),
    }


def reference(x, c, ctx, c_ctx, ada_w, ada_b, norm1_g, w_in, lru_conv_w, lru_conv_b, lru_wa, lru_ba,
              lru_wx, lru_bx, lru_lambda, ssd_conv_w, ssd_conv_b, ssd_a_log, ssd_dt_bias, ssd_d,
              ssd_norm_g, w_out, norm2_g, ffn_w_up, ffn_conv_w, ffn_conv_b, ffn_w_down, final_norm_g):
    silu_c = jax.nn.silu(c)
    silu_cc = jax.nn.silu(c_ctx)
    xc = ctx
    for i in range(DEPTH):
        mod_lat = (silu_c @ ada_w[i] + ada_b[i])[:, None, :]
        mod_ctx = silu_cc @ ada_w[i] + ada_b[i]
        lp = dict(norm1_g=norm1_g[i], w_in=w_in[i], lru_conv_w=lru_conv_w[i], lru_conv_b=lru_conv_b[i],
                  lru_wa=lru_wa[i], lru_ba=lru_ba[i], lru_wx=lru_wx[i], lru_bx=lru_bx[i],
                  lru_lambda=lru_lambda[i], ssd_conv_w=ssd_conv_w[i], ssd_conv_b=ssd_conv_b[i],
                  ssd_a_log=ssd_a_log[i], ssd_dt_bias=ssd_dt_bias[i], ssd_d=ssd_d[i],
                  ssd_norm_g=ssd_norm_g[i], w_out=w_out[i], norm2_g=norm2_g[i], ffn_w_up=ffn_w_up[i],
                  ffn_conv_w=ffn_conv_w[i], ffn_conv_b=ffn_conv_b[i], ffn_w_down=ffn_w_down[i])
        x, xc = hybrid_layer(x, xc, mod_lat, mod_ctx, lp, i < DEPTH - 1)
    return rms_norm(x, final_norm_g)
```

```python
import functools

import jax
import jax.numpy as jnp
from jax import lax
from jax.experimental import pallas as pl
from jax.experimental.pallas import tpu as pltpu

F32 = jnp.float32
BF16 = jnp.bfloat16

EPS = 1e-6
GRID_W = 64
LRU_HEADS = 16
LRU_C = 8.0
SSD_HEADS = 16
SSD_HEAD_DIM = 64
SSD_GROUPS = 2
SSD_STATE = 128
SSD_CHUNK = 128
SUBLANES = 8
LANES = 128
VMEM_LIMIT = 56 * 1024 * 1024


def _cparams(sem):
    return pltpu.CompilerParams(dimension_semantics=sem, vmem_limit_bytes=VMEM_LIMIT)


def _split_bf16(v, terms):
    parts = []
    rem = v
    for _ in range(terms):
        p = rem.astype(BF16)
        parts.append(p)
        rem = rem - p.astype(F32)
    return parts


def _dot(a, b):
    return jnp.dot(a, b, preferred_element_type=F32)


def _mod_kernel(s_ref, w_ref, b_ref, o_ref):
    s = s_ref[...]
    s = s * jax.nn.sigmoid(s)
    s_hi, s_lo = _split_bf16(s, 2)
    w_hi, w_lo = _split_bf16(w_ref[...], 2)
    acc = _dot(s_hi, w_hi) + _dot(s_hi, w_lo) + _dot(s_lo, w_hi)
    o_ref[...] = acc + b_ref[...]


def _mod_call(s, w, b):
    rows, d = s.shape
    n = w.shape[1]
    nb = 1536
    return pl.pallas_call(
        _mod_kernel,
        grid=(n // nb,),
        in_specs=[pl.BlockSpec((rows, d), lambda j: (0, 0)),
                  pl.BlockSpec((d, nb), lambda j: (0, j)),
                  pl.BlockSpec((1, nb), lambda j: (0, j))],
        out_specs=pl.BlockSpec((rows, nb), lambda j: (0, j)),
        out_shape=jax.ShapeDtypeStruct((rows, n), F32),
        compiler_params=_cparams(("parallel",)),
        name="mod",
    )(s, w, b)


def _norm_mod(x, g, shift, scale):
    ms = jnp.mean(x * x, axis=-1, keepdims=True)
    y = x * lax.rsqrt(ms + EPS) * g
    return y * (1.0 + scale) + shift


def _inproj_kernel(x_ref, sh_ref, sc_ref, g_ref, *refs):
    n = len(refs) // 2
    w_refs, o_refs = refs[:n], refs[n:]
    h = _norm_mod(x_ref[...], g_ref[...], sh_ref[...], sc_ref[...]).astype(BF16)
    for w_ref, o_ref in zip(w_refs, o_refs):
        o_ref[...] = _dot(h, w_ref[...])


def _inproj_call(x, shift, scale, g, weights, tm):
    b, l, d = x.shape
    row = lambda: pl.BlockSpec((None, 1, d), lambda bi, i: (bi, 0, 0))
    in_specs = [pl.BlockSpec((None, tm, d), lambda bi, i: (bi, i, 0)), row(), row(),
                pl.BlockSpec((1, d), lambda bi, i: (0, 0))]
    in_specs += [pl.BlockSpec(w.shape, lambda bi, i: (0, 0)) for w in weights]
    out_specs = [pl.BlockSpec((None, tm, w.shape[1]), lambda bi, i: (bi, i, 0)) for w in weights]
    out_shape = [jax.ShapeDtypeStruct((b, l, w.shape[1]), F32) for w in weights]
    return pl.pallas_call(
        _inproj_kernel,
        grid=(b, l // tm),
        in_specs=in_specs,
        out_specs=out_specs,
        out_shape=out_shape,
        compiler_params=_cparams(("parallel", "parallel")),
        name="inproj",
    )(x, shift, scale, g, *weights)


def _fill_halo_buffer(xb_ref, prev_ref, main_ref, next_ref, has_prev, has_next, t):
    xb_ref[0:SUBLANES, :] = jnp.where(has_prev, prev_ref[...], 0.0)
    xb_ref[SUBLANES:SUBLANES + t, :] = main_ref[...]
    xb_ref[SUBLANES + t:SUBLANES + t + SUBLANES, :] = jnp.where(has_next, next_ref[...], 0.0)


LRU_RB = 64


def _lru_kernel(*refs, t, nt, rev, final):
    (lu_ref, prev_ref, next_ref, h0_ref, cw_ref, cb_ref, wg_ref, ba_ref, bx_ref, lam_ref) = refs[:10]
    pos = 10
    if final:
        hb_ref, lg_ref = refs[pos:pos + 2]
        pos += 2
    o_ref, hl_ref = refs[pos:pos + 2]
    xb_ref, a_ref, u_ref, hs_ref, carry_ref = refs[pos + 2:]

    i = pl.program_id(1)
    c = (nt - 1 - i) if rev else i

    @pl.when(i == 0)
    def _():
        carry_ref[...] = h0_ref[...]

    _fill_halo_buffer(xb_ref, prev_ref, lu_ref, next_ref, c > 0, c < nt - 1, t)

    cw = cw_ref[...]
    cb = cb_ref[...]
    ba = ba_ref[...]
    bx = bx_ref[...]
    log_decay = -LRU_C * jax.nn.softplus(-lam_ref[...])
    width = cw.shape[1]
    ngroups = wg_ref.shape[0]
    gw = width // ngroups

    def gate_block(rb, carry):
        r0 = pl.multiple_of(rb * LRU_RB, LRU_RB)
        win = xb_ref[pl.ds(r0, LRU_RB + 2 * SUBLANES), :]
        xc = cb
        for k in range(4):
            xc = xc + cw[k:k + 1, :] * win[SUBLANES - 2 + k:SUBLANES - 2 + k + LRU_RB, :]
        for g in range(ngroups):
            sl = slice(g * gw, (g + 1) * gw)
            xg = xc[:, sl]
            pre = _dot(xg.astype(BF16), wg_ref[g])
            gate_r = jax.nn.sigmoid(pre[:, :gw] + ba[:, sl])
            gate_i = jax.nn.sigmoid(pre[:, gw:] + bx[:, sl])
            a = jnp.exp(gate_r * log_decay[:, sl])
            u = jnp.sqrt(1.0 - a * a) * (gate_i * xg)
            a_ref[pl.ds(r0, LRU_RB), sl] = a
            u_ref[pl.ds(r0, LRU_RB), sl] = u
        return carry

    lax.fori_loop(0, t // LRU_RB, gate_block, 0)

    row = lax.broadcasted_iota(jnp.int32, (SUBLANES, width), 0)
    ngrp = t // SUBLANES

    def scan_block(jj, h):
        j = (ngrp - 1 - jj) if rev else jj
        r0 = pl.multiple_of(j * SUBLANES, SUBLANES)
        a8 = a_ref[pl.ds(r0, SUBLANES), :]
        u8 = u_ref[pl.ds(r0, SUBLANES), :]
        for s in (1, 2, 4):
            if rev:
                shift, m = SUBLANES - s, row < SUBLANES - s
            else:
                shift, m = s, row >= s
            a_sh = pltpu.roll(a8, shift, 0)
            u_sh = pltpu.roll(u8, shift, 0)
            u8 = jnp.where(m, a8 * u_sh + u8, u8)
            a8 = jnp.where(m, a8 * a_sh, a8)
        h8 = a8 * h + u8
        hs_ref[pl.ds(r0, SUBLANES), :] = h8
        return h8[0:1, :] if rev else h8[SUBLANES - 1:SUBLANES, :]

    h_last = lax.fori_loop(0, ngrp, scan_block, carry_ref[...])
    carry_ref[...] = h_last
    hl_ref[...] = h_last
    if final:
        o_ref[...] = (hs_ref[...] + hb_ref[...]) * jax.nn.gelu(lg_ref[...])
    else:
        o_ref[...] = hs_ref[...]


def _lru_call(lu, h0, cw, cb, wg, ba, bx, lam, *, t, rev, hb=None, lg=None):
    b, l, w = lu.shape
    nt = l // t
    final = hb is not None
    tb = t // SUBLANES
    nb8 = l // SUBLANES
    cidx = (lambda i: nt - 1 - i) if rev else (lambda i: i)
    main = lambda: pl.BlockSpec((None, t, w), lambda bi, i: (bi, cidx(i), 0))
    vec = lambda a: pl.BlockSpec(a.shape, lambda bi, i: (0,) * a.ndim)
    in_specs = [
        main(),
        pl.BlockSpec((None, SUBLANES, w), lambda bi, i: (bi, jnp.maximum(cidx(i) * tb - 1, 0), 0)),
        pl.BlockSpec((None, SUBLANES, w), lambda bi, i: (bi, jnp.minimum((cidx(i) + 1) * tb, nb8 - 1), 0)),
        pl.BlockSpec((None, 1, w), lambda bi, i: (bi, 0, 0)),
        vec(cw), vec(cb), vec(wg), vec(ba), vec(bx), vec(lam),
    ]
    args = [lu, lu, lu, h0, cw, cb, wg, ba, bx, lam]
    if final:
        in_specs += [main(), main()]
        args += [hb, lg]
    kern = functools.partial(_lru_kernel, t=t, nt=nt, rev=rev, final=final)
    return pl.pallas_call(
        kern,
        grid=(b, nt),
        in_specs=in_specs,
        out_specs=[main(), pl.BlockSpec((None, 1, w), lambda bi, i: (bi, 0, 0))],
        out_shape=[jax.ShapeDtypeStruct((b, l, w), F32), jax.ShapeDtypeStruct((b, 1, w), F32)],
        scratch_shapes=[pltpu.VMEM((t + 2 * SUBLANES, w), F32), pltpu.VMEM((t, w), F32),
                        pltpu.VMEM((t, w), F32), pltpu.VMEM((t, w), F32), pltpu.VMEM((1, w), F32)],
        compiler_params=_cparams(("parallel", "arbitrary")),
        name="lru_rev" if rev else "lru_fwd",
    )(*args)


def _ssd_kernel(*refs, nc, rev, mode):
    (xm_ref, xp_ref, xn_ref, dt_ref, h0_ref, cw_ref, cb_ref, bias_r_ref, bias_c_ref,
     alog_r_ref, alog_c_ref) = refs[:11]
    pos = 11
    if mode == "final":
        yb_ref, z_ref, dsk_ref, ng_ref = refs[pos:pos + 4]
        pos += 4
    if mode != "state":
        o_ref = refs[pos]
        pos += 1
    hl_ref = refs[pos]
    xb_ref, hst_ref = refs[pos + 1:]

    t = SSD_CHUNK
    width = SSD_HEADS * SSD_HEAD_DIM
    gn = SSD_STATE
    i = pl.program_id(1)
    c = (nc - 1 - i) if rev else i
    d = 1 if rev else 0
    tl = 0 if rev else t - 1

    @pl.when(i == 0)
    def _():
        hst_ref[...] = h0_ref[...]

    _fill_halo_buffer(xb_ref, xp_ref, xm_ref, xn_ref, c > 0, c < nc - 1, t)
    cw = cw_ref[...]
    xc = cb_ref[...]
    for k in range(4):
        xc = xc + cw[k:k + 1, :] * xb_ref[SUBLANES - 2 + k:SUBLANES - 2 + k + t, :]
    xc = xc * jax.nn.sigmoid(xc)
    x = xc[:, :width]
    bm = xc[:, width:width + SSD_GROUPS * gn]
    cm = xc[:, width + SSD_GROUPS * gn:]

    ti = lax.broadcasted_iota(jnp.int32, (t, t), 0)
    si = lax.broadcasted_iota(jnp.int32, (t, t), 1)
    inc = (si >= ti) if rev else (si <= ti)
    inc_b = jnp.where(inc, 1.0, 0.0).astype(BF16)
    inc_t_b = jnp.where((ti >= si) if rev else (ti <= si), 1.0, 0.0).astype(BF16)

    dtraw = dt_ref[...]
    dt = jax.nn.softplus(dtraw + bias_r_ref[...])
    da = dt * (-jnp.exp(alog_r_ref[...]))
    cs = sum(_dot(inc_b, p) for p in _split_bf16(da, 3))
    q0 = SSD_HEADS * d
    dtraw_t = dtraw.T[q0:q0 + SSD_HEADS, :]
    dt_t = jax.nn.softplus(dtraw_t + bias_c_ref[q0:q0 + SSD_HEADS, :])
    da_t = dt_t * (-jnp.exp(alog_c_ref[q0:q0 + SSD_HEADS, :]))
    cs_t = sum(_dot(p, inc_t_b) for p in _split_bf16(da_t, 3))
    w1_t = dt_t * jnp.exp(cs_t[:, tl:tl + 1] - cs_t)
    ecs = jnp.exp(cs)

    lane = lax.broadcasted_iota(jnp.int32, (t, LANES), 1)
    lo = lane < SSD_HEAD_DIM
    hg = SSD_HEADS // SSD_GROUPS
    gw = hg * SSD_HEAD_DIM

    if mode != "state":
        scores = [lax.dot_general(cm[:, g * gn:(g + 1) * gn].astype(BF16),
                                  bm[:, g * gn:(g + 1) * gn].astype(BF16),
                                  (((1,), (1,)), ((), ())), preferred_element_type=F32)
                  for g in range(SSD_GROUPS)]
        z_off = [_dot(cm[:, g * gn:(g + 1) * gn].astype(BF16),
                      hst_ref[:, g * gw:(g + 1) * gw].astype(BF16)) for g in range(SSD_GROUPS)]
    bm_t = [bm[:, g * gn:(g + 1) * gn].T for g in range(SSD_GROUPS)]

    ys = []
    for pr in range(SSD_HEADS // 2):
        g = (2 * pr) // hg
        xp = x[:, pr * LANES:(pr + 1) * LANES]
        rhs = jnp.concatenate([jnp.where(lo, xp, 0.0), jnp.where(lo, 0.0, xp)], axis=0).astype(BF16)
        lhs_s = jnp.concatenate([bm_t[g] * w1_t[2 * pr + e:2 * pr + e + 1, :] for e in range(2)],
                                axis=1).astype(BF16)
        if mode != "state":
            ms = []
            for e in range(2):
                hh = 2 * pr + e
                q = q0 + hh
                dec = jnp.exp(cs[:, q:q + 1] - cs_t[hh:hh + 1, :])
                lmat = jnp.where(inc, dec, 0.0) * dt_t[hh:hh + 1, :]
                ms.append((scores[g] * lmat).astype(BF16))
            lhs = jnp.concatenate([jnp.concatenate(ms, axis=1), lhs_s], axis=0)
            res = _dot(lhs, rhs)
            y_diag, s_new = res[:t], res[t:]
            e_pair = jnp.where(lo, ecs[:, q0 + 2 * pr:q0 + 2 * pr + 1], ecs[:, q0 + 2 * pr + 1:q0 + 2 * pr + 2])
            col = (pr * LANES) % gw
            ys.append(y_diag + z_off[g][:, col:col + LANES] * e_pair)
        else:
            s_new = _dot(lhs_s, rhs)
        dec_pair = jnp.where(lo[0:1, :], ecs[tl:tl + 1, q0 + 2 * pr:q0 + 2 * pr + 1],
                             ecs[tl:tl + 1, q0 + 2 * pr + 1:q0 + 2 * pr + 2])
        hst_ref[:, pr * LANES:(pr + 1) * LANES] = hst_ref[:, pr * LANES:(pr + 1) * LANES] * dec_pair + s_new

    @pl.when(i == nc - 1)
    def _():
        hl_ref[...] = hst_ref[...]

    if mode == "state":
        return
    y = jnp.concatenate(ys, axis=1)
    if mode == "y":
        o_ref[...] = y
    else:
        y = y + yb_ref[...] + x * dsk_ref[...]
        zz = z_ref[...]
        gated = y * (zz * jax.nn.sigmoid(zz))
        ms2 = jnp.mean(gated * gated, axis=-1, keepdims=True)
        o_ref[...] = gated * lax.rsqrt(ms2 + EPS) * ng_ref[...]


def _ssd_call(xbc, dt, h0, params, *, nc, rev, mode, column_major, yb=None, z=None, dsk=None, ng=None):
    cw, cb, bias_r, bias_c, alog_r, alog_c = params
    b = xbc.shape[0]
    t = SSD_CHUNK
    cdim = cw.shape[1]
    width = SSD_HEADS * SSD_HEAD_DIM
    cidx = (lambda i: nc - 1 - i) if rev else (lambda i: i)
    tb = t // SUBLANES
    if column_major:
        def blk(wd):
            return pl.BlockSpec((None, t, wd), lambda bi, i: (bi, 0, cidx(i)))
        prev = pl.BlockSpec((None, SUBLANES, cdim), lambda bi, i: (bi, tb - 1, jnp.maximum(cidx(i) - 1, 0)))
        nxt = pl.BlockSpec((None, SUBLANES, cdim), lambda bi, i: (bi, 0, jnp.minimum(cidx(i) + 1, nc - 1)))
        yshape = (b, t, nc * width)
    else:
        def blk(wd):
            return pl.BlockSpec((None, t, wd), lambda bi, i: (bi, cidx(i), 0))
        prev = pl.BlockSpec((None, SUBLANES, cdim), lambda bi, i: (bi, jnp.maximum(cidx(i) * tb - 1, 0), 0))
        nxt = pl.BlockSpec((None, SUBLANES, cdim),
                           lambda bi, i: (bi, jnp.minimum((cidx(i) + 1) * tb, nc * tb - 1), 0))
        yshape = (b, nc * t, width)
    vec = lambda a: pl.BlockSpec(a.shape, lambda bi, i: (0,) * a.ndim)
    state = lambda: pl.BlockSpec((None, SSD_STATE, width), lambda bi, i: (bi, 0, 0))
    in_specs = [blk(cdim), prev, nxt, blk(LANES), state(), vec(cw), vec(cb), vec(bias_r), vec(bias_c),
                vec(alog_r), vec(alog_c)]
    args = [xbc, xbc, xbc, dt, h0, cw, cb, bias_r, bias_c, alog_r, alog_c]
    if mode == "final":
        in_specs += [blk(width), blk(width), vec(dsk), vec(ng)]
        args += [yb, z, dsk, ng]
    if mode == "state":
        out_specs = [state()]
        out_shape = [jax.ShapeDtypeStruct((b, SSD_STATE, width), F32)]
    else:
        out_specs = [blk(width), state()]
        out_shape = [jax.ShapeDtypeStruct(yshape, F32), jax.ShapeDtypeStruct((b, SSD_STATE, width), F32)]
    kern = functools.partial(_ssd_kernel, nc=nc, rev=rev, mode=mode)
    return pl.pallas_call(
        kern,
        grid=(b, nc),
        in_specs=in_specs,
        out_specs=out_specs,
        out_shape=out_shape,
        scratch_shapes=[pltpu.VMEM((t + 2 * SUBLANES, cdim), F32), pltpu.VMEM((SSD_STATE, width), F32)],
        compiler_params=_cparams(("parallel", "arbitrary")),
        name=f"ssd_{mode}_{'rev' if rev else 'fwd'}",
    )(*args)


def _outproj_kernel(x_ref, lru_ref, ssd_ref, g_ref, w1_ref, w2_ref, o_ref):
    mix = _dot(lru_ref[...].astype(BF16), w1_ref[...]) + _dot(ssd_ref[...].astype(BF16), w2_ref[...])
    o_ref[...] = x_ref[...] + g_ref[...] * mix


def _outproj_call(x, lru, ssd, gate, w1, w2, tm):
    b, l, d = x.shape
    tok = lambda wd: pl.BlockSpec((None, tm, wd), lambda bi, i: (bi, i, 0))
    return pl.pallas_call(
        _outproj_kernel,
        grid=(b, l // tm),
        in_specs=[tok(d), tok(lru.shape[2]), tok(ssd.shape[2]),
                  pl.BlockSpec((None, 1, d), lambda bi, i: (bi, 0, 0)),
                  pl.BlockSpec(w1.shape, lambda bi, i: (0, 0)),
                  pl.BlockSpec(w2.shape, lambda bi, i: (0, 0))],
        out_specs=tok(d),
        out_shape=jax.ShapeDtypeStruct((b, l, d), F32),
        compiler_params=_cparams(("parallel", "parallel")),
        name="outproj",
    )(x, lru, ssd, gate, w1, w2)


def _ffn_kernel(xm_ref, xp_ref, xn_ref, sh_ref, sc_ref, gt_ref, ng_ref, fg_ref,
                wv_ref, wg_ref, cwv_ref, cwg_ref, cbv_ref, cbg_ref, wd_ref, o_ref,
                f_ref, uv_ref, ug_ref, acc_ref, *, tm, nt, nj):
    i = pl.program_id(1)
    j = pl.program_id(2)

    @pl.when(j == 0)
    def _():
        ng, sh, sc = ng_ref[...], sh_ref[...], sc_ref[...]
        fp = _norm_mod(xp_ref[...], ng, sh, sc)
        fn = _norm_mod(xn_ref[...], ng, sh, sc)
        f_ref[0:SUBLANES, :] = jnp.where(i > 0, fp, 0.0).astype(BF16)
        f_ref[SUBLANES:SUBLANES + tm, :] = _norm_mod(xm_ref[...], ng, sh, sc).astype(BF16)
        f_ref[SUBLANES + tm:, :] = jnp.where(i < nt - 1, fn, 0.0).astype(BF16)
        acc_ref[...] = jnp.zeros_like(acc_ref)

    f = f_ref[...]
    uv_ref[...] = _dot(f, wv_ref[...])
    ug_ref[...] = _dot(f, wg_ref[...])

    def conv(u_ref, cw_ref, cb_ref):
        cw = cw_ref[...]
        out = cb_ref[...]
        for k in range(3):
            out = out + cw[k:k + 1, :] * u_ref[SUBLANES - 1 + k:SUBLANES - 1 + k + tm, :]
        return out

    act = jax.nn.gelu(conv(ug_ref, cwg_ref, cbg_ref)) * conv(uv_ref, cwv_ref, cbv_ref)
    acc_ref[...] += _dot(act.astype(BF16), wd_ref[...])

    @pl.when(j == nj - 1)
    def _():
        x2 = xm_ref[...] + gt_ref[...] * acc_ref[...]
        ms = jnp.mean(x2 * x2, axis=-1, keepdims=True)
        o_ref[...] = x2 * lax.rsqrt(ms + EPS) * fg_ref[...]


def _ffn_call(x1, shift, scale, gate, norm_g, final_g, w_up, conv_w, conv_b, w_down, tm, fb):
    b, l, d = x1.shape
    dff = w_down.shape[0]
    nt = l // tm
    nj = dff // fb
    tb = tm // SUBLANES
    nb8 = l // SUBLANES
    row = lambda: pl.BlockSpec((None, 1, d), lambda bi, i, j: (bi, 0, 0))
    vec = lambda: pl.BlockSpec((1, d), lambda bi, i, j: (0, 0))
    in_specs = [
        pl.BlockSpec((None, tm, d), lambda bi, i, j: (bi, i, 0)),
        pl.BlockSpec((None, SUBLANES, d), lambda bi, i, j: (bi, jnp.maximum(i * tb - 1, 0), 0)),
        pl.BlockSpec((None, SUBLANES, d), lambda bi, i, j: (bi, jnp.minimum((i + 1) * tb, nb8 - 1), 0)),
        row(), row(), row(), vec(), vec(),
        pl.BlockSpec((d, fb), lambda bi, i, j: (0, j)),
        pl.BlockSpec((d, fb), lambda bi, i, j: (0, nj + j)),
        pl.BlockSpec((3, fb), lambda bi, i, j: (0, j)),
        pl.BlockSpec((3, fb), lambda bi, i, j: (0, nj + j)),
        pl.BlockSpec((1, fb), lambda bi, i, j: (0, j)),
        pl.BlockSpec((1, fb), lambda bi, i, j: (0, nj + j)),
        pl.BlockSpec((fb, d), lambda bi, i, j: (j, 0)),
    ]
    kern = functools.partial(_ffn_kernel, tm=tm, nt=nt, nj=nj)
    return pl.pallas_call(
        kern,
        grid=(b, nt, nj),
        in_specs=in_specs,
        out_specs=pl.BlockSpec((None, tm, d), lambda bi, i, j: (bi, i, 0)),
        out_shape=jax.ShapeDtypeStruct((b, l, d), F32),
        scratch_shapes=[pltpu.VMEM((tm + 2 * SUBLANES, d), BF16),
                        pltpu.VMEM((tm + 2 * SUBLANES, fb), F32),
                        pltpu.VMEM((tm + 2 * SUBLANES, fb), F32),
                        pltpu.VMEM((tm, d), F32)],
        compiler_params=_cparams(("parallel", "parallel", "arbitrary")),
        name="ffn",
    )(x1, x1, x1, shift, scale, gate, norm_g, final_g, w_up, w_up, conv_w, conv_w, conv_b, conv_b, w_down)


def _block_diag_gates(wa, wx, heads_per_group):
    h, hd, _ = wa.shape
    ng = h // heads_per_group
    eye = jnp.eye(heads_per_group, dtype=wa.dtype)

    def bd(w):
        w = w.reshape(ng, heads_per_group, hd, hd)
        return jnp.einsum('gaij,ab->gaibj', w, eye).reshape(ng, heads_per_group * hd, heads_per_group * hd)

    return jnp.concatenate([bd(wa), bd(wx)], axis=-1).astype(BF16)


def _pad_lanes(v, n=LANES):
    return jnp.pad(v, ((0, 0), (0, n - v.shape[1])))


def kernel(x, c, ctx, c_ctx, ada_w, ada_b, norm1_g, w_in, lru_conv_w, lru_conv_b, lru_wa, lru_ba, lru_wx, lru_bx,
           lru_lambda, ssd_conv_w, ssd_conv_b, ssd_a_log, ssd_dt_bias, ssd_d, ssd_norm_g, w_out, norm2_g,
           ffn_w_up, ffn_conv_w, ffn_conv_b, ffn_w_down, final_norm_g):
    b, l, d = x.shape
    lctx = ctx.shape[1]
    lw = lru_conv_w.shape[2]
    sw = SSD_HEADS * SSD_HEAD_DIM
    cdim = ssd_conv_w.shape[2]
    assert ada_w.shape[0] == 1, "single layer"
    assert l // GRID_W == SSD_CHUNK, "an SSD chunk is one column of the latent grid"
    assert lctx % SSD_CHUNK == 0

    s_in = jnp.zeros((SUBLANES, d), F32).at[:b].set(c).at[b].set(c_ctx)
    mod = _mod_call(s_in, ada_w[0], ada_b)
    mod_lat = [m.reshape(b, 1, d) for m in jnp.split(mod[:b], 6, axis=-1)]
    mod_ctx = [jnp.broadcast_to(m.reshape(1, 1, d), (b, 1, d)) for m in jnp.split(mod[b:b + 1], 6, axis=-1)]
    sh1, sc1, g1, sh2, sc2, g2 = mod_lat
    csh1, csc1 = mod_ctx[0], mod_ctx[1]

    wi = w_in[0].astype(BF16)
    o1, o2, o3, o4 = lw, 2 * lw, 2 * lw + sw, 2 * lw + sw + cdim
    w_parts = [wi[:, :o1], wi[:, o1:o2], wi[:, o2:o3], wi[:, o3:o4], _pad_lanes(wi[:, o4:])]
    n1 = norm1_g
    lu_l, lg_l, z_l, xbc_l, dt_l = _inproj_call(x, sh1, sc1, n1, w_parts, 256)
    lu_c, _, _, xbc_c, dt_c = _inproj_call(ctx, csh1, csc1, n1, w_parts, lctx)

    hpg = 4
    lcw, lcb = lru_conv_w[0], lru_conv_b
    zeros_w = jnp.zeros((b, 1, lw), F32)
    lru_args = []
    for dr in range(2):
        lru_args.append((lcw, lcb, _block_diag_gates(lru_wa[0, dr], lru_wx[0, dr], hpg),
                         lru_ba[0, dr][None], lru_bx[0, dr][None], lru_lambda[0, dr][None]))
    _, hc_f = _lru_call(lu_c, zeros_w, *lru_args[0], t=lctx, rev=False)
    _, hc_b = _lru_call(lu_c, zeros_w, *lru_args[1], t=lctx, rev=True)
    h_b, _ = _lru_call(lu_l, hc_b, *lru_args[1], t=512, rev=True)
    lru_out, _ = _lru_call(lu_l, hc_f, *lru_args[0], t=512, rev=False, hb=h_b, lg=lg_l)

    rows = l // GRID_W
    nc = GRID_W
    ssd_params = (ssd_conv_w[0], ssd_conv_b,
                  _pad_lanes(ssd_dt_bias[0].reshape(1, -1)), _pad_lanes(ssd_dt_bias[0].reshape(1, -1)).T,
                  _pad_lanes(ssd_a_log[0].reshape(1, -1)), _pad_lanes(ssd_a_log[0].reshape(1, -1)).T)
    xbc_cm = xbc_l.reshape(b, rows, GRID_W * cdim)
    dt_cm = dt_l.reshape(b, rows, GRID_W * LANES)
    z_cm = z_l.reshape(b, rows, GRID_W * sw)
    zero_state = jnp.zeros((b, SSD_STATE, sw), F32)
    nctx = lctx // SSD_CHUNK
    (sc_f,) = _ssd_call(xbc_c, dt_c, zero_state, ssd_params, nc=nctx, rev=False, mode="state", column_major=False)
    (sc_b,) = _ssd_call(xbc_c, dt_c, zero_state, ssd_params, nc=nctx, rev=True, mode="state", column_major=False)
    y_b, _ = _ssd_call(xbc_cm, dt_cm, sc_b, ssd_params, nc=nc, rev=True, mode="y", column_major=True)
    dsk = jnp.repeat(ssd_d[0], SSD_HEAD_DIM)[None]
    ssd_out, _ = _ssd_call(xbc_cm, dt_cm, sc_f, ssd_params, nc=nc, rev=False, mode="final", column_major=True,
                           yb=y_b, z=z_cm, dsk=dsk, ng=ssd_norm_g)
    ssd_out = ssd_out.reshape(b, l, sw)

    wo = w_out[0].astype(BF16)
    x1 = _outproj_call(x, lru_out, ssd_out, g1, wo[:lw], wo[lw:], 512)
    return _ffn_call(x1, sh2, sc2, g2, norm2_g, final_norm_g[None], ffn_w_up[0].astype(BF16), ffn_conv_w[0],
                     ffn_conv_b, ffn_w_down[0].astype(BF16), 512, 512)
```

```python
import functools

import jax
import jax.numpy as jnp
from jax import lax
from jax.experimental import pallas as pl
from jax.experimental.pallas import tpu as pltpu

F32 = jnp.float32
BF16 = jnp.bfloat16

EPS = 1e-6
GRID_W = 64
LRU_C = 8.0
SSD_HEADS = 16
SSD_HEAD_DIM = 64
SSD_GROUPS = 2
SSD_STATE = 128
SSD_CHUNK = 128
SUBLANES = 8
LANES = 128
VMEM_LIMIT = 56 * 1024 * 1024
TILE = SUBLANES * GRID_W


def _cparams(sem):
    return pltpu.CompilerParams(dimension_semantics=sem, vmem_limit_bytes=VMEM_LIMIT)


def _split_bf16(v, terms):
    parts = []
    rem = v
    for _ in range(terms):
        p = rem.astype(BF16)
        parts.append(p)
        rem = rem - p.astype(F32)
    return parts


def _dot(a, b):
    return jnp.dot(a, b, preferred_element_type=F32)


def _sigmoid(x):
    return 0.5 * jnp.tanh(0.5 * x) + 0.5


def _silu(x):
    h = 0.5 * x
    return h * jnp.tanh(h) + h


def _gather_groups(ref):
    return jnp.concatenate([ref[:, s, :] for s in range(SUBLANES)], axis=0)


def _scatter_groups(ref, val):
    seg = val.shape[0] // SUBLANES
    for s in range(SUBLANES):
        ref[:, s, :] = val[s * seg:(s + 1) * seg, :]


def _mod_kernel(s_ref, w_ref, b_ref, o_ref):
    s = _silu(s_ref[...])
    s_hi, s_lo = _split_bf16(s, 2)
    w_hi, w_lo = _split_bf16(w_ref[...], 2)
    acc = _dot(s_hi, w_hi) + _dot(s_hi, w_lo) + _dot(s_lo, w_hi)
    o_ref[...] = acc + b_ref[...]


def _mod_call(s, w, b):
    rows, d = s.shape
    n = w.shape[1]
    nb = 1536
    return pl.pallas_call(
        _mod_kernel,
        grid=(n // nb,),
        in_specs=[pl.BlockSpec((rows, d), lambda j: (0, 0)),
                  pl.BlockSpec((d, nb), lambda j: (0, j)),
                  pl.BlockSpec((1, nb), lambda j: (0, j))],
        out_specs=pl.BlockSpec((rows, nb), lambda j: (0, j)),
        out_shape=jax.ShapeDtypeStruct((rows, n), F32),
        compiler_params=_cparams(("parallel",)),
        name="mod",
    )(s, w, b)


def _norm_mod(x, g, shift, scale):
    ms = jnp.mean(x * x, axis=-1, keepdims=True)
    y = x * lax.rsqrt(ms + EPS) * g
    return y * (1.0 + scale) + shift


def _inproj_kernel(x_ref, sh_ref, sc_ref, g_ref, *refs, kinds):
    n = len(kinds)
    w_refs, o_refs = refs[:n], refs[n:]
    h = _norm_mod(x_ref[...], g_ref[...], sh_ref[...], sc_ref[...]).astype(BF16)
    for w_ref, o_ref, kind in zip(w_refs, o_refs, kinds):
        res = _dot(h, w_ref[...])
        if kind == "raster":
            o_ref[...] = res
        elif kind == "chunks":
            rows = o_ref.shape[1]
            for ci in range(o_ref.shape[0]):
                o_ref[ci] = res[ci * rows:(ci + 1) * rows, :]
        else:
            _scatter_groups(o_ref, res)


def _inproj_call(x, shift, scale, g, outs, tm):
    b, l, d = x.shape
    row = lambda: pl.BlockSpec((None, 1, d), lambda bi, i: (bi, 0, 0))
    in_specs = [pl.BlockSpec((None, tm, d), lambda bi, i: (bi, i, 0)), row(), row(),
                pl.BlockSpec((1, d), lambda bi, i: (0, 0))]
    in_specs += [pl.BlockSpec(o[0].shape, lambda bi, i: (0, 0)) for o in outs]
    out_specs, out_shape = [], []
    for _, _, shape, block, imap in outs:
        out_specs.append(pl.BlockSpec((None,) + block, functools.partial(lambda bi, i, f: (bi,) + f(i), f=imap)))
        out_shape.append(jax.ShapeDtypeStruct((b,) + shape, F32))
    kern = functools.partial(_inproj_kernel, kinds=tuple(o[1] for o in outs))
    return pl.pallas_call(
        kern,
        grid=(b, l // tm),
        in_specs=in_specs,
        out_specs=out_specs,
        out_shape=out_shape,
        compiler_params=_cparams(("parallel", "parallel")),
        name="inproj",
    )(x, shift, scale, g, *[o[0] for o in outs])


LRU_RB = 64
SQRT_FLOOR = 1e-30
HALO = 3 * SUBLANES


def _lru_kernel(*refs, seg, nt, rev, final):
    (lu_ref, prev_ref, next_ref, h0_ref, cw_ref, cb_ref, wg_ref, ba_ref, bx_ref, lam_ref) = refs[:10]
    pos = 10
    if final:
        hb_ref, lg_ref = refs[pos:pos + 2]
        pos += 2
    o_ref, hl_ref = refs[pos:pos + 2]
    xe_ref, a_ref, u_ref, st_ref, carry_ref = refs[pos + 2:]

    t = seg * SUBLANES
    i = pl.program_id(1)
    c = (nt - 1 - i) if rev else i

    @pl.when(i == 0)
    def _():
        carry_ref[...] = h0_ref[...]

    width = cw_ref.shape[1]
    row = lax.broadcasted_iota(jnp.int32, (SUBLANES, width), 0)

    def before(own, other):
        return jnp.where(row == 0, pltpu.roll(jnp.where(c > 0, other, 0.0), 1, 0), pltpu.roll(own, 1, 0))

    xe_ref[0:SUBLANES, :] = before(lu_ref[seg - 2], prev_ref[0])
    xe_ref[SUBLANES:2 * SUBLANES, :] = before(lu_ref[seg - 1], prev_ref[1])
    xe_ref[2 * SUBLANES:2 * SUBLANES + t, :] = lu_ref[...].reshape(t, width)
    xe_ref[2 * SUBLANES + t:HALO + t, :] = jnp.where(
        row == SUBLANES - 1, pltpu.roll(jnp.where(c < nt - 1, next_ref[0], 0.0), SUBLANES - 1, 0),
        pltpu.roll(lu_ref[0], SUBLANES - 1, 0))

    cw = cw_ref[...]
    cb = cb_ref[...]
    ba = ba_ref[...]
    bx = bx_ref[...]
    log_decay = -LRU_C * jax.nn.softplus(-lam_ref[...])
    ngroups = wg_ref.shape[0]
    gw = width // ngroups

    def gate_block(rb, carry):
        r0 = pl.multiple_of(rb * LRU_RB, LRU_RB)
        xc = cb
        for k in range(4):
            xc = xc + cw[k:k + 1, :] * xe_ref[pl.ds(r0 + k * SUBLANES, LRU_RB), :]
        for g in range(ngroups):
            sl = slice(g * gw, (g + 1) * gw)
            xg = xc[:, sl]
            pre = _dot(xg.astype(BF16), wg_ref[g])
            gate_r = _sigmoid(pre[:, :gw] + ba[:, sl])
            gate_i = _sigmoid(pre[:, gw:] + bx[:, sl])
            a = jnp.exp(gate_r * log_decay[:, sl])
            y = 1.0 - a * a
            u = (y * lax.rsqrt(jnp.maximum(y, SQRT_FLOOR))) * (gate_i * xg)
            a_ref[pl.ds(r0, LRU_RB), sl] = a
            u_ref[pl.ds(r0, LRU_RB), sl] = u
        return carry

    lax.fori_loop(0, t // LRU_RB, gate_block, 0)

    def slab(jj):
        j = (seg - 1 - jj) if rev else jj
        return j, pl.multiple_of(j * SUBLANES, SUBLANES)

    def seg_totals(jj, hp):
        h, p = hp
        _, r0 = slab(jj)
        a8 = a_ref[pl.ds(r0, SUBLANES), :]
        return a8 * h + u_ref[pl.ds(r0, SUBLANES), :], p * a8

    h_end, p_end = lax.fori_loop(0, seg, seg_totals,
                                 (jnp.zeros((SUBLANES, width), F32), jnp.ones((SUBLANES, width), F32)), unroll=4)

    cur = carry_ref[...]
    for r in (range(SUBLANES - 1, -1, -1) if rev else range(SUBLANES)):
        st_ref[r:r + 1, :] = cur
        cur = p_end[r:r + 1, :] * cur + h_end[r:r + 1, :]
    carry_ref[...] = cur
    hl_ref[...] = cur

    def emit(jj, h):
        j, r0 = slab(jj)
        h = a_ref[pl.ds(r0, SUBLANES), :] * h + u_ref[pl.ds(r0, SUBLANES), :]
        if final:
            o_ref[j] = (h + hb_ref[j]) * jax.nn.gelu(lg_ref[j])
        else:
            o_ref[j] = h
        return h

    lax.fori_loop(0, seg, emit, st_ref[...], unroll=4)


def _lru_call(lu, h0, cw, cb, wg, ba, bx, lam, *, seg, rev, hb=None, lg=None):
    b, n8, _, w = lu.shape
    nt = n8 // seg
    final = hb is not None
    cidx = (lambda i: nt - 1 - i) if rev else (lambda i: i)
    main = lambda: pl.BlockSpec((None, seg, SUBLANES, w), lambda bi, i: (bi, cidx(i), 0, 0))
    vec = lambda a: pl.BlockSpec(a.shape, lambda bi, i: (0,) * a.ndim)
    in_specs = [
        main(),
        pl.BlockSpec((None, 2, SUBLANES, w), lambda bi, i: (bi, jnp.maximum(cidx(i) * (seg // 2) - 1, 0), 0, 0)),
        pl.BlockSpec((None, 1, SUBLANES, w), lambda bi, i: (bi, jnp.minimum((cidx(i) + 1) * seg, n8 - 1), 0, 0)),
        pl.BlockSpec((None, 1, w), lambda bi, i: (bi, 0, 0)),
        vec(cw), vec(cb), vec(wg), vec(ba), vec(bx), vec(lam),
    ]
    args = [lu, lu, lu, h0, cw, cb, wg, ba, bx, lam]
    if final:
        in_specs += [main(), main()]
        args += [hb, lg]
    t = seg * SUBLANES
    kern = functools.partial(_lru_kernel, seg=seg, nt=nt, rev=rev, final=final)
    return pl.pallas_call(
        kern,
        grid=(b, nt),
        in_specs=in_specs,
        out_specs=[main(), pl.BlockSpec((None, 1, w), lambda bi, i: (bi, 0, 0))],
        out_shape=[jax.ShapeDtypeStruct(lu.shape, F32), jax.ShapeDtypeStruct((b, 1, w), F32)],
        scratch_shapes=[pltpu.VMEM((t + HALO, w), F32), pltpu.VMEM((t, w), F32), pltpu.VMEM((t, w), F32),
                        pltpu.VMEM((SUBLANES, w), F32), pltpu.VMEM((1, w), F32)],
        compiler_params=_cparams(("parallel", "arbitrary")),
        name="lru_rev" if rev else "lru_fwd",
    )(*args)


def _ssd_kernel(*refs, nc, rev, mode):
    (xm_ref, xp_ref, xn_ref, dt_ref, h0_ref, cw_ref, cb_ref, bias_r_ref, bias_c_ref,
     alog_r_ref, alog_c_ref) = refs[:11]
    pos = 11
    if mode == "final":
        yb_ref, dsk_ref = refs[pos:pos + 2]
        pos += 2
    if mode != "state":
        o_ref = refs[pos]
        pos += 1
    hl_ref = refs[pos]
    xb_ref, hst_ref = refs[pos + 1:]

    t = SSD_CHUNK
    width = SSD_HEADS * SSD_HEAD_DIM
    gn = SSD_STATE
    i = pl.program_id(1)
    c = (nc - 1 - i) if rev else i
    d = 1 if rev else 0
    tl = 0 if rev else t - 1

    @pl.when(i == 0)
    def _():
        hst_ref[...] = h0_ref[...]

    xb_ref[0:SUBLANES, :] = jnp.where(c > 0, xp_ref[...], 0.0)
    xb_ref[SUBLANES:SUBLANES + t, :] = xm_ref[...]
    xb_ref[SUBLANES + t:, :] = jnp.where(c < nc - 1, xn_ref[...], 0.0)
    cw = cw_ref[...]
    xc = cb_ref[...]
    for k in range(4):
        xc = xc + cw[k:k + 1, :] * xb_ref[SUBLANES - 2 + k:SUBLANES - 2 + k + t, :]
    xc = _silu(xc)
    x = xc[:, :width]
    bm = xc[:, width:width + SSD_GROUPS * gn]
    cm = xc[:, width + SSD_GROUPS * gn:]

    ti = lax.broadcasted_iota(jnp.int32, (t, t), 0)
    si = lax.broadcasted_iota(jnp.int32, (t, t), 1)
    inc = (si >= ti) if rev else (si <= ti)
    inc_b = jnp.where(inc, 1.0, 0.0).astype(BF16)
    inc_t_b = jnp.where((ti >= si) if rev else (ti <= si), 1.0, 0.0).astype(BF16)

    dtraw = dt_ref[...]
    dt = jax.nn.softplus(dtraw + bias_r_ref[...])
    da = dt * (-jnp.exp(alog_r_ref[...]))
    cs = sum(_dot(inc_b, p) for p in _split_bf16(da, 3))
    q0 = SSD_HEADS * d
    dtraw_t = dtraw.T[q0:q0 + SSD_HEADS, :]
    dt_t = jax.nn.softplus(dtraw_t + bias_c_ref[q0:q0 + SSD_HEADS, :])
    da_t = dt_t * (-jnp.exp(alog_c_ref[q0:q0 + SSD_HEADS, :]))
    cs_t = sum(_dot(p, inc_t_b) for p in _split_bf16(da_t, 3))
    w1_t = dt_t * jnp.exp(cs_t[:, tl:tl + 1] - cs_t)
    ecs = jnp.exp(cs)

    lane = lax.broadcasted_iota(jnp.int32, (t, LANES), 1)
    lo = lane < SSD_HEAD_DIM
    hg = SSD_HEADS // SSD_GROUPS
    gw = hg * SSD_HEAD_DIM

    if mode != "state":
        scores = [lax.dot_general(cm[:, g * gn:(g + 1) * gn].astype(BF16),
                                  bm[:, g * gn:(g + 1) * gn].astype(BF16),
                                  (((1,), (1,)), ((), ())), preferred_element_type=F32)
                  for g in range(SSD_GROUPS)]
        z_off = [_dot(cm[:, g * gn:(g + 1) * gn].astype(BF16),
                      hst_ref[:, g * gw:(g + 1) * gw].astype(BF16)) for g in range(SSD_GROUPS)]
    bm_t = [bm[:, g * gn:(g + 1) * gn].T for g in range(SSD_GROUPS)]

    ys = []
    for pr in range(SSD_HEADS // 2):
        g = (2 * pr) // hg
        xp = x[:, pr * LANES:(pr + 1) * LANES]
        rhs = jnp.concatenate([jnp.where(lo, xp, 0.0), jnp.where(lo, 0.0, xp)], axis=0).astype(BF16)
        lhs_s = jnp.concatenate([bm_t[g] * w1_t[2 * pr + e:2 * pr + e + 1, :] for e in range(2)],
                                axis=1).astype(BF16)
        if mode != "state":
            ms = []
            for e in range(2):
                hh = 2 * pr + e
                q = q0 + hh
                dec = jnp.exp(cs[:, q:q + 1] - cs_t[hh:hh + 1, :])
                lmat = jnp.where(inc, dec, 0.0) * dt_t[hh:hh + 1, :]
                ms.append((scores[g] * lmat).astype(BF16))
            lhs = jnp.concatenate([jnp.concatenate(ms, axis=1), lhs_s], axis=0)
            res = _dot(lhs, rhs)
            y_diag, s_new = res[:t], res[t:]
            e_pair = jnp.where(lo, ecs[:, q0 + 2 * pr:q0 + 2 * pr + 1], ecs[:, q0 + 2 * pr + 1:q0 + 2 * pr + 2])
            col = (pr * LANES) % gw
            ys.append(y_diag + z_off[g][:, col:col + LANES] * e_pair)
        else:
            s_new = _dot(lhs_s, rhs)
        dec_pair = jnp.where(lo[0:1, :], ecs[tl:tl + 1, q0 + 2 * pr:q0 + 2 * pr + 1],
                             ecs[tl:tl + 1, q0 + 2 * pr + 1:q0 + 2 * pr + 2])
        hst_ref[:, pr * LANES:(pr + 1) * LANES] = hst_ref[:, pr * LANES:(pr + 1) * LANES] * dec_pair + s_new

    @pl.when(i == nc - 1)
    def _():
        hl_ref[...] = hst_ref[...]

    if mode == "state":
        return
    y = jnp.concatenate(ys, axis=1)
    if mode == "final":
        y = y + yb_ref[...] + x * dsk_ref[...]
    o_ref[...] = y


def _ssd_call(xbc, dt, h0, params, *, rev, mode, yb=None, dsk=None):
    cw, cb, bias_r, bias_c, alog_r, alog_c = params
    b, nc, t, cdim = xbc.shape
    width = SSD_HEADS * SSD_HEAD_DIM
    cidx = (lambda i: nc - 1 - i) if rev else (lambda i: i)
    tb = t // SUBLANES
    blk = lambda wd: pl.BlockSpec((None, None, t, wd), lambda bi, i: (bi, cidx(i), 0, 0))
    prev = pl.BlockSpec((None, None, SUBLANES, cdim), lambda bi, i: (bi, jnp.maximum(cidx(i) - 1, 0), tb - 1, 0))
    nxt = pl.BlockSpec((None, None, SUBLANES, cdim), lambda bi, i: (bi, jnp.minimum(cidx(i) + 1, nc - 1), 0, 0))
    vec = lambda a: pl.BlockSpec(a.shape, lambda bi, i: (0,) * a.ndim)
    state = lambda: pl.BlockSpec((None, SSD_STATE, width), lambda bi, i: (bi, 0, 0))
    in_specs = [blk(cdim), prev, nxt, blk(LANES), state(), vec(cw), vec(cb), vec(bias_r), vec(bias_c),
                vec(alog_r), vec(alog_c)]
    args = [xbc, xbc, xbc, dt, h0, cw, cb, bias_r, bias_c, alog_r, alog_c]
    if mode == "final":
        in_specs += [blk(width), vec(dsk)]
        args += [yb, dsk]
    out_specs = [state()]
    out_shape = [jax.ShapeDtypeStruct((b, SSD_STATE, width), F32)]
    if mode != "state":
        out_specs = [blk(width)] + out_specs
        out_shape = [jax.ShapeDtypeStruct((b, nc, t, width), F32)] + out_shape
    kern = functools.partial(_ssd_kernel, nc=nc, rev=rev, mode=mode)
    return pl.pallas_call(
        kern,
        grid=(b, nc),
        in_specs=in_specs,
        out_specs=out_specs,
        out_shape=out_shape,
        scratch_shapes=[pltpu.VMEM((t + 2 * SUBLANES, cdim), F32), pltpu.VMEM((SSD_STATE, width), F32)],
        compiler_params=_cparams(("parallel", "arbitrary")),
        name=f"ssd_{mode}_{'rev' if rev else 'fwd'}",
    )(*args)


def _outproj_kernel(x_ref, lru_ref, y_ref, z_ref, g_ref, ng_ref, w1_ref, w2_ref, o_ref):
    lru = _gather_groups(lru_ref)
    y = _gather_groups(y_ref)
    zz = z_ref[...]
    gated = y * _silu(zz)
    ms = jnp.mean(gated * gated, axis=-1, keepdims=True)
    ssd = gated * lax.rsqrt(ms + EPS) * ng_ref[...]
    mix = _dot(lru.astype(BF16), w1_ref[...]) + _dot(ssd.astype(BF16), w2_ref[...])
    o_ref[...] = x_ref[...] + g_ref[...] * mix


def _outproj_call(x, lru, y, z, gate, norm_g, w1, w2):
    b, l, d = x.shape
    seg = TILE // SUBLANES
    tok = lambda wd: pl.BlockSpec((None, TILE, wd), lambda bi, i: (bi, i, 0))
    return pl.pallas_call(
        _outproj_kernel,
        grid=(b, l // TILE),
        in_specs=[tok(d),
                  pl.BlockSpec((None, seg, SUBLANES, lru.shape[3]), lambda bi, i: (bi, i, 0, 0)),
                  pl.BlockSpec((None, GRID_W, SUBLANES, y.shape[3]), lambda bi, i: (bi, 0, i, 0)),
                  tok(z.shape[2]),
                  pl.BlockSpec((None, 1, d), lambda bi, i: (bi, 0, 0)),
                  pl.BlockSpec(norm_g.shape, lambda bi, i: (0, 0)),
                  pl.BlockSpec(w1.shape, lambda bi, i: (0, 0)),
                  pl.BlockSpec(w2.shape, lambda bi, i: (0, 0))],
        out_specs=tok(d),
        out_shape=jax.ShapeDtypeStruct((b, l, d), F32),
        compiler_params=_cparams(("parallel", "parallel")),
        name="outproj",
    )(x, lru, y, z, gate, norm_g, w1, w2)


def _ffn_kernel(xm_ref, xp_ref, xn_ref, sh_ref, sc_ref, gt_ref, ng_ref, fg_ref,
                wv_ref, wg_ref, cwv_ref, cwg_ref, cbv_ref, cbg_ref, wd_ref, o_ref,
                f_ref, uv_ref, ug_ref, acc_ref, *, tm, nt, nj):
    i = pl.program_id(1)
    j = pl.program_id(2)

    @pl.when(j == 0)
    def _():
        ng, sh, sc = ng_ref[...], sh_ref[...], sc_ref[...]
        fp = _norm_mod(xp_ref[...], ng, sh, sc)
        fn = _norm_mod(xn_ref[...], ng, sh, sc)
        f_ref[0:SUBLANES, :] = jnp.where(i > 0, fp, 0.0).astype(BF16)
        f_ref[SUBLANES:SUBLANES + tm, :] = _norm_mod(xm_ref[...], ng, sh, sc).astype(BF16)
        f_ref[SUBLANES + tm:, :] = jnp.where(i < nt - 1, fn, 0.0).astype(BF16)
        acc_ref[...] = jnp.zeros_like(acc_ref)

    f = f_ref[...]
    uv_ref[...] = _dot(f, wv_ref[...])
    ug_ref[...] = _dot(f, wg_ref[...])

    def conv(u_ref, cw_ref, cb_ref):
        cw = cw_ref[...]
        out = cb_ref[...]
        for k in range(3):
            out = out + cw[k:k + 1, :] * u_ref[SUBLANES - 1 + k:SUBLANES - 1 + k + tm, :]
        return out

    act = jax.nn.gelu(conv(ug_ref, cwg_ref, cbg_ref)) * conv(uv_ref, cwv_ref, cbv_ref)
    acc_ref[...] += _dot(act.astype(BF16), wd_ref[...])

    @pl.when(j == nj - 1)
    def _():
        x2 = xm_ref[...] + gt_ref[...] * acc_ref[...]
        ms = jnp.mean(x2 * x2, axis=-1, keepdims=True)
        o_ref[...] = x2 * lax.rsqrt(ms + EPS) * fg_ref[...]


def _ffn_call(x1, shift, scale, gate, norm_g, final_g, w_up, conv_w, conv_b, w_down, tm, fb):
    b, l, d = x1.shape
    dff = w_down.shape[0]
    nt = l // tm
    nj = dff // fb
    tb = tm // SUBLANES
    nb8 = l // SUBLANES
    row = lambda: pl.BlockSpec((None, 1, d), lambda bi, i, j: (bi, 0, 0))
    vec = lambda: pl.BlockSpec((1, d), lambda bi, i, j: (0, 0))
    in_specs = [
        pl.BlockSpec((None, tm, d), lambda bi, i, j: (bi, i, 0)),
        pl.BlockSpec((None, SUBLANES, d), lambda bi, i, j: (bi, jnp.maximum(i * tb - 1, 0), 0)),
        pl.BlockSpec((None, SUBLANES, d), lambda bi, i, j: (bi, jnp.minimum((i + 1) * tb, nb8 - 1), 0)),
        row(), row(), row(), vec(), vec(),
        pl.BlockSpec((d, fb), lambda bi, i, j: (0, j)),
        pl.BlockSpec((d, fb), lambda bi, i, j: (0, nj + j)),
        pl.BlockSpec((3, fb), lambda bi, i, j: (0, j)),
        pl.BlockSpec((3, fb), lambda bi, i, j: (0, nj + j)),
        pl.BlockSpec((1, fb), lambda bi, i, j: (0, j)),
        pl.BlockSpec((1, fb), lambda bi, i, j: (0, nj + j)),
        pl.BlockSpec((fb, d), lambda bi, i, j: (j, 0)),
    ]
    kern = functools.partial(_ffn_kernel, tm=tm, nt=nt, nj=nj)
    return pl.pallas_call(
        kern,
        grid=(b, nt, nj),
        in_specs=in_specs,
        out_specs=pl.BlockSpec((None, tm, d), lambda bi, i, j: (bi, i, 0)),
        out_shape=jax.ShapeDtypeStruct((b, l, d), F32),
        scratch_shapes=[pltpu.VMEM((tm + 2 * SUBLANES, d), BF16),
                        pltpu.VMEM((tm + 2 * SUBLANES, fb), F32),
                        pltpu.VMEM((tm + 2 * SUBLANES, fb), F32),
                        pltpu.VMEM((tm, d), F32)],
        compiler_params=_cparams(("parallel", "parallel", "arbitrary")),
        name="ffn",
    )(x1, x1, x1, shift, scale, gate, norm_g, final_g, w_up, w_up, conv_w, conv_w, conv_b, conv_b, w_down)


def _block_diag_gates(wa, wx, heads_per_group):
    h, hd, _ = wa.shape
    ng = h // heads_per_group
    eye = jnp.eye(heads_per_group, dtype=wa.dtype)

    def bd(w):
        w = w.reshape(ng, heads_per_group, hd, hd)
        return jnp.einsum('gaij,ab->gaibj', w, eye).reshape(ng, heads_per_group * hd, heads_per_group * hd)

    return jnp.concatenate([bd(wa), bd(wx)], axis=-1).astype(BF16)


def _pad_lanes(v, n=LANES):
    return jnp.pad(v, ((0, 0), (0, n - v.shape[1])))


def kernel(x, c, ctx, c_ctx, ada_w, ada_b, norm1_g, w_in, lru_conv_w, lru_conv_b, lru_wa, lru_ba, lru_wx, lru_bx,
           lru_lambda, ssd_conv_w, ssd_conv_b, ssd_a_log, ssd_dt_bias, ssd_d, ssd_norm_g, w_out, norm2_g,
           ffn_w_up, ffn_conv_w, ffn_conv_b, ffn_w_down, final_norm_g):
    b, l, d = x.shape
    lctx = ctx.shape[1]
    lw = lru_conv_w.shape[2]
    sw = SSD_HEADS * SSD_HEAD_DIM
    cdim = ssd_conv_w.shape[2]
    rows = l // GRID_W
    assert ada_w.shape[0] == 1, "single layer"
    assert rows == SSD_CHUNK, "an SSD chunk is one column of the latent grid"
    assert lctx % SSD_CHUNK == 0 and l % TILE == 0

    s_in = jnp.zeros((SUBLANES, d), F32).at[:b].set(c).at[b].set(c_ctx)
    mod = _mod_call(s_in, ada_w[0], ada_b)
    mod_lat = [m.reshape(b, 1, d) for m in jnp.split(mod[:b], 6, axis=-1)]
    mod_ctx = [jnp.broadcast_to(m.reshape(1, 1, d), (b, 1, d)) for m in jnp.split(mod[b:b + 1], 6, axis=-1)]
    sh1, sc1, g1, sh2, sc2, g2 = mod_lat
    csh1, csc1 = mod_ctx[0], mod_ctx[1]

    wi = w_in[0].astype(BF16)
    o1, o2, o3, o4 = lw, 2 * lw, 2 * lw + sw, 2 * lw + sw + cdim
    w_lu, w_lg, w_z, w_xbc, w_dt = wi[:, :o1], wi[:, o1:o2], wi[:, o2:o3], wi[:, o3:o4], _pad_lanes(wi[:, o4:])
    seg = TILE // SUBLANES
    interleaved = lambda wd: ("groups", (l // SUBLANES, SUBLANES, wd), (seg, SUBLANES, wd), lambda i: (i, 0, 0))
    column = lambda wd: ("groups", (GRID_W, rows, wd), (GRID_W, SUBLANES, wd), lambda i: (0, i, 0))
    lu_l, lg_l, z_l, xbc_l, dt_l = _inproj_call(x, sh1, sc1, norm1_g, [
        (w_lu,) + interleaved(lw), (w_lg,) + interleaved(lw),
        (w_z, "raster", (l, sw), (TILE, sw), lambda i: (i, 0)),
        (w_xbc,) + column(cdim), (w_dt,) + column(LANES)], TILE)
    cseg = lctx // SUBLANES
    nctx = lctx // SSD_CHUNK
    chunks = lambda wd: ("chunks", (nctx, SSD_CHUNK, wd), (nctx, SSD_CHUNK, wd), lambda i: (0, 0, 0))
    lu_c, xbc_c, dt_c = _inproj_call(ctx, csh1, csc1, norm1_g, [
        (w_lu, "groups", (cseg, SUBLANES, lw), (cseg, SUBLANES, lw), lambda i: (0, 0, 0)),
        (w_xbc,) + chunks(cdim), (w_dt,) + chunks(LANES)], lctx)

    hpg = 4
    lcw, lcb = lru_conv_w[0], lru_conv_b
    zeros_w = jnp.zeros((b, 1, lw), F32)
    lru_args = []
    for dr in range(2):
        lru_args.append((lcw, lcb, _block_diag_gates(lru_wa[0, dr], lru_wx[0, dr], hpg),
                         lru_ba[0, dr][None], lru_bx[0, dr][None], lru_lambda[0, dr][None]))
    _, hc_f = _lru_call(lu_c, zeros_w, *lru_args[0], seg=cseg, rev=False)
    _, hc_b = _lru_call(lu_c, zeros_w, *lru_args[1], seg=cseg, rev=True)
    h_b, _ = _lru_call(lu_l, hc_b, *lru_args[1], seg=seg, rev=True)
    lru_out, _ = _lru_call(lu_l, hc_f, *lru_args[0], seg=seg, rev=False, hb=h_b, lg=lg_l)

    ssd_params = (ssd_conv_w[0], ssd_conv_b,
                  _pad_lanes(ssd_dt_bias[0].reshape(1, -1)), _pad_lanes(ssd_dt_bias[0].reshape(1, -1)).T,
                  _pad_lanes(ssd_a_log[0].reshape(1, -1)), _pad_lanes(ssd_a_log[0].reshape(1, -1)).T)
    zero_state = jnp.zeros((b, SSD_STATE, sw), F32)
    (sc_f,) = _ssd_call(xbc_c, dt_c, zero_state, ssd_params, rev=False, mode="state")
    (sc_b,) = _ssd_call(xbc_c, dt_c, zero_state, ssd_params, rev=True, mode="state")
    y_b, _ = _ssd_call(xbc_l, dt_l, sc_b, ssd_params, rev=True, mode="y")
    dsk = jnp.repeat(ssd_d[0], SSD_HEAD_DIM)[None]
    y_l, _ = _ssd_call(xbc_l, dt_l, sc_f, ssd_params, rev=False, mode="final", yb=y_b, dsk=dsk)

    wo = w_out[0].astype(BF16)
    x1 = _outproj_call(x, lru_out, y_l, z_l, g1, ssd_norm_g, wo[:lw], wo[lw:])
    return _ffn_call(x1, sh2, sc2, g2, norm2_g, final_norm_g[None], ffn_w_up[0].astype(BF16), ffn_conv_w[0],
                     ffn_conv_b, ffn_w_down[0].astype(BF16), 512, 512)
```

```python
import functools

import jax
import jax.numpy as jnp
from jax import lax
from jax.experimental import pallas as pl
from jax.experimental.pallas import tpu as pltpu

F32 = jnp.float32
BF16 = jnp.bfloat16

EPS = 1e-6
GRID_W = 64
LRU_C = 8.0
SSD_HEADS = 16
SSD_HEAD_DIM = 64
SSD_GROUPS = 2
SSD_STATE = 128
SSD_CHUNK = 128
SUBLANES = 8
LANES = 128
VMEM_LIMIT = 56 * 1024 * 1024
TILE = SUBLANES * GRID_W


def _cparams(sem):
    return pltpu.CompilerParams(dimension_semantics=sem, vmem_limit_bytes=VMEM_LIMIT)


def _split_bf16(v, terms):
    parts = []
    rem = v
    for _ in range(terms):
        p = rem.astype(BF16)
        parts.append(p)
        rem = rem - p.astype(F32)
    return parts


def _dot(a, b):
    return jnp.dot(a, b, preferred_element_type=F32)


def _sigmoid(x):
    return 0.5 * jnp.tanh(0.5 * x) + 0.5


def _silu(x):
    h = 0.5 * x
    return h * jnp.tanh(h) + h


def _gather_groups(ref):
    return jnp.concatenate([ref[:, s, :] for s in range(SUBLANES)], axis=0)


def _scatter_groups(ref, val):
    seg = val.shape[0] // SUBLANES
    for s in range(SUBLANES):
        ref[:, s, :] = val[s * seg:(s + 1) * seg, :]


def _mod_kernel(s_ref, w_ref, b_ref, o_ref):
    s = _silu(s_ref[...])
    s_hi, s_lo = _split_bf16(s, 2)
    w_hi, w_lo = _split_bf16(w_ref[...], 2)
    acc = _dot(s_hi, w_hi) + _dot(s_hi, w_lo) + _dot(s_lo, w_hi)
    o_ref[...] = acc + b_ref[...]


def _mod_call(s, w, b):
    rows, d = s.shape
    n = w.shape[1]
    nb = 1536
    return pl.pallas_call(
        _mod_kernel,
        grid=(n // nb,),
        in_specs=[pl.BlockSpec((rows, d), lambda j: (0, 0)),
                  pl.BlockSpec((d, nb), lambda j: (0, j)),
                  pl.BlockSpec((1, nb), lambda j: (0, j))],
        out_specs=pl.BlockSpec((rows, nb), lambda j: (0, j)),
        out_shape=jax.ShapeDtypeStruct((rows, n), F32),
        compiler_params=_cparams(("parallel",)),
        name="mod",
    )(s, w, b)


def _norm_mod(x, g, shift, scale):
    ms = jnp.mean(x * x, axis=-1, keepdims=True)
    y = x * lax.rsqrt(ms + EPS) * g
    return y * (1.0 + scale) + shift


def _inproj_kernel(x_ref, sh_ref, sc_ref, g_ref, *refs, kinds):
    n = len(kinds)
    w_refs, o_refs = refs[:n], refs[n:]
    h = _norm_mod(x_ref[...], g_ref[...], sh_ref[...], sc_ref[...]).astype(BF16)
    for w_ref, o_ref, kind in zip(w_refs, o_refs, kinds):
        res = _dot(h, w_ref[...])
        if kind == "raster":
            o_ref[...] = res
        elif kind == "chunks":
            rows = o_ref.shape[1]
            for ci in range(o_ref.shape[0]):
                o_ref[ci] = res[ci * rows:(ci + 1) * rows, :]
        else:
            _scatter_groups(o_ref, res)


def _inproj_call(x, shift, scale, g, outs, tm):
    b, l, d = x.shape
    row = lambda: pl.BlockSpec((None, 1, d), lambda bi, i: (bi, 0, 0))
    in_specs = [pl.BlockSpec((None, tm, d), lambda bi, i: (bi, i, 0)), row(), row(),
                pl.BlockSpec((1, d), lambda bi, i: (0, 0))]
    in_specs += [pl.BlockSpec(o[0].shape, lambda bi, i: (0, 0)) for o in outs]
    out_specs, out_shape = [], []
    for _, _, shape, block, imap in outs:
        out_specs.append(pl.BlockSpec((None,) + block, functools.partial(lambda bi, i, f: (bi,) + f(i), f=imap)))
        out_shape.append(jax.ShapeDtypeStruct((b,) + shape, F32))
    kern = functools.partial(_inproj_kernel, kinds=tuple(o[1] for o in outs))
    return pl.pallas_call(
        kern,
        grid=(b, l // tm),
        in_specs=in_specs,
        out_specs=out_specs,
        out_shape=out_shape,
        compiler_params=_cparams(("parallel", "parallel")),
        name="inproj",
    )(x, shift, scale, g, *[o[0] for o in outs])


LRU_RB = 64
SQRT_FLOOR = 1e-30
HALO = 3 * SUBLANES


def _lru_kernel(*refs, seg, nt, rev, final):
    (lu_ref, prev_ref, next_ref, h0_ref, cw_ref, cb_ref, wg_ref, ba_ref, bx_ref, lam_ref) = refs[:10]
    pos = 10
    if final:
        hb_ref, lg_ref = refs[pos:pos + 2]
        pos += 2
    o_ref, hl_ref = refs[pos:pos + 2]
    xe_ref, a_ref, u_ref, st_ref, carry_ref = refs[pos + 2:]

    t = seg * SUBLANES
    i = pl.program_id(1)
    c = (nt - 1 - i) if rev else i

    @pl.when(i == 0)
    def _():
        carry_ref[...] = h0_ref[...]

    width = cw_ref.shape[1]
    row = lax.broadcasted_iota(jnp.int32, (SUBLANES, width), 0)

    def before(own, other):
        return jnp.where(row == 0, pltpu.roll(jnp.where(c > 0, other, 0.0), 1, 0), pltpu.roll(own, 1, 0))

    xe_ref[0:SUBLANES, :] = before(lu_ref[seg - 2], prev_ref[0])
    xe_ref[SUBLANES:2 * SUBLANES, :] = before(lu_ref[seg - 1], prev_ref[1])
    xe_ref[2 * SUBLANES:2 * SUBLANES + t, :] = lu_ref[...].reshape(t, width)
    xe_ref[2 * SUBLANES + t:HALO + t, :] = jnp.where(
        row == SUBLANES - 1, pltpu.roll(jnp.where(c < nt - 1, next_ref[0], 0.0), SUBLANES - 1, 0),
        pltpu.roll(lu_ref[0], SUBLANES - 1, 0))

    cw = cw_ref[...]
    cb = cb_ref[...]
    ba = ba_ref[...]
    bx = bx_ref[...]
    log_decay = -LRU_C * jax.nn.softplus(-lam_ref[...])
    ngroups = wg_ref.shape[0]
    gw = width // ngroups

    def gate_block(rb, carry):
        r0 = pl.multiple_of(rb * LRU_RB, LRU_RB)
        xc = cb
        for k in range(4):
            xc = xc + cw[k:k + 1, :] * xe_ref[pl.ds(r0 + k * SUBLANES, LRU_RB), :]
        for g in range(ngroups):
            sl = slice(g * gw, (g + 1) * gw)
            xg = xc[:, sl]
            pre = _dot(xg.astype(BF16), wg_ref[g])
            gate_r = _sigmoid(pre[:, :gw] + ba[:, sl])
            gate_i = _sigmoid(pre[:, gw:] + bx[:, sl])
            a = jnp.exp(gate_r * log_decay[:, sl])
            y = 1.0 - a * a
            u = (y * lax.rsqrt(jnp.maximum(y, SQRT_FLOOR))) * (gate_i * xg)
            a_ref[pl.ds(r0, LRU_RB), sl] = a
            u_ref[pl.ds(r0, LRU_RB), sl] = u
        return carry

    lax.fori_loop(0, t // LRU_RB, gate_block, 0)

    def slab(jj):
        j = (seg - 1 - jj) if rev else jj
        return j, pl.multiple_of(j * SUBLANES, SUBLANES)

    def seg_totals(jj, hp):
        h, p = hp
        _, r0 = slab(jj)
        a8 = a_ref[pl.ds(r0, SUBLANES), :]
        return a8 * h + u_ref[pl.ds(r0, SUBLANES), :], p * a8

    h_end, p_end = lax.fori_loop(0, seg, seg_totals,
                                 (jnp.zeros((SUBLANES, width), F32), jnp.ones((SUBLANES, width), F32)), unroll=4)

    cur = carry_ref[...]
    for r in (range(SUBLANES - 1, -1, -1) if rev else range(SUBLANES)):
        st_ref[r:r + 1, :] = cur
        cur = p_end[r:r + 1, :] * cur + h_end[r:r + 1, :]
    carry_ref[...] = cur
    hl_ref[...] = cur

    def emit(jj, h):
        j, r0 = slab(jj)
        h = a_ref[pl.ds(r0, SUBLANES), :] * h + u_ref[pl.ds(r0, SUBLANES), :]
        if final:
            o_ref[j] = (h + hb_ref[j]) * jax.nn.gelu(lg_ref[j])
        else:
            o_ref[j] = h
        return h

    lax.fori_loop(0, seg, emit, st_ref[...], unroll=4)


def _lru_call(lu, h0, cw, cb, wg, ba, bx, lam, *, seg, rev, hb=None, lg=None):
    b, n8, _, w = lu.shape
    nt = n8 // seg
    final = hb is not None
    cidx = (lambda i: nt - 1 - i) if rev else (lambda i: i)
    main = lambda: pl.BlockSpec((None, seg, SUBLANES, w), lambda bi, i: (bi, cidx(i), 0, 0))
    vec = lambda a: pl.BlockSpec(a.shape, lambda bi, i: (0,) * a.ndim)
    in_specs = [
        main(),
        pl.BlockSpec((None, 2, SUBLANES, w), lambda bi, i: (bi, jnp.maximum(cidx(i) * (seg // 2) - 1, 0), 0, 0)),
        pl.BlockSpec((None, 1, SUBLANES, w), lambda bi, i: (bi, jnp.minimum((cidx(i) + 1) * seg, n8 - 1), 0, 0)),
        pl.BlockSpec((None, 1, w), lambda bi, i: (bi, 0, 0)),
        vec(cw), vec(cb), vec(wg), vec(ba), vec(bx), vec(lam),
    ]
    args = [lu, lu, lu, h0, cw, cb, wg, ba, bx, lam]
    if final:
        in_specs += [main(), main()]
        args += [hb, lg]
    t = seg * SUBLANES
    kern = functools.partial(_lru_kernel, seg=seg, nt=nt, rev=rev, final=final)
    return pl.pallas_call(
        kern,
        grid=(b, nt),
        in_specs=in_specs,
        out_specs=[main(), pl.BlockSpec((None, 1, w), lambda bi, i: (bi, 0, 0))],
        out_shape=[jax.ShapeDtypeStruct(lu.shape, F32), jax.ShapeDtypeStruct((b, 1, w), F32)],
        scratch_shapes=[pltpu.VMEM((t + HALO, w), F32), pltpu.VMEM((t, w), F32), pltpu.VMEM((t, w), F32),
                        pltpu.VMEM((SUBLANES, w), F32), pltpu.VMEM((1, w), F32)],
        compiler_params=_cparams(("parallel", "arbitrary")),
        name="lru_rev" if rev else "lru_fwd",
    )(*args)


def _ssd_kernel(*refs, nc, rev, mode):
    if mode == "second":
        xc_ref, dt_ref, h0_ref, bias_r_ref, bias_c_ref, alog_r_ref, alog_c_ref, yb_ref, o_ref, hl_ref, hst_ref = refs
    else:
        (xm_ref, xp_ref, xn_ref, dt_ref, h0_ref, cw_ref, cb_ref, bias_r_ref, bias_c_ref,
         alog_r_ref, alog_c_ref) = refs[:11]
        if mode == "first":
            dsk_ref, o_ref, xc_ref, hl_ref, xb_ref, hst_ref = refs[11:]
        else:
            hl_ref, xb_ref, hst_ref = refs[11:]

    t = SSD_CHUNK
    width = SSD_HEADS * SSD_HEAD_DIM
    gn = SSD_STATE
    i = pl.program_id(1)
    c = (nc - 1 - i) if rev else i
    d = 1 if rev else 0
    tl = 0 if rev else t - 1

    @pl.when(i == 0)
    def _():
        hst_ref[...] = h0_ref[...]

    slabs = []
    for s in range((width + 2 * SSD_GROUPS * gn) // LANES):
        ls = slice(s * LANES, (s + 1) * LANES)
        if mode == "second":
            slabs.append(xc_ref[:, ls])
            continue
        xb_ref[s, 0:SUBLANES, :] = jnp.where(c > 0, xp_ref[:, ls], 0.0)
        xb_ref[s, SUBLANES:SUBLANES + t, :] = xm_ref[:, ls]
        xb_ref[s, SUBLANES + t:, :] = jnp.where(c < nc - 1, xn_ref[:, ls], 0.0)
        xc = cb_ref[:, ls]
        for k in range(4):
            xc = xc + cw_ref[k:k + 1, ls] * xb_ref[s, SUBLANES - 2 + k:SUBLANES - 2 + k + t, :]
        slabs.append(_silu(xc))
        if mode == "first":
            xc_ref[:, ls] = slabs[-1].astype(BF16)
    nx = width // LANES
    x_slabs = slabs[:nx]
    bm = slabs[nx:nx + SSD_GROUPS]
    cm = slabs[nx + SSD_GROUPS:]

    ti = lax.broadcasted_iota(jnp.int32, (t, t), 0)
    si = lax.broadcasted_iota(jnp.int32, (t, t), 1)
    inc = (si >= ti) if rev else (si <= ti)
    inc_b = jnp.where(inc, 1.0, 0.0).astype(BF16)
    inc_t_b = jnp.where((ti >= si) if rev else (ti <= si), 1.0, 0.0).astype(BF16)

    dtraw = dt_ref[...]
    dt = jax.nn.softplus(dtraw + bias_r_ref[...])
    da = dt * (-jnp.exp(alog_r_ref[...]))
    cs = sum(_dot(inc_b, p) for p in _split_bf16(da, 3))
    q0 = SSD_HEADS * d
    dtraw_t = dtraw.T[q0:q0 + SSD_HEADS, :]
    dt_t = jax.nn.softplus(dtraw_t + bias_c_ref[q0:q0 + SSD_HEADS, :])
    da_t = dt_t * (-jnp.exp(alog_c_ref[q0:q0 + SSD_HEADS, :]))
    cs_t = sum(_dot(p, inc_t_b) for p in _split_bf16(da_t, 3))
    w1_t = dt_t * jnp.exp(cs_t[:, tl:tl + 1] - cs_t)
    ecs = jnp.exp(cs)

    lane = lax.broadcasted_iota(jnp.int32, (t, LANES), 1)
    lo = lane < SSD_HEAD_DIM
    hg = SSD_HEADS // SSD_GROUPS
    gw = hg * SSD_HEAD_DIM

    if mode != "state":
        scores = [lax.dot_general(cm[g].astype(BF16), bm[g].astype(BF16),
                                  (((1,), (1,)), ((), ())), preferred_element_type=F32)
                  for g in range(SSD_GROUPS)]
        z_off = [_dot(cm[g].astype(BF16), hst_ref[:, g * gw:(g + 1) * gw].astype(BF16))
                 for g in range(SSD_GROUPS)]
    bm_t = [bm[g].astype(F32).T for g in range(SSD_GROUPS)]
    keep_lo = jnp.where(lo, 1.0, 0.0).astype(BF16)
    keep_hi = jnp.where(lo, 0.0, 1.0).astype(BF16)

    ys = []
    for pr in range(SSD_HEADS // 2):
        g = (2 * pr) // hg
        xp = x_slabs[pr].astype(BF16)
        rhs = jnp.concatenate([xp * keep_lo, xp * keep_hi], axis=0)
        lhs_s = jnp.concatenate([bm_t[g] * w1_t[2 * pr + e:2 * pr + e + 1, :] for e in range(2)],
                                axis=1).astype(BF16)
        if mode != "state":
            ms = []
            for e in range(2):
                hh = 2 * pr + e
                q = q0 + hh
                dec = jnp.exp(cs[:, q:q + 1] - cs_t[hh:hh + 1, :])
                lmat = jnp.where(inc, dec, 0.0) * dt_t[hh:hh + 1, :]
                ms.append((scores[g] * lmat).astype(BF16))
            lhs = jnp.concatenate([jnp.concatenate(ms, axis=1), lhs_s], axis=0)
            res = _dot(lhs, rhs)
            y_diag, s_new = res[:t], res[t:]
            e_pair = jnp.where(lo, ecs[:, q0 + 2 * pr:q0 + 2 * pr + 1], ecs[:, q0 + 2 * pr + 1:q0 + 2 * pr + 2])
            col = (pr * LANES) % gw
            ys.append(y_diag + z_off[g][:, col:col + LANES] * e_pair)
        else:
            s_new = _dot(lhs_s, rhs)
        dec_pair = jnp.where(lo[0:1, :], ecs[tl:tl + 1, q0 + 2 * pr:q0 + 2 * pr + 1],
                             ecs[tl:tl + 1, q0 + 2 * pr + 1:q0 + 2 * pr + 2])
        hst_ref[:, pr * LANES:(pr + 1) * LANES] = hst_ref[:, pr * LANES:(pr + 1) * LANES] * dec_pair + s_new

    @pl.when(i == nc - 1)
    def _():
        hl_ref[...] = hst_ref[...]

    if mode == "state":
        return
    y = jnp.concatenate(ys, axis=1)
    if mode == "first":
        o_ref[...] = y + jnp.concatenate(x_slabs, axis=1) * dsk_ref[...]
    else:
        o_ref[...] = y + yb_ref[...]


def _ssd_call(xin, dt, h0, params, *, rev, mode, yb=None, dsk=None):
    cw, cb, bias_r, bias_c, alog_r, alog_c = params
    b, nc, t, cdim = xin.shape
    width = SSD_HEADS * SSD_HEAD_DIM
    cidx = (lambda i: nc - 1 - i) if rev else (lambda i: i)
    tb = t // SUBLANES
    blk = lambda wd: pl.BlockSpec((None, None, t, wd), lambda bi, i: (bi, cidx(i), 0, 0))
    prev = pl.BlockSpec((None, None, SUBLANES, cdim), lambda bi, i: (bi, jnp.maximum(cidx(i) - 1, 0), tb - 1, 0))
    nxt = pl.BlockSpec((None, None, SUBLANES, cdim), lambda bi, i: (bi, jnp.minimum(cidx(i) + 1, nc - 1), 0, 0))
    vec = lambda a: pl.BlockSpec(a.shape, lambda bi, i: (0,) * a.ndim)
    state = lambda: pl.BlockSpec((None, SSD_STATE, width), lambda bi, i: (bi, 0, 0))
    small = [bias_r, bias_c, alog_r, alog_c]
    y_shape = jax.ShapeDtypeStruct((b, nc, t, width), F32)
    h_shape = jax.ShapeDtypeStruct((b, SSD_STATE, width), F32)
    scratch = [pltpu.VMEM((SSD_STATE, width), F32)]
    if mode == "second":
        in_specs = [blk(cdim), blk(LANES), state()] + [vec(a) for a in small] + [blk(width)]
        args = [xin, dt, h0] + small + [yb]
        out_specs, out_shape = [blk(width), state()], [y_shape, h_shape]
    else:
        in_specs = [blk(cdim), prev, nxt, blk(LANES), state(), vec(cw), vec(cb)] + [vec(a) for a in small]
        args = [xin, xin, xin, dt, h0, cw, cb] + small
        out_specs, out_shape = [state()], [h_shape]
        scratch = [pltpu.VMEM((cdim // LANES, t + 2 * SUBLANES, LANES), F32)] + scratch
        if mode == "first":
            in_specs.append(vec(dsk))
            args.append(dsk)
            out_specs = [blk(width), blk(cdim)] + out_specs
            out_shape = [y_shape, jax.ShapeDtypeStruct((b, nc, t, cdim), BF16)] + out_shape
    kern = functools.partial(_ssd_kernel, nc=nc, rev=rev, mode=mode)
    return pl.pallas_call(
        kern,
        grid=(b, nc),
        in_specs=in_specs,
        out_specs=out_specs,
        out_shape=out_shape,
        scratch_shapes=scratch,
        compiler_params=_cparams(("parallel", "arbitrary")),
        name=f"ssd_{mode}_{'rev' if rev else 'fwd'}",
    )(*args)


def _outproj_kernel(x_ref, lru_ref, y_ref, z_ref, g_ref, ng_ref, w1_ref, w2_ref, o_ref):
    lru = _gather_groups(lru_ref)
    y = _gather_groups(y_ref)
    zz = z_ref[...]
    gated = y * _silu(zz)
    ms = jnp.mean(gated * gated, axis=-1, keepdims=True)
    ssd = gated * lax.rsqrt(ms + EPS) * ng_ref[...]
    mix = _dot(lru.astype(BF16), w1_ref[...]) + _dot(ssd.astype(BF16), w2_ref[...])
    o_ref[...] = x_ref[...] + g_ref[...] * mix


def _outproj_call(x, lru, y, z, gate, norm_g, w1, w2):
    b, l, d = x.shape
    seg = TILE // SUBLANES
    tok = lambda wd: pl.BlockSpec((None, TILE, wd), lambda bi, i: (bi, i, 0))
    return pl.pallas_call(
        _outproj_kernel,
        grid=(b, l // TILE),
        in_specs=[tok(d),
                  pl.BlockSpec((None, seg, SUBLANES, lru.shape[3]), lambda bi, i: (bi, i, 0, 0)),
                  pl.BlockSpec((None, GRID_W, SUBLANES, y.shape[3]), lambda bi, i: (bi, 0, i, 0)),
                  tok(z.shape[2]),
                  pl.BlockSpec((None, 1, d), lambda bi, i: (bi, 0, 0)),
                  pl.BlockSpec(norm_g.shape, lambda bi, i: (0, 0)),
                  pl.BlockSpec(w1.shape, lambda bi, i: (0, 0)),
                  pl.BlockSpec(w2.shape, lambda bi, i: (0, 0))],
        out_specs=tok(d),
        out_shape=jax.ShapeDtypeStruct((b, l, d), F32),
        compiler_params=_cparams(("parallel", "parallel")),
        name="outproj",
    )(x, lru, y, z, gate, norm_g, w1, w2)


FFN_RB = 128


def _ffn_kernel(xm_ref, xp_ref, xn_ref, sh_ref, sc_ref, gt_ref, ng_ref, fg_ref,
                wv_ref, wg_ref, cwv_ref, cwg_ref, cbv_ref, cbg_ref, wd_ref, o_ref,
                f_ref, uv0_ref, ug0_ref, uv1_ref, ug1_ref, act_ref, acc_ref, *, tm, nt, nj):
    i = pl.program_id(1)
    jj = pl.program_id(2)
    u_refs = ((uv0_ref, ug0_ref), (uv1_ref, ug1_ref))
    nslab = uv0_ref.shape[0]

    npiece = nslab // 2

    def up_piece(slot, p):
        u_ref, w_ref = (u_refs[slot][0], wv_ref) if p < npiece else (u_refs[slot][1], wg_ref)
        q = p % npiece
        res = _dot(f_ref[...], w_ref[:, 2 * q * LANES:2 * (q + 1) * LANES])
        u_ref[2 * q] = res[:, :LANES]
        u_ref[2 * q + 1] = res[:, LANES:]

    def conv_slab(slot, s):
        uv_ref, ug_ref = u_refs[slot]
        ls = slice(s * LANES, (s + 1) * LANES)
        for rb in range(tm // FFN_RB):
            r0 = SUBLANES - 1 + rb * FFN_RB
            val, gate = cbv_ref[:, ls].astype(BF16), cbg_ref[:, ls].astype(BF16)
            for k in range(3):
                val = val + cwv_ref[k:k + 1, ls].astype(BF16) * uv_ref[s, r0 + k:r0 + k + FFN_RB, :].astype(BF16)
                gate = gate + cwg_ref[k:k + 1, ls].astype(BF16) * ug_ref[s, r0 + k:r0 + k + FFN_RB, :].astype(BF16)
            act_ref[rb * FFN_RB:(rb + 1) * FFN_RB, ls] = jax.nn.gelu(gate) * val

    def down_piece(q):
        ks = slice(2 * q * LANES, 2 * (q + 1) * LANES)
        acc_ref[...] += _dot(act_ref[:, ks], wd_ref[ks, :])

    def step(up_slot, down_slot):
        for p in range(2 * npiece):
            if up_slot is not None:
                up_piece(up_slot, p)
            if down_slot is not None and p < nslab:
                conv_slab(down_slot, p)
                if p % 2 == 1:
                    down_piece(p // 2)

    @pl.when(jj == 0)
    def _():
        ng, sh, sc = ng_ref[...], sh_ref[...], sc_ref[...]
        fp = _norm_mod(xp_ref[...], ng, sh, sc)
        fn = _norm_mod(xn_ref[...], ng, sh, sc)
        f_ref[0:SUBLANES, :] = jnp.where(i > 0, fp, 0.0).astype(BF16)
        f_ref[SUBLANES:SUBLANES + tm, :] = _norm_mod(xm_ref[...], ng, sh, sc).astype(BF16)
        f_ref[SUBLANES + tm:, :] = jnp.where(i < nt - 1, fn, 0.0).astype(BF16)
        acc_ref[...] = jnp.zeros_like(acc_ref)
        step(0, None)

    for parity in range(2):
        @pl.when((jj > 0) & (jj < nj) & (jj % 2 == parity))
        def _():
            step(parity, 1 - parity)

    @pl.when(jj == nj)
    def _():
        step(None, (nj - 1) % 2)
        x2 = xm_ref[...] + gt_ref[...] * acc_ref[...]
        ms = jnp.mean(x2 * x2, axis=-1, keepdims=True)
        o_ref[...] = x2 * lax.rsqrt(ms + EPS) * fg_ref[...]


def _ffn_call(x1, shift, scale, gate, norm_g, final_g, w_up, conv_w, conv_b, w_down, tm, fb):
    b, l, d = x1.shape
    dff = w_down.shape[0]
    nt = l // tm
    nj = dff // fb
    tb = tm // SUBLANES
    nb8 = l // SUBLANES
    row = lambda: pl.BlockSpec((None, 1, d), lambda bi, i, j: (bi, 0, 0))
    vec = lambda: pl.BlockSpec((1, d), lambda bi, i, j: (0, 0))
    upj = lambda j: jnp.minimum(j, nj - 1)
    dnj = lambda j: jnp.maximum(j - 1, 0)
    in_specs = [
        pl.BlockSpec((None, tm, d), lambda bi, i, j: (bi, i, 0)),
        pl.BlockSpec((None, SUBLANES, d), lambda bi, i, j: (bi, jnp.maximum(i * tb - 1, 0), 0)),
        pl.BlockSpec((None, SUBLANES, d), lambda bi, i, j: (bi, jnp.minimum((i + 1) * tb, nb8 - 1), 0)),
        row(), row(), row(), vec(), vec(),
        pl.BlockSpec((d, fb), lambda bi, i, j: (0, upj(j))),
        pl.BlockSpec((d, fb), lambda bi, i, j: (0, nj + upj(j))),
        pl.BlockSpec((3, fb), lambda bi, i, j: (0, dnj(j))),
        pl.BlockSpec((3, fb), lambda bi, i, j: (0, nj + dnj(j))),
        pl.BlockSpec((1, fb), lambda bi, i, j: (0, dnj(j))),
        pl.BlockSpec((1, fb), lambda bi, i, j: (0, nj + dnj(j))),
        pl.BlockSpec((fb, d), lambda bi, i, j: (dnj(j), 0)),
    ]
    kern = functools.partial(_ffn_kernel, tm=tm, nt=nt, nj=nj)
    slab = lambda: pltpu.VMEM((fb // LANES, tm + 2 * SUBLANES, LANES), F32)
    return pl.pallas_call(
        kern,
        grid=(b, nt, nj + 1),
        in_specs=in_specs,
        out_specs=pl.BlockSpec((None, tm, d), lambda bi, i, j: (bi, i, 0)),
        out_shape=jax.ShapeDtypeStruct((b, l, d), F32),
        scratch_shapes=[pltpu.VMEM((tm + 2 * SUBLANES, d), BF16), slab(), slab(), slab(), slab(),
                        pltpu.VMEM((tm, fb), BF16), pltpu.VMEM((tm, d), F32)],
        compiler_params=_cparams(("parallel", "parallel", "arbitrary")),
        name="ffn",
    )(x1, x1, x1, shift, scale, gate, norm_g, final_g, w_up, w_up, conv_w, conv_w, conv_b, conv_b, w_down)


def _block_diag_gates(wa, wx, heads_per_group):
    h, hd, _ = wa.shape
    ng = h // heads_per_group
    eye = jnp.eye(heads_per_group, dtype=wa.dtype)

    def bd(w):
        w = w.reshape(ng, heads_per_group, hd, hd)
        return jnp.einsum('gaij,ab->gaibj', w, eye).reshape(ng, heads_per_group * hd, heads_per_group * hd)

    return jnp.concatenate([bd(wa), bd(wx)], axis=-1).astype(BF16)


def _pad_lanes(v, n=LANES):
    return jnp.pad(v, ((0, 0), (0, n - v.shape[1])))


def kernel(x, c, ctx, c_ctx, ada_w, ada_b, norm1_g, w_in, lru_conv_w, lru_conv_b, lru_wa, lru_ba, lru_wx, lru_bx,
           lru_lambda, ssd_conv_w, ssd_conv_b, ssd_a_log, ssd_dt_bias, ssd_d, ssd_norm_g, w_out, norm2_g,
           ffn_w_up, ffn_conv_w, ffn_conv_b, ffn_w_down, final_norm_g):
    b, l, d = x.shape
    lctx = ctx.shape[1]
    lw = lru_conv_w.shape[2]
    sw = SSD_HEADS * SSD_HEAD_DIM
    cdim = ssd_conv_w.shape[2]
    rows = l // GRID_W
    assert ada_w.shape[0] == 1, "single layer"
    assert rows == SSD_CHUNK, "an SSD chunk is one column of the latent grid"
    assert lctx % SSD_CHUNK == 0 and l % TILE == 0

    s_in = jnp.zeros((SUBLANES, d), F32).at[:b].set(c).at[b].set(c_ctx)
    mod = _mod_call(s_in, ada_w[0], ada_b)
    mod_lat = [m.reshape(b, 1, d) for m in jnp.split(mod[:b], 6, axis=-1)]
    mod_ctx = [jnp.broadcast_to(m.reshape(1, 1, d), (b, 1, d)) for m in jnp.split(mod[b:b + 1], 6, axis=-1)]
    sh1, sc1, g1, sh2, sc2, g2 = mod_lat
    csh1, csc1 = mod_ctx[0], mod_ctx[1]

    wi = w_in[0].astype(BF16)
    o1, o2, o3, o4 = lw, 2 * lw, 2 * lw + sw, 2 * lw + sw + cdim
    w_lu, w_lg, w_z, w_xbc, w_dt = wi[:, :o1], wi[:, o1:o2], wi[:, o2:o3], wi[:, o3:o4], _pad_lanes(wi[:, o4:])
    seg = TILE // SUBLANES
    interleaved = lambda wd: ("groups", (l // SUBLANES, SUBLANES, wd), (seg, SUBLANES, wd), lambda i: (i, 0, 0))
    column = lambda wd: ("groups", (GRID_W, rows, wd), (GRID_W, SUBLANES, wd), lambda i: (0, i, 0))
    lu_l, lg_l, z_l, xbc_l, dt_l = _inproj_call(x, sh1, sc1, norm1_g, [
        (w_lu,) + interleaved(lw), (w_lg,) + interleaved(lw),
        (w_z, "raster", (l, sw), (TILE, sw), lambda i: (i, 0)),
        (w_xbc,) + column(cdim), (w_dt,) + column(LANES)], TILE)
    cseg = lctx // SUBLANES
    nctx = lctx // SSD_CHUNK
    chunks = lambda wd: ("chunks", (nctx, SSD_CHUNK, wd), (nctx, SSD_CHUNK, wd), lambda i: (0, 0, 0))
    lu_c, xbc_c, dt_c = _inproj_call(ctx, csh1, csc1, norm1_g, [
        (w_lu, "groups", (cseg, SUBLANES, lw), (cseg, SUBLANES, lw), lambda i: (0, 0, 0)),
        (w_xbc,) + chunks(cdim), (w_dt,) + chunks(LANES)], lctx)

    hpg = 4
    lcw, lcb = lru_conv_w[0], lru_conv_b
    zeros_w = jnp.zeros((b, 1, lw), F32)
    lru_args = []
    for dr in range(2):
        lru_args.append((lcw, lcb, _block_diag_gates(lru_wa[0, dr], lru_wx[0, dr], hpg),
                         lru_ba[0, dr][None], lru_bx[0, dr][None], lru_lambda[0, dr][None]))
    _, hc_f = _lru_call(lu_c, zeros_w, *lru_args[0], seg=cseg, rev=False)
    _, hc_b = _lru_call(lu_c, zeros_w, *lru_args[1], seg=cseg, rev=True)
    h_b, _ = _lru_call(lu_l, hc_b, *lru_args[1], seg=seg, rev=True)
    lru_out, _ = _lru_call(lu_l, hc_f, *lru_args[0], seg=seg, rev=False, hb=h_b, lg=lg_l)

    ssd_params = (ssd_conv_w[0], ssd_conv_b,
                  _pad_lanes(ssd_dt_bias[0].reshape(1, -1)), _pad_lanes(ssd_dt_bias[0].reshape(1, -1)).T,
                  _pad_lanes(ssd_a_log[0].reshape(1, -1)), _pad_lanes(ssd_a_log[0].reshape(1, -1)).T)
    zero_state = jnp.zeros((b, SSD_STATE, sw), F32)
    (sc_f,) = _ssd_call(xbc_c, dt_c, zero_state, ssd_params, rev=False, mode="state")
    (sc_b,) = _ssd_call(xbc_c, dt_c, zero_state, ssd_params, rev=True, mode="state")
    dsk = jnp.repeat(ssd_d[0], SSD_HEAD_DIM)[None]
    y_b, xc_l, _ = _ssd_call(xbc_l, dt_l, sc_b, ssd_params, rev=True, mode="first", dsk=dsk)
    y_l, _ = _ssd_call(xc_l, dt_l, sc_f, ssd_params, rev=False, mode="second", yb=y_b)

    wo = w_out[0].astype(BF16)
    x1 = _outproj_call(x, lru_out, y_l, z_l, g1, ssd_norm_g, wo[:lw], wo[lw:])
    return _ffn_call(x1, sh2, sc2, g2, norm2_g, final_norm_g[None], ffn_w_up[0].astype(BF16), ffn_conv_w[0],
                     ffn_conv_b, ffn_w_down[0].astype(BF16), 512, 512)
```

```python
import functools

import jax
import jax.numpy as jnp
from jax import lax
from jax.experimental import pallas as pl
from jax.experimental.pallas import tpu as pltpu

F32 = jnp.float32
BF16 = jnp.bfloat16

EPS = 1e-6
GRID_W = 64
LRU_C = 8.0
SSD_HEADS = 16
SSD_HEAD_DIM = 64
SSD_GROUPS = 2
SSD_STATE = 128
SSD_CHUNK = 128
SUBLANES = 8
LANES = 128
VMEM_LIMIT = 56 * 1024 * 1024
TILE = SUBLANES * GRID_W


def _cparams(sem):
    return pltpu.CompilerParams(dimension_semantics=sem, vmem_limit_bytes=VMEM_LIMIT)


def _split_bf16(v, terms):
    parts = []
    rem = v
    for _ in range(terms):
        p = rem.astype(BF16)
        parts.append(p)
        rem = rem - p.astype(F32)
    return parts


def _dot(a, b):
    return jnp.dot(a, b, preferred_element_type=F32)


def _sigmoid(x):
    return 0.5 * jnp.tanh(0.5 * x) + 0.5


def _silu(x):
    h = 0.5 * x
    return h * jnp.tanh(h) + h


def _gather_groups(ref):
    return jnp.concatenate([ref[:, s, :] for s in range(SUBLANES)], axis=0)


def _scatter_groups(ref, val):
    seg = val.shape[0] // SUBLANES
    for s in range(SUBLANES):
        ref[:, s, :] = val[s * seg:(s + 1) * seg, :]


def _mod_kernel(s_ref, w_ref, b_ref, o_ref):
    s = _silu(s_ref[...])
    s_hi, s_lo = _split_bf16(s, 2)
    w_hi, w_lo = _split_bf16(w_ref[...], 2)
    acc = _dot(s_hi, w_hi) + _dot(s_hi, w_lo) + _dot(s_lo, w_hi)
    o_ref[...] = acc + b_ref[...]


def _mod_call(s, w, b):
    rows, d = s.shape
    n = w.shape[1]
    nb = 1536
    return pl.pallas_call(
        _mod_kernel,
        grid=(n // nb,),
        in_specs=[pl.BlockSpec((rows, d), lambda j: (0, 0)),
                  pl.BlockSpec((d, nb), lambda j: (0, j)),
                  pl.BlockSpec((1, nb), lambda j: (0, j))],
        out_specs=pl.BlockSpec((rows, nb), lambda j: (0, j)),
        out_shape=jax.ShapeDtypeStruct((rows, n), F32),
        compiler_params=_cparams(("parallel",)),
        name="mod",
    )(s, w, b)


def _norm_mod(x, g, shift, scale):
    ms = jnp.mean(x * x, axis=-1, keepdims=True)
    y = x * lax.rsqrt(ms + EPS) * g
    return y * (1.0 + scale) + shift


def _inproj_kernel(x_ref, sh_ref, sc_ref, g_ref, *refs, kinds):
    n = len(kinds)
    w_refs, o_refs = refs[:n], refs[n:]
    h = _norm_mod(x_ref[...], g_ref[...], sh_ref[...], sc_ref[...]).astype(BF16)
    for w_ref, o_ref, kind in zip(w_refs, o_refs, kinds):
        res = _dot(h, w_ref[...])
        if kind == "raster":
            o_ref[...] = res
        elif kind == "chunks":
            rows = o_ref.shape[1]
            for ci in range(o_ref.shape[0]):
                o_ref[ci] = res[ci * rows:(ci + 1) * rows, :]
        else:
            _scatter_groups(o_ref, res)


def _inproj_call(x, shift, scale, g, outs, tm):
    b, l, d = x.shape
    row = lambda: pl.BlockSpec((None, 1, d), lambda bi, i: (bi, 0, 0))
    in_specs = [pl.BlockSpec((None, tm, d), lambda bi, i: (bi, i, 0)), row(), row(),
                pl.BlockSpec((1, d), lambda bi, i: (0, 0))]
    in_specs += [pl.BlockSpec(o[0].shape, lambda bi, i: (0, 0)) for o in outs]
    out_specs, out_shape = [], []
    for _, _, shape, block, imap in outs:
        out_specs.append(pl.BlockSpec((None,) + block, functools.partial(lambda bi, i, f: (bi,) + f(i), f=imap)))
        out_shape.append(jax.ShapeDtypeStruct((b,) + shape, F32))
    kern = functools.partial(_inproj_kernel, kinds=tuple(o[1] for o in outs))
    return pl.pallas_call(
        kern,
        grid=(b, l // tm),
        in_specs=in_specs,
        out_specs=out_specs,
        out_shape=out_shape,
        compiler_params=_cparams(("parallel", "parallel")),
        name="inproj",
    )(x, shift, scale, g, *[o[0] for o in outs])


LRU_RB = 64
SQRT_FLOOR = 1e-30
LOG2_E = 1.4426950408889634
HALO = 3 * SUBLANES


def _lru_kernel(*refs, seg, nt, rev, final):
    (lu_ref, prev_ref, next_ref, h0_ref, cw_ref, cb_ref, wg_ref, ba_ref, bx_ref, lam_ref) = refs[:10]
    pos = 10
    if final:
        hb_ref, lg_ref = refs[pos:pos + 2]
        pos += 2
    o_ref, hl_ref = refs[pos:pos + 2]
    xe_ref, a_ref, u_ref, st_ref, carry_ref = refs[pos + 2:]

    t = seg * SUBLANES
    i = pl.program_id(1)
    c = (nt - 1 - i) if rev else i

    @pl.when(i == 0)
    def _():
        carry_ref[...] = h0_ref[...]

    width = cw_ref.shape[1]
    row = lax.broadcasted_iota(jnp.int32, (SUBLANES, width), 0)

    def before(own, other):
        return jnp.where(row == 0, pltpu.roll(jnp.where(c > 0, other, 0.0), 1, 0), pltpu.roll(own, 1, 0))

    xe_ref[0:SUBLANES, :] = before(lu_ref[seg - 2], prev_ref[0])
    xe_ref[SUBLANES:2 * SUBLANES, :] = before(lu_ref[seg - 1], prev_ref[1])
    xe_ref[2 * SUBLANES:2 * SUBLANES + t, :] = lu_ref[...].reshape(t, width)
    xe_ref[2 * SUBLANES + t:HALO + t, :] = jnp.where(
        row == SUBLANES - 1, pltpu.roll(jnp.where(c < nt - 1, next_ref[0], 0.0), SUBLANES - 1, 0),
        pltpu.roll(lu_ref[0], SUBLANES - 1, 0))

    cw = cw_ref[...]
    cb = cb_ref[...]
    half_ba = 0.5 * ba_ref[...]
    half_bx = 0.5 * bx_ref[...]
    log_decay = -LRU_C * jax.nn.softplus(-lam_ref[...])
    c2 = log_decay * (0.5 * LOG2_E)
    ngroups = wg_ref.shape[0]
    gw = width // ngroups

    def gate_block(rb, carry):
        r0 = pl.multiple_of(rb * LRU_RB, LRU_RB)
        xc = cb
        for k in range(4):
            xc = xc + cw[k:k + 1, :] * xe_ref[pl.ds(r0 + k * SUBLANES, LRU_RB), :]
        for g in range(ngroups):
            sl = slice(g * gw, (g + 1) * gw)
            xg = xc[:, sl]
            pre = _dot(xg.astype(BF16), wg_ref[g])
            t_r = jnp.tanh(pre[:, :gw] + half_ba[:, sl])
            t_i = jnp.tanh(pre[:, gw:] + half_bx[:, sl])
            a = jnp.exp2(t_r * c2[:, sl] + c2[:, sl])
            y = 1.0 - a * a
            hx = 0.5 * xg
            u = (y * lax.rsqrt(jnp.maximum(y, SQRT_FLOOR))) * (hx * t_i + hx)
            a_ref[pl.ds(r0, LRU_RB), sl] = a
            u_ref[pl.ds(r0, LRU_RB), sl] = u
        return carry

    lax.fori_loop(0, t // LRU_RB, gate_block, 0)

    def slab(jj):
        j = (seg - 1 - jj) if rev else jj
        return j, pl.multiple_of(j * SUBLANES, SUBLANES)

    def seg_totals(jj, hp):
        h, p = hp
        _, r0 = slab(jj)
        a8 = a_ref[pl.ds(r0, SUBLANES), :]
        return a8 * h + u_ref[pl.ds(r0, SUBLANES), :], p * a8

    h_end, p_end = lax.fori_loop(0, seg, seg_totals,
                                 (jnp.zeros((SUBLANES, width), F32), jnp.ones((SUBLANES, width), F32)), unroll=4)

    cur = carry_ref[...]
    for r in (range(SUBLANES - 1, -1, -1) if rev else range(SUBLANES)):
        st_ref[r:r + 1, :] = cur
        cur = p_end[r:r + 1, :] * cur + h_end[r:r + 1, :]
    carry_ref[...] = cur
    hl_ref[...] = cur

    def emit(jj, h):
        j, r0 = slab(jj)
        h = a_ref[pl.ds(r0, SUBLANES), :] * h + u_ref[pl.ds(r0, SUBLANES), :]
        if final:
            o_ref[j] = (h + hb_ref[j]) * jax.nn.gelu(lg_ref[j])
        else:
            o_ref[j] = h
        return h

    lax.fori_loop(0, seg, emit, st_ref[...], unroll=4)


def _lru_call(lu, h0, cw, cb, wg, ba, bx, lam, *, seg, rev, hb=None, lg=None):
    b, n8, _, w = lu.shape
    nt = n8 // seg
    final = hb is not None
    cidx = (lambda i: nt - 1 - i) if rev else (lambda i: i)
    main = lambda: pl.BlockSpec((None, seg, SUBLANES, w), lambda bi, i: (bi, cidx(i), 0, 0))
    vec = lambda a: pl.BlockSpec(a.shape, lambda bi, i: (0,) * a.ndim)
    in_specs = [
        main(),
        pl.BlockSpec((None, 2, SUBLANES, w), lambda bi, i: (bi, jnp.maximum(cidx(i) * (seg // 2) - 1, 0), 0, 0)),
        pl.BlockSpec((None, 1, SUBLANES, w), lambda bi, i: (bi, jnp.minimum((cidx(i) + 1) * seg, n8 - 1), 0, 0)),
        pl.BlockSpec((None, 1, w), lambda bi, i: (bi, 0, 0)),
        vec(cw), vec(cb), vec(wg), vec(ba), vec(bx), vec(lam),
    ]
    args = [lu, lu, lu, h0, cw, cb, wg, ba, bx, lam]
    if final:
        in_specs += [main(), main()]
        args += [hb, lg]
    t = seg * SUBLANES
    kern = functools.partial(_lru_kernel, seg=seg, nt=nt, rev=rev, final=final)
    return pl.pallas_call(
        kern,
        grid=(b, nt),
        in_specs=in_specs,
        out_specs=[main(), pl.BlockSpec((None, 1, w), lambda bi, i: (bi, 0, 0))],
        out_shape=[jax.ShapeDtypeStruct(lu.shape, F32), jax.ShapeDtypeStruct((b, 1, w), F32)],
        scratch_shapes=[pltpu.VMEM((t + HALO, w), F32), pltpu.VMEM((t, w), F32), pltpu.VMEM((t, w), F32),
                        pltpu.VMEM((SUBLANES, w), F32), pltpu.VMEM((1, w), F32)],
        compiler_params=_cparams(("parallel", "arbitrary")),
        name="lru_rev" if rev else "lru_fwd",
    )(*args)


def _ssd_kernel(*refs, nc, rev, mode):
    if mode == "second":
        xc_ref, dt_ref, h0_ref, bias_r_ref, bias_c_ref, alog_r_ref, alog_c_ref, yb_ref, o_ref, hl_ref, hst_ref = refs
    else:
        (xm_ref, xp_ref, xn_ref, dt_ref, h0_ref, cw_ref, cb_ref, bias_r_ref, bias_c_ref,
         alog_r_ref, alog_c_ref) = refs[:11]
        if mode == "first":
            dsk_ref, o_ref, xc_ref, hl_ref, xb_ref, hst_ref = refs[11:]
        else:
            hl_ref, xb_ref, hst_ref = refs[11:]

    t = SSD_CHUNK
    width = SSD_HEADS * SSD_HEAD_DIM
    gn = SSD_STATE
    i = pl.program_id(1)
    c = (nc - 1 - i) if rev else i
    d = 1 if rev else 0
    tl = 0 if rev else t - 1

    @pl.when(i == 0)
    def _():
        hst_ref[...] = h0_ref[...]

    slabs = []
    for s in range((width + 2 * SSD_GROUPS * gn) // LANES):
        ls = slice(s * LANES, (s + 1) * LANES)
        if mode == "second":
            slabs.append(xc_ref[:, ls])
            continue
        xb_ref[s, 0:SUBLANES, :] = jnp.where(c > 0, xp_ref[:, ls], 0.0)
        xb_ref[s, SUBLANES:SUBLANES + t, :] = xm_ref[:, ls]
        xb_ref[s, SUBLANES + t:, :] = jnp.where(c < nc - 1, xn_ref[:, ls], 0.0)
        xc = cb_ref[:, ls]
        for k in range(4):
            xc = xc + cw_ref[k:k + 1, ls] * xb_ref[s, SUBLANES - 2 + k:SUBLANES - 2 + k + t, :]
        slabs.append(_silu(xc))
        if mode == "first":
            xc_ref[:, ls] = slabs[-1].astype(BF16)
    nx = width // LANES
    x_slabs = slabs[:nx]
    bm = slabs[nx:nx + SSD_GROUPS]
    cm = slabs[nx + SSD_GROUPS:]

    ti = lax.broadcasted_iota(jnp.int32, (t, t), 0)
    si = lax.broadcasted_iota(jnp.int32, (t, t), 1)
    inc = (si >= ti) if rev else (si <= ti)
    inc_b = jnp.where(inc, 1.0, 0.0).astype(BF16)
    inc_t_b = jnp.where((ti >= si) if rev else (ti <= si), 1.0, 0.0).astype(BF16)

    dtraw = dt_ref[...]
    dt = jax.nn.softplus(dtraw + bias_r_ref[...])
    da = dt * (-jnp.exp(alog_r_ref[...]))
    cs = sum(_dot(inc_b, p) for p in _split_bf16(da, 3))
    q0 = SSD_HEADS * d
    dtraw_t = dtraw.T[q0:q0 + SSD_HEADS, :]
    dt_t = jax.nn.softplus(dtraw_t + bias_c_ref[q0:q0 + SSD_HEADS, :])
    da_t = dt_t * (-jnp.exp(alog_c_ref[q0:q0 + SSD_HEADS, :]))
    cs_t = sum(_dot(p, inc_t_b) for p in _split_bf16(da_t, 3))
    w1_t = dt_t * jnp.exp(cs_t[:, tl:tl + 1] - cs_t)
    ecs = jnp.exp(cs)

    lane = lax.broadcasted_iota(jnp.int32, (t, LANES), 1)
    lo = lane < SSD_HEAD_DIM
    hg = SSD_HEADS // SSD_GROUPS
    gw = hg * SSD_HEAD_DIM

    if mode != "state":
        scores = [lax.dot_general(cm[g].astype(BF16), bm[g].astype(BF16),
                                  (((1,), (1,)), ((), ())), preferred_element_type=F32)
                  for g in range(SSD_GROUPS)]
        z_off = [_dot(cm[g].astype(BF16), hst_ref[:, g * gw:(g + 1) * gw].astype(BF16))
                 for g in range(SSD_GROUPS)]
    bm_t = [bm[g].astype(F32).T for g in range(SSD_GROUPS)]
    keep_lo = jnp.where(lo, 1.0, 0.0).astype(BF16)
    keep_hi = jnp.where(lo, 0.0, 1.0).astype(BF16)

    ys = []
    for pr in range(SSD_HEADS // 2):
        g = (2 * pr) // hg
        xp = x_slabs[pr].astype(BF16)
        rhs = jnp.concatenate([xp * keep_lo, xp * keep_hi], axis=0)
        lhs_s = jnp.concatenate([bm_t[g] * w1_t[2 * pr + e:2 * pr + e + 1, :] for e in range(2)],
                                axis=1).astype(BF16)
        if mode != "state":
            ms = []
            for e in range(2):
                hh = 2 * pr + e
                q = q0 + hh
                dec = jnp.exp(cs[:, q:q + 1] - cs_t[hh:hh + 1, :])
                lmat = jnp.where(inc, dec, 0.0) * dt_t[hh:hh + 1, :]
                ms.append((scores[g] * lmat).astype(BF16))
            lhs = jnp.concatenate([jnp.concatenate(ms, axis=1), lhs_s], axis=0)
            res = _dot(lhs, rhs)
            y_diag, s_new = res[:t], res[t:]
            e_pair = jnp.where(lo, ecs[:, q0 + 2 * pr:q0 + 2 * pr + 1], ecs[:, q0 + 2 * pr + 1:q0 + 2 * pr + 2])
            col = (pr * LANES) % gw
            ys.append(y_diag + z_off[g][:, col:col + LANES] * e_pair)
        else:
            s_new = _dot(lhs_s, rhs)
        dec_pair = jnp.where(lo[0:1, :], ecs[tl:tl + 1, q0 + 2 * pr:q0 + 2 * pr + 1],
                             ecs[tl:tl + 1, q0 + 2 * pr + 1:q0 + 2 * pr + 2])
        hst_ref[:, pr * LANES:(pr + 1) * LANES] = hst_ref[:, pr * LANES:(pr + 1) * LANES] * dec_pair + s_new

    @pl.when(i == nc - 1)
    def _():
        hl_ref[...] = hst_ref[...]

    if mode == "state":
        return
    y = jnp.concatenate(ys, axis=1)
    if mode == "first":
        o_ref[...] = y + jnp.concatenate(x_slabs, axis=1) * dsk_ref[...]
    else:
        o_ref[...] = y + yb_ref[...]


def _ssd_call(xin, dt, h0, params, *, rev, mode, yb=None, dsk=None):
    cw, cb, bias_r, bias_c, alog_r, alog_c = params
    b, nc, t, cdim = xin.shape
    width = SSD_HEADS * SSD_HEAD_DIM
    cidx = (lambda i: nc - 1 - i) if rev else (lambda i: i)
    tb = t // SUBLANES
    blk = lambda wd: pl.BlockSpec((None, None, t, wd), lambda bi, i: (bi, cidx(i), 0, 0))
    prev = pl.BlockSpec((None, None, SUBLANES, cdim), lambda bi, i: (bi, jnp.maximum(cidx(i) - 1, 0), tb - 1, 0))
    nxt = pl.BlockSpec((None, None, SUBLANES, cdim), lambda bi, i: (bi, jnp.minimum(cidx(i) + 1, nc - 1), 0, 0))
    vec = lambda a: pl.BlockSpec(a.shape, lambda bi, i: (0,) * a.ndim)
    state = lambda: pl.BlockSpec((None, SSD_STATE, width), lambda bi, i: (bi, 0, 0))
    small = [bias_r, bias_c, alog_r, alog_c]
    y_shape = jax.ShapeDtypeStruct((b, nc, t, width), F32)
    h_shape = jax.ShapeDtypeStruct((b, SSD_STATE, width), F32)
    scratch = [pltpu.VMEM((SSD_STATE, width), F32)]
    if mode == "second":
        in_specs = [blk(cdim), blk(LANES), state()] + [vec(a) for a in small] + [blk(width)]
        args = [xin, dt, h0] + small + [yb]
        out_specs, out_shape = [blk(width), state()], [y_shape, h_shape]
    else:
        in_specs = [blk(cdim), prev, nxt, blk(LANES), state(), vec(cw), vec(cb)] + [vec(a) for a in small]
        args = [xin, xin, xin, dt, h0, cw, cb] + small
        out_specs, out_shape = [state()], [h_shape]
        scratch = [pltpu.VMEM((cdim // LANES, t + 2 * SUBLANES, LANES), F32)] + scratch
        if mode == "first":
            in_specs.append(vec(dsk))
            args.append(dsk)
            out_specs = [blk(width), blk(cdim)] + out_specs
            out_shape = [y_shape, jax.ShapeDtypeStruct((b, nc, t, cdim), BF16)] + out_shape
    kern = functools.partial(_ssd_kernel, nc=nc, rev=rev, mode=mode)
    return pl.pallas_call(
        kern,
        grid=(b, nc),
        in_specs=in_specs,
        out_specs=out_specs,
        out_shape=out_shape,
        scratch_shapes=scratch,
        compiler_params=_cparams(("parallel", "arbitrary")),
        name=f"ssd_{mode}_{'rev' if rev else 'fwd'}",
    )(*args)


def _outproj_kernel(x_ref, lru_ref, y_ref, z_ref, g_ref, ng_ref, w1_ref, w2_ref, o_ref):
    lru = _gather_groups(lru_ref)
    y = _gather_groups(y_ref)
    zz = z_ref[...]
    gated = y * _silu(zz)
    ms = jnp.mean(gated * gated, axis=-1, keepdims=True)
    ssd = gated * lax.rsqrt(ms + EPS) * ng_ref[...]
    mix = _dot(lru.astype(BF16), w1_ref[...]) + _dot(ssd.astype(BF16), w2_ref[...])
    o_ref[...] = x_ref[...] + g_ref[...] * mix


def _outproj_call(x, lru, y, z, gate, norm_g, w1, w2):
    b, l, d = x.shape
    seg = TILE // SUBLANES
    tok = lambda wd: pl.BlockSpec((None, TILE, wd), lambda bi, i: (bi, i, 0))
    return pl.pallas_call(
        _outproj_kernel,
        grid=(b, l // TILE),
        in_specs=[tok(d),
                  pl.BlockSpec((None, seg, SUBLANES, lru.shape[3]), lambda bi, i: (bi, i, 0, 0)),
                  pl.BlockSpec((None, GRID_W, SUBLANES, y.shape[3]), lambda bi, i: (bi, 0, i, 0)),
                  tok(z.shape[2]),
                  pl.BlockSpec((None, 1, d), lambda bi, i: (bi, 0, 0)),
                  pl.BlockSpec(norm_g.shape, lambda bi, i: (0, 0)),
                  pl.BlockSpec(w1.shape, lambda bi, i: (0, 0)),
                  pl.BlockSpec(w2.shape, lambda bi, i: (0, 0))],
        out_specs=tok(d),
        out_shape=jax.ShapeDtypeStruct((b, l, d), F32),
        compiler_params=_cparams(("parallel", "parallel")),
        name="outproj",
    )(x, lru, y, z, gate, norm_g, w1, w2)


FFN_RB = 128


def _ffn_kernel(xm_ref, xp_ref, xn_ref, sh_ref, sc_ref, gt_ref, ng_ref, fg_ref,
                wup_ref, cwv_ref, cwg_ref, cbv_ref, cbg_ref, wdn_ref, o_ref,
                f_ref, uv0_ref, ug0_ref, uv1_ref, ug1_ref, act_ref, acc_ref, *, tm, nt, nj):
    i = pl.program_id(1)
    jj = pl.program_id(2)
    u_refs = ((uv0_ref, ug0_ref), (uv1_ref, ug1_ref))
    nslab = uv0_ref.shape[0]

    npiece = nslab // 2

    def up_piece(slot, p, blk):
        u_ref, wi = (u_refs[slot][0], blk) if p < npiece else (u_refs[slot][1], nj + blk)
        q = p % npiece
        res = _dot(f_ref[...], wup_ref[wi, :, 2 * q * LANES:2 * (q + 1) * LANES])
        u_ref[2 * q] = res[:, :LANES]
        u_ref[2 * q + 1] = res[:, LANES:]

    def conv_slab(slot, s):
        uv_ref, ug_ref = u_refs[slot]
        ls = slice(s * LANES, (s + 1) * LANES)
        for rb in range(tm // FFN_RB):
            r0 = SUBLANES - 1 + rb * FFN_RB
            val, gate = cbv_ref[:, ls].astype(BF16), cbg_ref[:, ls].astype(BF16)
            for k in range(3):
                val = val + cwv_ref[k:k + 1, ls].astype(BF16) * uv_ref[s, r0 + k:r0 + k + FFN_RB, :].astype(BF16)
                gate = gate + cwg_ref[k:k + 1, ls].astype(BF16) * ug_ref[s, r0 + k:r0 + k + FFN_RB, :].astype(BF16)
            act_ref[rb * FFN_RB:(rb + 1) * FFN_RB, ls] = jax.nn.gelu(gate) * val

    def down_piece(q, blk):
        ks = slice(2 * q * LANES, 2 * (q + 1) * LANES)
        acc_ref[...] += _dot(act_ref[:, ks], wdn_ref[blk, ks, :])

    def step(up_slot, down_slot, up_blk=None, down_blk=None):
        for p in range(2 * npiece):
            if up_slot is not None:
                up_piece(up_slot, p, up_blk)
            if down_slot is not None and p < nslab:
                conv_slab(down_slot, p)
                if p % 2 == 1:
                    down_piece(p // 2, down_blk)

    @pl.when(jj == 0)
    def _():
        ng, sh, sc = ng_ref[...], sh_ref[...], sc_ref[...]
        fp = _norm_mod(xp_ref[...], ng, sh, sc)
        fn = _norm_mod(xn_ref[...], ng, sh, sc)
        f_ref[0:SUBLANES, :] = jnp.where(i > 0, fp, 0.0).astype(BF16)
        f_ref[SUBLANES:SUBLANES + tm, :] = _norm_mod(xm_ref[...], ng, sh, sc).astype(BF16)
        f_ref[SUBLANES + tm:, :] = jnp.where(i < nt - 1, fn, 0.0).astype(BF16)
        acc_ref[...] = jnp.zeros_like(acc_ref)
        step(0, None, up_blk=0)

    for parity in range(2):
        @pl.when((jj > 0) & (jj < nj) & (jj % 2 == parity))
        def _():
            step(parity, 1 - parity, up_blk=jj, down_blk=jj - 1)

    @pl.when(jj == nj)
    def _():
        step(None, (nj - 1) % 2, down_blk=nj - 1)
        x2 = xm_ref[...] + gt_ref[...] * acc_ref[...]
        ms = jnp.mean(x2 * x2, axis=-1, keepdims=True)
        o_ref[...] = x2 * lax.rsqrt(ms + EPS) * fg_ref[...]


def _ffn_call(x1, shift, scale, gate, norm_g, final_g, w_up, conv_w, conv_b, w_down, tm, fb):
    b, l, d = x1.shape
    dff = w_down.shape[0]
    nt = l // tm
    nj = dff // fb
    tb = tm // SUBLANES
    nb8 = l // SUBLANES
    row = lambda: pl.BlockSpec((None, 1, d), lambda bi, i, j: (bi, 0, 0))
    vec = lambda: pl.BlockSpec((1, d), lambda bi, i, j: (0, 0))
    dnj = lambda j: jnp.maximum(j - 1, 0)
    w_up = w_up.reshape(d, 2 * nj, fb).transpose(1, 0, 2)
    w_down = w_down.reshape(nj, fb, d)
    resident = lambda a: pl.BlockSpec(a.shape, lambda bi, i, j: (0,) * a.ndim, pipeline_mode=pl.Buffered(1))
    in_specs = [
        pl.BlockSpec((None, tm, d), lambda bi, i, j: (bi, i, 0)),
        pl.BlockSpec((None, SUBLANES, d), lambda bi, i, j: (bi, jnp.maximum(i * tb - 1, 0), 0)),
        pl.BlockSpec((None, SUBLANES, d), lambda bi, i, j: (bi, jnp.minimum((i + 1) * tb, nb8 - 1), 0)),
        row(), row(), row(), vec(), vec(),
        resident(w_up),
        pl.BlockSpec((3, fb), lambda bi, i, j: (0, dnj(j))),
        pl.BlockSpec((3, fb), lambda bi, i, j: (0, nj + dnj(j))),
        pl.BlockSpec((1, fb), lambda bi, i, j: (0, dnj(j))),
        pl.BlockSpec((1, fb), lambda bi, i, j: (0, nj + dnj(j))),
        resident(w_down),
    ]
    kern = functools.partial(_ffn_kernel, tm=tm, nt=nt, nj=nj)
    slab = lambda: pltpu.VMEM((fb // LANES, tm + 2 * SUBLANES, LANES), F32)
    return pl.pallas_call(
        kern,
        grid=(b, nt, nj + 1),
        in_specs=in_specs,
        out_specs=pl.BlockSpec((None, tm, d), lambda bi, i, j: (bi, i, 0)),
        out_shape=jax.ShapeDtypeStruct((b, l, d), F32),
        scratch_shapes=[pltpu.VMEM((tm + 2 * SUBLANES, d), BF16), slab(), slab(), slab(), slab(),
                        pltpu.VMEM((tm, fb), BF16), pltpu.VMEM((tm, d), F32)],
        compiler_params=_cparams(("parallel", "parallel", "arbitrary")),
        name="ffn",
    )(x1, x1, x1, shift, scale, gate, norm_g, final_g, w_up, conv_w, conv_w, conv_b, conv_b, w_down)


def _block_diag_gates(wa, wx, heads_per_group):
    h, hd, _ = wa.shape
    ng = h // heads_per_group
    eye = jnp.eye(heads_per_group, dtype=wa.dtype)

    def bd(w):
        w = w.reshape(ng, heads_per_group, hd, hd)
        return jnp.einsum('gaij,ab->gaibj', w, eye).reshape(ng, heads_per_group * hd, heads_per_group * hd)

    return (0.5 * jnp.concatenate([bd(wa), bd(wx)], axis=-1)).astype(BF16)


def _pad_lanes(v, n=LANES):
    return jnp.pad(v, ((0, 0), (0, n - v.shape[1])))


def kernel(x, c, ctx, c_ctx, ada_w, ada_b, norm1_g, w_in, lru_conv_w, lru_conv_b, lru_wa, lru_ba, lru_wx, lru_bx,
           lru_lambda, ssd_conv_w, ssd_conv_b, ssd_a_log, ssd_dt_bias, ssd_d, ssd_norm_g, w_out, norm2_g,
           ffn_w_up, ffn_conv_w, ffn_conv_b, ffn_w_down, final_norm_g):
    b, l, d = x.shape
    lctx = ctx.shape[1]
    lw = lru_conv_w.shape[2]
    sw = SSD_HEADS * SSD_HEAD_DIM
    cdim = ssd_conv_w.shape[2]
    rows = l // GRID_W
    assert ada_w.shape[0] == 1, "single layer"
    assert rows == SSD_CHUNK, "an SSD chunk is one column of the latent grid"
    assert lctx % SSD_CHUNK == 0 and l % TILE == 0

    s_in = jnp.zeros((SUBLANES, d), F32).at[:b].set(c).at[b].set(c_ctx)
    mod = _mod_call(s_in, ada_w[0], ada_b)
    mod_lat = [m.reshape(b, 1, d) for m in jnp.split(mod[:b], 6, axis=-1)]
    mod_ctx = [jnp.broadcast_to(m.reshape(1, 1, d), (b, 1, d)) for m in jnp.split(mod[b:b + 1], 6, axis=-1)]
    sh1, sc1, g1, sh2, sc2, g2 = mod_lat
    csh1, csc1 = mod_ctx[0], mod_ctx[1]

    wi = w_in[0].astype(BF16)
    o1, o2, o3, o4 = lw, 2 * lw, 2 * lw + sw, 2 * lw + sw + cdim
    w_lu, w_lg, w_z, w_xbc, w_dt = wi[:, :o1], wi[:, o1:o2], wi[:, o2:o3], wi[:, o3:o4], _pad_lanes(wi[:, o4:])
    seg = TILE // SUBLANES
    interleaved = lambda wd: ("groups", (l // SUBLANES, SUBLANES, wd), (seg, SUBLANES, wd), lambda i: (i, 0, 0))
    column = lambda wd: ("groups", (GRID_W, rows, wd), (GRID_W, SUBLANES, wd), lambda i: (0, i, 0))
    lu_l, lg_l, z_l, xbc_l, dt_l = _inproj_call(x, sh1, sc1, norm1_g, [
        (w_lu,) + interleaved(lw), (w_lg,) + interleaved(lw),
        (w_z, "raster", (l, sw), (TILE, sw), lambda i: (i, 0)),
        (w_xbc,) + column(cdim), (w_dt,) + column(LANES)], TILE)
    cseg = lctx // SUBLANES
    nctx = lctx // SSD_CHUNK
    chunks = lambda wd: ("chunks", (nctx, SSD_CHUNK, wd), (nctx, SSD_CHUNK, wd), lambda i: (0, 0, 0))
    lu_c, xbc_c, dt_c = _inproj_call(ctx, csh1, csc1, norm1_g, [
        (w_lu, "groups", (cseg, SUBLANES, lw), (cseg, SUBLANES, lw), lambda i: (0, 0, 0)),
        (w_xbc,) + chunks(cdim), (w_dt,) + chunks(LANES)], lctx)

    hpg = 4
    lcw, lcb = lru_conv_w[0], lru_conv_b
    zeros_w = jnp.zeros((b, 1, lw), F32)
    lru_args = []
    for dr in range(2):
        lru_args.append((lcw, lcb, _block_diag_gates(lru_wa[0, dr], lru_wx[0, dr], hpg),
                         lru_ba[0, dr][None], lru_bx[0, dr][None], lru_lambda[0, dr][None]))
    _, hc_f = _lru_call(lu_c, zeros_w, *lru_args[0], seg=cseg, rev=False)
    _, hc_b = _lru_call(lu_c, zeros_w, *lru_args[1], seg=cseg, rev=True)
    h_b, _ = _lru_call(lu_l, hc_b, *lru_args[1], seg=seg, rev=True)
    lru_out, _ = _lru_call(lu_l, hc_f, *lru_args[0], seg=seg, rev=False, hb=h_b, lg=lg_l)

    ssd_params = (ssd_conv_w[0], ssd_conv_b,
                  _pad_lanes(ssd_dt_bias[0].reshape(1, -1)), _pad_lanes(ssd_dt_bias[0].reshape(1, -1)).T,
                  _pad_lanes(ssd_a_log[0].reshape(1, -1)), _pad_lanes(ssd_a_log[0].reshape(1, -1)).T)
    zero_state = jnp.zeros((b, SSD_STATE, sw), F32)
    (sc_f,) = _ssd_call(xbc_c, dt_c, zero_state, ssd_params, rev=False, mode="state")
    (sc_b,) = _ssd_call(xbc_c, dt_c, zero_state, ssd_params, rev=True, mode="state")
    dsk = jnp.repeat(ssd_d[0], SSD_HEAD_DIM)[None]
    y_b, xc_l, _ = _ssd_call(xbc_l, dt_l, sc_b, ssd_params, rev=True, mode="first", dsk=dsk)
    y_l, _ = _ssd_call(xc_l, dt_l, sc_f, ssd_params, rev=False, mode="second", yb=y_b)

    wo = w_out[0].astype(BF16)
    x1 = _outproj_call(x, lru_out, y_l, z_l, g1, ssd_norm_g, wo[:lw], wo[lw:])
    return _ffn_call(x1, sh2, sc2, g2, norm2_g, final_norm_g[None], ffn_w_up[0].astype(BF16), ffn_conv_w[0],
                     ffn_conv_b, ffn_w_down[0].astype(BF16), 512, 512)
```

```python
import functools

import jax
import jax.numpy as jnp
from jax import lax
from jax.experimental import pallas as pl
from jax.experimental.pallas import tpu as pltpu

F32 = jnp.float32
BF16 = jnp.bfloat16

EPS = 1e-6
GRID_W = 64
LRU_C = 8.0
SSD_HEADS = 16
SSD_HEAD_DIM = 64
SSD_GROUPS = 2
SSD_STATE = 128
SSD_CHUNK = 128
SUBLANES = 8
LANES = 128
VMEM_LIMIT = 56 * 1024 * 1024
TILE = SUBLANES * GRID_W


def _cparams(sem):
    return pltpu.CompilerParams(dimension_semantics=sem, vmem_limit_bytes=VMEM_LIMIT)


def _split_bf16(v, terms):
    parts = []
    rem = v
    for _ in range(terms):
        p = rem.astype(BF16)
        parts.append(p)
        rem = rem - p.astype(F32)
    return parts


def _dot(a, b):
    return jnp.dot(a, b, preferred_element_type=F32)


def _sigmoid(x):
    return 0.5 * jnp.tanh(0.5 * x) + 0.5


def _silu(x):
    h = 0.5 * x
    return h * jnp.tanh(h) + h


def _gather_groups(ref):
    return jnp.concatenate([ref[:, s, :] for s in range(SUBLANES)], axis=0)


def _scatter_groups(ref, val):
    seg = val.shape[0] // SUBLANES
    for s in range(SUBLANES):
        ref[:, s, :] = val[s * seg:(s + 1) * seg, :]


def _mod_kernel(s_ref, w_ref, b_ref, o_ref):
    s = _silu(s_ref[...])
    s_hi, s_lo = _split_bf16(s, 2)
    w_hi, w_lo = _split_bf16(w_ref[...], 2)
    acc = _dot(s_hi, w_hi) + _dot(s_hi, w_lo) + _dot(s_lo, w_hi)
    o_ref[...] = acc + b_ref[...]


def _mod_call(s, w, b):
    rows, d = s.shape
    n = w.shape[1]
    nb = 1536
    return pl.pallas_call(
        _mod_kernel,
        grid=(n // nb,),
        in_specs=[pl.BlockSpec((rows, d), lambda j: (0, 0)),
                  pl.BlockSpec((d, nb), lambda j: (0, j)),
                  pl.BlockSpec((1, nb), lambda j: (0, j))],
        out_specs=pl.BlockSpec((rows, nb), lambda j: (0, j)),
        out_shape=jax.ShapeDtypeStruct((rows, n), F32),
        compiler_params=_cparams(("parallel",)),
        name="mod",
    )(s, w, b)


def _norm_mod(x, g, shift, scale):
    ms = jnp.mean(x * x, axis=-1, keepdims=True)
    y = x * lax.rsqrt(ms + EPS) * g
    return y * (1.0 + scale) + shift


def _inproj_kernel(x_ref, sh_ref, sc_ref, g_ref, *refs, kinds):
    n = len(kinds)
    w_refs, o_refs = refs[:n], refs[n:]
    h = _norm_mod(x_ref[...], g_ref[...], sh_ref[...], sc_ref[...]).astype(BF16)
    for w_ref, o_ref, kind in zip(w_refs, o_refs, kinds):
        res = _dot(h, w_ref[...])
        if kind == "raster":
            o_ref[...] = res
        elif kind == "chunks":
            rows = o_ref.shape[1]
            for ci in range(o_ref.shape[0]):
                o_ref[ci] = res[ci * rows:(ci + 1) * rows, :]
        else:
            _scatter_groups(o_ref, res)


def _inproj_call(x, shift, scale, g, outs, tm):
    b, l, d = x.shape
    row = lambda: pl.BlockSpec((None, 1, d), lambda bi, i: (bi, 0, 0))
    in_specs = [pl.BlockSpec((None, tm, d), lambda bi, i: (bi, i, 0)), row(), row(),
                pl.BlockSpec((1, d), lambda bi, i: (0, 0))]
    in_specs += [pl.BlockSpec(o[0].shape, lambda bi, i: (0, 0)) for o in outs]
    out_specs, out_shape = [], []
    for _, _, shape, block, imap in outs:
        out_specs.append(pl.BlockSpec((None,) + block, functools.partial(lambda bi, i, f: (bi,) + f(i), f=imap)))
        out_shape.append(jax.ShapeDtypeStruct((b,) + shape, F32))
    kern = functools.partial(_inproj_kernel, kinds=tuple(o[1] for o in outs))
    return pl.pallas_call(
        kern,
        grid=(b, l // tm),
        in_specs=in_specs,
        out_specs=out_specs,
        out_shape=out_shape,
        compiler_params=_cparams(("parallel", "parallel")),
        name="inproj",
    )(x, shift, scale, g, *[o[0] for o in outs])


LRU_RB = 64
SQRT_FLOOR = 1e-30
LOG2_E = 1.4426950408889634
HALO = 3 * SUBLANES


def _lru_kernel(*refs, seg, nt, rev, final):
    (lu_ref, prev_ref, next_ref, h0_ref, cw_ref, cb_ref, wg_ref, ba_ref, bx_ref, lam_ref) = refs[:10]
    pos = 10
    if final:
        hb_ref, lg_ref = refs[pos:pos + 2]
        pos += 2
    o_ref, hl_ref = refs[pos:pos + 2]
    xe_ref, a_ref, u_ref, st_ref, carry_ref = refs[pos + 2:]

    t = seg * SUBLANES
    i = pl.program_id(1)
    c = (nt - 1 - i) if rev else i

    @pl.when(i == 0)
    def _():
        carry_ref[...] = h0_ref[...]

    width = cw_ref.shape[1]
    row = lax.broadcasted_iota(jnp.int32, (SUBLANES, width), 0)

    def before(own, other):
        return jnp.where(row == 0, pltpu.roll(jnp.where(c > 0, other, 0.0), 1, 0), pltpu.roll(own, 1, 0))

    xe_ref[0:SUBLANES, :] = before(lu_ref[seg - 2], prev_ref[0])
    xe_ref[SUBLANES:2 * SUBLANES, :] = before(lu_ref[seg - 1], prev_ref[1])
    xe_ref[2 * SUBLANES:2 * SUBLANES + t, :] = lu_ref[...].reshape(t, width)
    xe_ref[2 * SUBLANES + t:HALO + t, :] = jnp.where(
        row == SUBLANES - 1, pltpu.roll(jnp.where(c < nt - 1, next_ref[0], 0.0), SUBLANES - 1, 0),
        pltpu.roll(lu_ref[0], SUBLANES - 1, 0))

    cw = cw_ref[...]
    cb = cb_ref[...]
    half_ba = 0.5 * ba_ref[...]
    half_bx = 0.5 * bx_ref[...]
    log_decay = -LRU_C * jax.nn.softplus(-lam_ref[...])
    c2 = log_decay * (0.5 * LOG2_E)
    ngroups = wg_ref.shape[0]
    gw = width // ngroups

    def gate_block(rb, carry):
        r0 = pl.multiple_of(rb * LRU_RB, LRU_RB)
        xc = cb
        for k in range(4):
            xc = xc + cw[k:k + 1, :] * xe_ref[pl.ds(r0 + k * SUBLANES, LRU_RB), :]
        for g in range(ngroups):
            sl = slice(g * gw, (g + 1) * gw)
            xg = xc[:, sl]
            pre = _dot(xg.astype(BF16), wg_ref[g])
            t_r = jnp.tanh(pre[:, :gw] + half_ba[:, sl])
            t_i = jnp.tanh(pre[:, gw:] + half_bx[:, sl])
            a = jnp.exp2(t_r * c2[:, sl] + c2[:, sl])
            y = 1.0 - a * a
            hx = 0.5 * xg
            u = (y * lax.rsqrt(jnp.maximum(y, SQRT_FLOOR))) * (hx * t_i + hx)
            a_ref[pl.ds(r0, LRU_RB), sl] = a
            u_ref[pl.ds(r0, LRU_RB), sl] = u
        return carry

    lax.fori_loop(0, t // LRU_RB, gate_block, 0)

    def slab(jj):
        j = (seg - 1 - jj) if rev else jj
        return j, pl.multiple_of(j * SUBLANES, SUBLANES)

    def seg_totals(jj, hp):
        h, p = hp
        _, r0 = slab(jj)
        a8 = a_ref[pl.ds(r0, SUBLANES), :]
        return a8 * h + u_ref[pl.ds(r0, SUBLANES), :], p * a8

    h_end, p_end = lax.fori_loop(0, seg, seg_totals,
                                 (jnp.zeros((SUBLANES, width), F32), jnp.ones((SUBLANES, width), F32)), unroll=4)

    cur = carry_ref[...]
    for r in (range(SUBLANES - 1, -1, -1) if rev else range(SUBLANES)):
        st_ref[r:r + 1, :] = cur
        cur = p_end[r:r + 1, :] * cur + h_end[r:r + 1, :]
    carry_ref[...] = cur
    hl_ref[...] = cur

    def emit(jj, h):
        j, r0 = slab(jj)
        h = a_ref[pl.ds(r0, SUBLANES), :] * h + u_ref[pl.ds(r0, SUBLANES), :]
        if final:
            o_ref[j] = (h + hb_ref[j]) * jax.nn.gelu(lg_ref[j])
        else:
            o_ref[j] = h
        return h

    lax.fori_loop(0, seg, emit, st_ref[...], unroll=4)


def _lru_call(lu, h0, cw, cb, wg, ba, bx, lam, *, seg, rev, hb=None, lg=None):
    b, n8, _, w = lu.shape
    nt = n8 // seg
    final = hb is not None
    cidx = (lambda i: nt - 1 - i) if rev else (lambda i: i)
    main = lambda: pl.BlockSpec((None, seg, SUBLANES, w), lambda bi, i: (bi, cidx(i), 0, 0))
    vec = lambda a: pl.BlockSpec(a.shape, lambda bi, i: (0,) * a.ndim)
    in_specs = [
        main(),
        pl.BlockSpec((None, 2, SUBLANES, w), lambda bi, i: (bi, jnp.maximum(cidx(i) * (seg // 2) - 1, 0), 0, 0)),
        pl.BlockSpec((None, 1, SUBLANES, w), lambda bi, i: (bi, jnp.minimum((cidx(i) + 1) * seg, n8 - 1), 0, 0)),
        pl.BlockSpec((None, 1, w), lambda bi, i: (bi, 0, 0)),
        vec(cw), vec(cb), vec(wg), vec(ba), vec(bx), vec(lam),
    ]
    args = [lu, lu, lu, h0, cw, cb, wg, ba, bx, lam]
    if final:
        in_specs += [main(), main()]
        args += [hb, lg]
    t = seg * SUBLANES
    kern = functools.partial(_lru_kernel, seg=seg, nt=nt, rev=rev, final=final)
    return pl.pallas_call(
        kern,
        grid=(b, nt),
        in_specs=in_specs,
        out_specs=[main(), pl.BlockSpec((None, 1, w), lambda bi, i: (bi, 0, 0))],
        out_shape=[jax.ShapeDtypeStruct(lu.shape, F32), jax.ShapeDtypeStruct((b, 1, w), F32)],
        scratch_shapes=[pltpu.VMEM((t + HALO, w), F32), pltpu.VMEM((t, w), F32), pltpu.VMEM((t, w), F32),
                        pltpu.VMEM((SUBLANES, w), F32), pltpu.VMEM((1, w), F32)],
        compiler_params=_cparams(("parallel", "arbitrary")),
        name="lru_rev" if rev else "lru_fwd",
    )(*args)


def _ssd_kernel(*refs, batched, nb, **kw):
    for bi in range(nb):
        _ssd_chunk(*[r.at[bi] if is_b else r for r, is_b in zip(refs, batched)], **kw)


def _ssd_chunk(*refs, nc, rev, mode):
    if mode == "second":
        xc_ref, dt_ref, h0_ref, bias_r_ref, bias_c_ref, alog_r_ref, alog_c_ref, yb_ref, o_ref, hl_ref, hst_ref = refs
    else:
        (xm_ref, xp_ref, xn_ref, dt_ref, h0_ref, cw_ref, cb_ref, bias_r_ref, bias_c_ref,
         alog_r_ref, alog_c_ref) = refs[:11]
        if mode == "first":
            dsk_ref, o_ref, xc_ref, hl_ref, xb_ref, hst_ref = refs[11:]
        else:
            hl_ref, xb_ref, hst_ref = refs[11:]

    t = SSD_CHUNK
    width = SSD_HEADS * SSD_HEAD_DIM
    gn = SSD_STATE
    i = pl.program_id(0)
    c = (nc - 1 - i) if rev else i
    d = 1 if rev else 0
    tl = 0 if rev else t - 1

    @pl.when(i == 0)
    def _():
        hst_ref[...] = h0_ref[...]

    slabs = []
    for s in range((width + 2 * SSD_GROUPS * gn) // LANES):
        ls = slice(s * LANES, (s + 1) * LANES)
        if mode == "second":
            slabs.append(xc_ref[:, ls])
            continue
        xb_ref[s, 0:SUBLANES, :] = jnp.where(c > 0, xp_ref[:, ls], 0.0)
        xb_ref[s, SUBLANES:SUBLANES + t, :] = xm_ref[:, ls]
        xb_ref[s, SUBLANES + t:, :] = jnp.where(c < nc - 1, xn_ref[:, ls], 0.0)
        xc = cb_ref[:, ls]
        for k in range(4):
            xc = xc + cw_ref[k:k + 1, ls] * xb_ref[s, SUBLANES - 2 + k:SUBLANES - 2 + k + t, :]
        slabs.append(_silu(xc))
        if mode == "first":
            xc_ref[:, ls] = slabs[-1].astype(BF16)
    nx = width // LANES
    x_slabs = slabs[:nx]
    bm = slabs[nx:nx + SSD_GROUPS]
    cm = slabs[nx + SSD_GROUPS:]

    ti = lax.broadcasted_iota(jnp.int32, (t, t), 0)
    si = lax.broadcasted_iota(jnp.int32, (t, t), 1)
    inc = (si >= ti) if rev else (si <= ti)
    inc_b = jnp.where(inc, 1.0, 0.0).astype(BF16)
    inc_t_b = jnp.where((ti >= si) if rev else (ti <= si), 1.0, 0.0).astype(BF16)

    dtraw = dt_ref[...]
    dt = jax.nn.softplus(dtraw + bias_r_ref[...])
    da = dt * (-jnp.exp(alog_r_ref[...]))
    cs = sum(_dot(inc_b, p) for p in _split_bf16(da, 3))
    q0 = SSD_HEADS * d
    dtraw_t = dtraw.T[q0:q0 + SSD_HEADS, :]
    dt_t = jax.nn.softplus(dtraw_t + bias_c_ref[q0:q0 + SSD_HEADS, :])
    da_t = dt_t * (-jnp.exp(alog_c_ref[q0:q0 + SSD_HEADS, :]))
    cs_t = sum(_dot(p, inc_t_b) for p in _split_bf16(da_t, 3))
    w1_t = dt_t * jnp.exp(cs_t[:, tl:tl + 1] - cs_t)
    ecs = jnp.exp(cs)

    lane = lax.broadcasted_iota(jnp.int32, (t, LANES), 1)
    lo = lane < SSD_HEAD_DIM
    hg = SSD_HEADS // SSD_GROUPS
    gw = hg * SSD_HEAD_DIM

    if mode != "state":
        scores = [lax.dot_general(cm[g].astype(BF16), bm[g].astype(BF16),
                                  (((1,), (1,)), ((), ())), preferred_element_type=F32)
                  for g in range(SSD_GROUPS)]
        z_off = [_dot(cm[g].astype(BF16), hst_ref[:, g * gw:(g + 1) * gw].astype(BF16))
                 for g in range(SSD_GROUPS)]
    bm_t = [bm[g].astype(F32).T for g in range(SSD_GROUPS)]
    keep_lo = jnp.where(lo, 1.0, 0.0).astype(BF16)
    keep_hi = jnp.where(lo, 0.0, 1.0).astype(BF16)

    ys = []
    for pr in range(SSD_HEADS // 2):
        g = (2 * pr) // hg
        xp = x_slabs[pr].astype(BF16)
        rhs = jnp.concatenate([xp * keep_lo, xp * keep_hi], axis=0)
        lhs_s = jnp.concatenate([bm_t[g] * w1_t[2 * pr + e:2 * pr + e + 1, :] for e in range(2)],
                                axis=1).astype(BF16)
        if mode != "state":
            ms = []
            for e in range(2):
                hh = 2 * pr + e
                q = q0 + hh
                dec = jnp.exp(cs[:, q:q + 1] - cs_t[hh:hh + 1, :])
                lmat = jnp.where(inc, dec, 0.0) * dt_t[hh:hh + 1, :]
                ms.append((scores[g] * lmat).astype(BF16))
            lhs = jnp.concatenate([jnp.concatenate(ms, axis=1), lhs_s], axis=0)
            res = _dot(lhs, rhs)
            y_diag, s_new = res[:t], res[t:]
            e_pair = jnp.where(lo, ecs[:, q0 + 2 * pr:q0 + 2 * pr + 1], ecs[:, q0 + 2 * pr + 1:q0 + 2 * pr + 2])
            col = (pr * LANES) % gw
            ys.append(y_diag + z_off[g][:, col:col + LANES] * e_pair)
        else:
            s_new = _dot(lhs_s, rhs)
        dec_pair = jnp.where(lo[0:1, :], ecs[tl:tl + 1, q0 + 2 * pr:q0 + 2 * pr + 1],
                             ecs[tl:tl + 1, q0 + 2 * pr + 1:q0 + 2 * pr + 2])
        hst_ref[:, pr * LANES:(pr + 1) * LANES] = hst_ref[:, pr * LANES:(pr + 1) * LANES] * dec_pair + s_new

    @pl.when(i == nc - 1)
    def _():
        hl_ref[...] = hst_ref[...]

    if mode == "state":
        return
    y = jnp.concatenate(ys, axis=1)
    if mode == "first":
        o_ref[...] = y + jnp.concatenate(x_slabs, axis=1) * dsk_ref[...]
    else:
        o_ref[...] = y + yb_ref[...]


def _ssd_call(xin, dt, h0, params, *, rev, mode, yb=None, dsk=None):
    cw, cb, bias_r, bias_c, alog_r, alog_c = params
    b, nc, t, cdim = xin.shape
    width = SSD_HEADS * SSD_HEAD_DIM
    cidx = (lambda i: nc - 1 - i) if rev else (lambda i: i)
    tb = t // SUBLANES
    blk = lambda wd: pl.BlockSpec((b, None, t, wd), lambda i: (0, cidx(i), 0, 0))
    prev = pl.BlockSpec((b, None, SUBLANES, cdim), lambda i: (0, jnp.maximum(cidx(i) - 1, 0), tb - 1, 0))
    nxt = pl.BlockSpec((b, None, SUBLANES, cdim), lambda i: (0, jnp.minimum(cidx(i) + 1, nc - 1), 0, 0))
    vec = lambda a: pl.BlockSpec(a.shape, lambda i: (0,) * a.ndim)
    state = lambda: pl.BlockSpec((b, SSD_STATE, width), lambda i: (0, 0, 0))
    small = [bias_r, bias_c, alog_r, alog_c]
    y_shape = jax.ShapeDtypeStruct((b, nc, t, width), F32)
    h_shape = jax.ShapeDtypeStruct((b, SSD_STATE, width), F32)
    scratch = [pltpu.VMEM((b, SSD_STATE, width), F32)]
    if mode == "second":
        in_specs = [blk(cdim), blk(LANES), state()] + [vec(a) for a in small] + [blk(width)]
        args = [xin, dt, h0] + small + [yb]
        batched = [True] * 3 + [False] * 4 + [True]
        out_specs, out_shape = [blk(width), state()], [y_shape, h_shape]
    else:
        in_specs = [blk(cdim), prev, nxt, blk(LANES), state(), vec(cw), vec(cb)] + [vec(a) for a in small]
        args = [xin, xin, xin, dt, h0, cw, cb] + small
        batched = [True] * 5 + [False] * 6
        out_specs, out_shape = [state()], [h_shape]
        scratch = [pltpu.VMEM((b, cdim // LANES, t + 2 * SUBLANES, LANES), F32)] + scratch
        if mode == "first":
            in_specs.append(vec(dsk))
            args.append(dsk)
            batched.append(False)
            out_specs = [blk(width), blk(cdim)] + out_specs
            out_shape = [y_shape, jax.ShapeDtypeStruct((b, nc, t, cdim), BF16)] + out_shape
    batched += [True] * (len(out_specs) + len(scratch))
    kern = functools.partial(_ssd_kernel, batched=tuple(batched), nb=b, nc=nc, rev=rev, mode=mode)
    return pl.pallas_call(
        kern,
        grid=(nc,),
        in_specs=in_specs,
        out_specs=out_specs,
        out_shape=out_shape,
        scratch_shapes=scratch,
        compiler_params=_cparams(("arbitrary",)),
        name=f"ssd_{mode}_{'rev' if rev else 'fwd'}",
    )(*args)


def _outproj_kernel(x_ref, lru_ref, y_ref, z_ref, g_ref, ng_ref, w1_ref, w2_ref, o_ref):
    lru = _gather_groups(lru_ref)
    y = _gather_groups(y_ref)
    zz = z_ref[...]
    gated = y * _silu(zz)
    ms = jnp.mean(gated * gated, axis=-1, keepdims=True)
    ssd = gated * lax.rsqrt(ms + EPS) * ng_ref[...]
    mix = _dot(lru.astype(BF16), w1_ref[...]) + _dot(ssd.astype(BF16), w2_ref[...])
    o_ref[...] = x_ref[...] + g_ref[...] * mix


def _outproj_call(x, lru, y, z, gate, norm_g, w1, w2):
    b, l, d = x.shape
    seg = TILE // SUBLANES
    tok = lambda wd: pl.BlockSpec((None, TILE, wd), lambda bi, i: (bi, i, 0))
    return pl.pallas_call(
        _outproj_kernel,
        grid=(b, l // TILE),
        in_specs=[tok(d),
                  pl.BlockSpec((None, seg, SUBLANES, lru.shape[3]), lambda bi, i: (bi, i, 0, 0)),
                  pl.BlockSpec((None, GRID_W, SUBLANES, y.shape[3]), lambda bi, i: (bi, 0, i, 0)),
                  tok(z.shape[2]),
                  pl.BlockSpec((None, 1, d), lambda bi, i: (bi, 0, 0)),
                  pl.BlockSpec(norm_g.shape, lambda bi, i: (0, 0)),
                  pl.BlockSpec(w1.shape, lambda bi, i: (0, 0)),
                  pl.BlockSpec(w2.shape, lambda bi, i: (0, 0))],
        out_specs=tok(d),
        out_shape=jax.ShapeDtypeStruct((b, l, d), F32),
        compiler_params=_cparams(("parallel", "parallel")),
        name="outproj",
    )(x, lru, y, z, gate, norm_g, w1, w2)


FFN_RB = 128


def _ffn_kernel(xm_ref, xp_ref, xn_ref, sh_ref, sc_ref, gt_ref, ng_ref, fg_ref,
                wup_ref, cwv_ref, cwg_ref, cbv_ref, cbg_ref, wdn_ref, o_ref,
                f_ref, uv_ref, ug_ref, act_ref, acc_ref, *, tm, nt, nj):
    i = pl.program_id(1)
    j = pl.program_id(2)
    nslab = uv_ref.shape[0]
    fb = nslab * LANES
    piece = 2 * LANES

    def up_piece(u_ref, half, q):
        col = pl.multiple_of((half * nj + j) * fb + q * piece, piece)
        res = _dot(f_ref[...], wup_ref[:, pl.ds(col, piece)])
        u_ref[2 * q] = res[:, :LANES]
        u_ref[2 * q + 1] = res[:, LANES:]

    def conv_slab(s):
        ls = slice(s * LANES, (s + 1) * LANES)
        for rb in range(tm // FFN_RB):
            r0 = SUBLANES - 1 + rb * FFN_RB
            val, gate = cbv_ref[:, ls].astype(BF16), cbg_ref[:, ls].astype(BF16)
            for k in range(3):
                val = val + cwv_ref[k:k + 1, ls].astype(BF16) * uv_ref[s, r0 + k:r0 + k + FFN_RB, :].astype(BF16)
                gate = gate + cwg_ref[k:k + 1, ls].astype(BF16) * ug_ref[s, r0 + k:r0 + k + FFN_RB, :].astype(BF16)
            act_ref[rb * FFN_RB:(rb + 1) * FFN_RB, ls] = jax.nn.gelu(gate) * val

    @pl.when(j == 0)
    def _():
        ng, sh, sc = ng_ref[...], sh_ref[...], sc_ref[...]
        fp = _norm_mod(xp_ref[...], ng, sh, sc)
        fn = _norm_mod(xn_ref[...], ng, sh, sc)
        f_ref[0:SUBLANES, :] = jnp.where(i > 0, fp, 0.0).astype(BF16)
        f_ref[SUBLANES:SUBLANES + tm, :] = _norm_mod(xm_ref[...], ng, sh, sc).astype(BF16)
        f_ref[SUBLANES + tm:, :] = jnp.where(i < nt - 1, fn, 0.0).astype(BF16)
        acc_ref[...] = jnp.zeros_like(acc_ref)

    for q in range(nslab // 2):
        up_piece(uv_ref, 0, q)
        up_piece(ug_ref, 1, q)
        conv_slab(2 * q)
        conv_slab(2 * q + 1)
    acc_ref[...] += _dot(act_ref[...], wdn_ref[j])

    @pl.when(j == nj - 1)
    def _():
        x2 = xm_ref[...] + gt_ref[...] * acc_ref[...]
        ms = jnp.mean(x2 * x2, axis=-1, keepdims=True)
        o_ref[...] = x2 * lax.rsqrt(ms + EPS) * fg_ref[...]


def _ffn_call(x1, shift, scale, gate, norm_g, final_g, w_up, conv_w, conv_b, w_down, tm, fb):
    b, l, d = x1.shape
    dff = w_down.shape[0]
    nt = l // tm
    nj = dff // fb
    tb = tm // SUBLANES
    nb8 = l // SUBLANES
    row = lambda: pl.BlockSpec((None, 1, d), lambda bi, i, j: (bi, 0, 0))
    vec = lambda: pl.BlockSpec((1, d), lambda bi, i, j: (0, 0))
    w_down = w_down.reshape(nj, fb, d)
    resident = lambda a: pl.BlockSpec(a.shape, lambda bi, i, j: (0,) * a.ndim, pipeline_mode=pl.Buffered(1))
    in_specs = [
        pl.BlockSpec((None, tm, d), lambda bi, i, j: (bi, i, 0)),
        pl.BlockSpec((None, SUBLANES, d), lambda bi, i, j: (bi, jnp.maximum(i * tb - 1, 0), 0)),
        pl.BlockSpec((None, SUBLANES, d), lambda bi, i, j: (bi, jnp.minimum((i + 1) * tb, nb8 - 1), 0)),
        row(), row(), row(), vec(), vec(),
        resident(w_up),
        pl.BlockSpec((3, fb), lambda bi, i, j: (0, j)),
        pl.BlockSpec((3, fb), lambda bi, i, j: (0, nj + j)),
        pl.BlockSpec((1, fb), lambda bi, i, j: (0, j)),
        pl.BlockSpec((1, fb), lambda bi, i, j: (0, nj + j)),
        resident(w_down),
    ]
    kern = functools.partial(_ffn_kernel, tm=tm, nt=nt, nj=nj)
    slab = lambda: pltpu.VMEM((fb // LANES, tm + 2 * SUBLANES, LANES), F32)
    return pl.pallas_call(
        kern,
        grid=(b, nt, nj),
        in_specs=in_specs,
        out_specs=pl.BlockSpec((None, tm, d), lambda bi, i, j: (bi, i, 0)),
        out_shape=jax.ShapeDtypeStruct((b, l, d), F32),
        scratch_shapes=[pltpu.VMEM((tm + 2 * SUBLANES, d), BF16), slab(), slab(),
                        pltpu.VMEM((tm, fb), BF16), pltpu.VMEM((tm, d), F32)],
        compiler_params=_cparams(("parallel", "parallel", "arbitrary")),
        name="ffn",
    )(x1, x1, x1, shift, scale, gate, norm_g, final_g, w_up, conv_w, conv_w, conv_b, conv_b, w_down)


def _block_diag_gates(wa, wx, heads_per_group):
    h, hd, _ = wa.shape
    ng = h // heads_per_group
    eye = jnp.eye(heads_per_group, dtype=wa.dtype)

    def bd(w):
        w = w.reshape(ng, heads_per_group, hd, hd)
        return jnp.einsum('gaij,ab->gaibj', w, eye).reshape(ng, heads_per_group * hd, heads_per_group * hd)

    return (0.5 * jnp.concatenate([bd(wa), bd(wx)], axis=-1)).astype(BF16)


def _pad_lanes(v, n=LANES):
    return jnp.pad(v, ((0, 0), (0, n - v.shape[1])))


def kernel(x, c, ctx, c_ctx, ada_w, ada_b, norm1_g, w_in, lru_conv_w, lru_conv_b, lru_wa, lru_ba, lru_wx, lru_bx,
           lru_lambda, ssd_conv_w, ssd_conv_b, ssd_a_log, ssd_dt_bias, ssd_d, ssd_norm_g, w_out, norm2_g,
           ffn_w_up, ffn_conv_w, ffn_conv_b, ffn_w_down, final_norm_g):
    b, l, d = x.shape
    lctx = ctx.shape[1]
    lw = lru_conv_w.shape[2]
    sw = SSD_HEADS * SSD_HEAD_DIM
    cdim = ssd_conv_w.shape[2]
    rows = l // GRID_W
    assert ada_w.shape[0] == 1, "single layer"
    assert rows == SSD_CHUNK, "an SSD chunk is one column of the latent grid"
    assert lctx % SSD_CHUNK == 0 and l % TILE == 0

    s_in = jnp.zeros((SUBLANES, d), F32).at[:b].set(c).at[b].set(c_ctx)
    mod = _mod_call(s_in, ada_w[0], ada_b)
    mod_lat = [m.reshape(b, 1, d) for m in jnp.split(mod[:b], 6, axis=-1)]
    mod_ctx = [jnp.broadcast_to(m.reshape(1, 1, d), (b, 1, d)) for m in jnp.split(mod[b:b + 1], 6, axis=-1)]
    sh1, sc1, g1, sh2, sc2, g2 = mod_lat
    csh1, csc1 = mod_ctx[0], mod_ctx[1]

    wi = w_in[0].astype(BF16)
    o1, o2, o3, o4 = lw, 2 * lw, 2 * lw + sw, 2 * lw + sw + cdim
    w_lu, w_lg, w_z, w_xbc, w_dt = wi[:, :o1], wi[:, o1:o2], wi[:, o2:o3], wi[:, o3:o4], _pad_lanes(wi[:, o4:])
    seg = TILE // SUBLANES
    interleaved = lambda wd: ("groups", (l // SUBLANES, SUBLANES, wd), (seg, SUBLANES, wd), lambda i: (i, 0, 0))
    column = lambda wd: ("groups", (GRID_W, rows, wd), (GRID_W, SUBLANES, wd), lambda i: (0, i, 0))
    lu_l, lg_l, z_l, xbc_l, dt_l = _inproj_call(x, sh1, sc1, norm1_g, [
        (w_lu,) + interleaved(lw), (w_lg,) + interleaved(lw),
        (w_z, "raster", (l, sw), (TILE, sw), lambda i: (i, 0)),
        (w_xbc,) + column(cdim), (w_dt,) + column(LANES)], TILE)
    cseg = lctx // SUBLANES
    nctx = lctx // SSD_CHUNK
    chunks = lambda wd: ("chunks", (nctx, SSD_CHUNK, wd), (nctx, SSD_CHUNK, wd), lambda i: (0, 0, 0))
    lu_c, xbc_c, dt_c = _inproj_call(ctx, csh1, csc1, norm1_g, [
        (w_lu, "groups", (cseg, SUBLANES, lw), (cseg, SUBLANES, lw), lambda i: (0, 0, 0)),
        (w_xbc,) + chunks(cdim), (w_dt,) + chunks(LANES)], lctx)

    hpg = 4
    lcw, lcb = lru_conv_w[0], lru_conv_b
    zeros_w = jnp.zeros((b, 1, lw), F32)
    lru_args = []
    for dr in range(2):
        lru_args.append((lcw, lcb, _block_diag_gates(lru_wa[0, dr], lru_wx[0, dr], hpg),
                         lru_ba[0, dr][None], lru_bx[0, dr][None], lru_lambda[0, dr][None]))
    _, hc_f = _lru_call(lu_c, zeros_w, *lru_args[0], seg=cseg, rev=False)
    _, hc_b = _lru_call(lu_c, zeros_w, *lru_args[1], seg=cseg, rev=True)
    h_b, _ = _lru_call(lu_l, hc_b, *lru_args[1], seg=seg, rev=True)
    lru_out, _ = _lru_call(lu_l, hc_f, *lru_args[0], seg=seg, rev=False, hb=h_b, lg=lg_l)

    ssd_params = (ssd_conv_w[0], ssd_conv_b,
                  _pad_lanes(ssd_dt_bias[0].reshape(1, -1)), _pad_lanes(ssd_dt_bias[0].reshape(1, -1)).T,
                  _pad_lanes(ssd_a_log[0].reshape(1, -1)), _pad_lanes(ssd_a_log[0].reshape(1, -1)).T)
    zero_state = jnp.zeros((b, SSD_STATE, sw), F32)
    (sc_f,) = _ssd_call(xbc_c, dt_c, zero_state, ssd_params, rev=False, mode="state")
    (sc_b,) = _ssd_call(xbc_c, dt_c, zero_state, ssd_params, rev=True, mode="state")
    dsk = jnp.repeat(ssd_d[0], SSD_HEAD_DIM)[None]
    y_b, xc_l, _ = _ssd_call(xbc_l, dt_l, sc_b, ssd_params, rev=True, mode="first", dsk=dsk)
    y_l, _ = _ssd_call(xc_l, dt_l, sc_f, ssd_params, rev=False, mode="second", yb=y_b)

    wo = w_out[0].astype(BF16)
    x1 = _outproj_call(x, lru_out, y_l, z_l, g1, ssd_norm_g, wo[:lw], wo[lw:])
    return _ffn_call(x1, sh2, sc2, g2, norm2_g, final_norm_g[None], ffn_w_up[0].astype(BF16), ffn_conv_w[0],
                     ffn_conv_b, ffn_w_down[0].astype(BF16), 512, 1024)
```

```python
import functools

import jax
import jax.numpy as jnp
from jax import lax
from jax.experimental import pallas as pl
from jax.experimental.pallas import tpu as pltpu

F32 = jnp.float32
BF16 = jnp.bfloat16

EPS = 1e-6
GRID_W = 64
LRU_C = 8.0
SSD_HEADS = 16
SSD_HEAD_DIM = 64
SSD_GROUPS = 2
SSD_STATE = 128
SSD_CHUNK = 128
SUBLANES = 8
LANES = 128
VMEM_LIMIT = 56 * 1024 * 1024
TILE = SUBLANES * GRID_W


def _cparams(sem):
    return pltpu.CompilerParams(dimension_semantics=sem, vmem_limit_bytes=VMEM_LIMIT)


def _split_bf16(v, terms):
    parts = []
    rem = v
    for _ in range(terms):
        p = rem.astype(BF16)
        parts.append(p)
        rem = rem - p.astype(F32)
    return parts


def _dot(a, b):
    return jnp.dot(a, b, preferred_element_type=F32)


def _sigmoid(x):
    return 0.5 * jnp.tanh(0.5 * x) + 0.5


def _silu(x):
    h = 0.5 * x
    return h * jnp.tanh(h) + h


def _gather_groups(ref):
    return jnp.concatenate([ref[:, s, :] for s in range(SUBLANES)], axis=0)


def _scatter_groups(ref, val):
    seg = val.shape[0] // SUBLANES
    for s in range(SUBLANES):
        ref[:, s, :] = val[s * seg:(s + 1) * seg, :]


def _mod_kernel(s_ref, w_ref, b_ref, o_ref):
    s = _silu(s_ref[...])
    s_hi, s_lo = _split_bf16(s, 2)
    w_hi, w_lo = _split_bf16(w_ref[...], 2)
    acc = _dot(s_hi, w_hi) + _dot(s_hi, w_lo) + _dot(s_lo, w_hi)
    o_ref[...] = acc + b_ref[...]


def _mod_call(s, w, b):
    rows, d = s.shape
    n = w.shape[1]
    nb = 1536
    return pl.pallas_call(
        _mod_kernel,
        grid=(n // nb,),
        in_specs=[pl.BlockSpec((rows, d), lambda j: (0, 0)),
                  pl.BlockSpec((d, nb), lambda j: (0, j)),
                  pl.BlockSpec((1, nb), lambda j: (0, j))],
        out_specs=pl.BlockSpec((rows, nb), lambda j: (0, j)),
        out_shape=jax.ShapeDtypeStruct((rows, n), F32),
        compiler_params=_cparams(("parallel",)),
        name="mod",
    )(s, w, b)


def _norm_mod(x, g, shift, scale):
    ms = jnp.mean(x * x, axis=-1, keepdims=True)
    y = x * lax.rsqrt(ms + EPS) * g
    return y * (1.0 + scale) + shift


def _inproj_kernel(x_ref, sh_ref, sc_ref, g_ref, *refs, kinds):
    n = len(kinds)
    w_refs, o_refs = refs[:n], refs[n:]
    h = _norm_mod(x_ref[...], g_ref[...], sh_ref[...], sc_ref[...]).astype(BF16)
    piece = 2 * LANES
    for w_ref, o_ref, kind in zip(w_refs, o_refs, kinds):
        cols = w_ref.shape[1]
        for c0 in range(0, cols, piece):
            cs = slice(c0, min(c0 + piece, cols))
            res = _dot(h, w_ref[:, cs])
            if kind == "raster":
                o_ref[:, cs] = res
            elif kind == "chunks":
                rows = o_ref.shape[1]
                for ci in range(o_ref.shape[0]):
                    o_ref[ci, :, cs] = res[ci * rows:(ci + 1) * rows, :]
            else:
                seg = o_ref.shape[0]
                for s in range(SUBLANES):
                    o_ref[:, s, cs] = res[s * seg:(s + 1) * seg, :]


def _inproj_call(x, shift, scale, g, outs, tm):
    b, l, d = x.shape
    row = lambda: pl.BlockSpec((None, 1, d), lambda bi, i: (bi, 0, 0))
    in_specs = [pl.BlockSpec((None, tm, d), lambda bi, i: (bi, i, 0)), row(), row(),
                pl.BlockSpec((1, d), lambda bi, i: (0, 0))]
    in_specs += [pl.BlockSpec((d, o[0][1]), functools.partial(lambda bi, i, cb: (0, cb), cb=o[0][2])) for o in outs]
    out_specs, out_shape = [], []
    for _, _, shape, block, imap in outs:
        out_specs.append(pl.BlockSpec((None,) + block, functools.partial(lambda bi, i, f: (bi,) + f(i), f=imap)))
        out_shape.append(jax.ShapeDtypeStruct((b,) + shape, F32))
    kern = functools.partial(_inproj_kernel, kinds=tuple(o[1] for o in outs))
    return pl.pallas_call(
        kern,
        grid=(b, l // tm),
        in_specs=in_specs,
        out_specs=out_specs,
        out_shape=out_shape,
        compiler_params=_cparams(("parallel", "parallel")),
        name="inproj",
    )(x, shift, scale, g, *[o[0][0] for o in outs])


LRU_RB = 64
SQRT_FLOOR = 1e-30
LOG2_E = 1.4426950408889634
HALO = 3 * SUBLANES


def _lru_kernel(*refs, seg, nt, rev, final):
    (lu_ref, prev_ref, next_ref, h0_ref, cw_ref, cb_ref, wg_ref, ba_ref, bx_ref, lam_ref) = refs[:10]
    pos = 10
    if final:
        hb_ref, lg_ref = refs[pos:pos + 2]
        pos += 2
    o_ref, hl_ref = refs[pos:pos + 2]
    xe_ref, a_ref, u_ref, st_ref, carry_ref = refs[pos + 2:]

    t = seg * SUBLANES
    i = pl.program_id(1)
    c = (nt - 1 - i) if rev else i

    @pl.when(i == 0)
    def _():
        carry_ref[...] = h0_ref[...]

    width = cw_ref.shape[1]
    row = lax.broadcasted_iota(jnp.int32, (SUBLANES, width), 0)

    def before(own, other):
        return jnp.where(row == 0, pltpu.roll(jnp.where(c > 0, other, 0.0), 1, 0), pltpu.roll(own, 1, 0))

    xe_ref[0:SUBLANES, :] = before(lu_ref[seg - 2], prev_ref[0])
    xe_ref[SUBLANES:2 * SUBLANES, :] = before(lu_ref[seg - 1], prev_ref[1])
    xe_ref[2 * SUBLANES:2 * SUBLANES + t, :] = lu_ref[...].reshape(t, width)
    xe_ref[2 * SUBLANES + t:HALO + t, :] = jnp.where(
        row == SUBLANES - 1, pltpu.roll(jnp.where(c < nt - 1, next_ref[0], 0.0), SUBLANES - 1, 0),
        pltpu.roll(lu_ref[0], SUBLANES - 1, 0))

    cw = cw_ref[...]
    cb = cb_ref[...]
    half_ba = 0.5 * ba_ref[...]
    half_bx = 0.5 * bx_ref[...]
    log_decay = -LRU_C * jax.nn.softplus(-lam_ref[...])
    c2 = log_decay * (0.5 * LOG2_E)
    ngroups = wg_ref.shape[0]
    gw = width // ngroups

    def gate_block(rb, carry):
        r0 = pl.multiple_of(rb * LRU_RB, LRU_RB)
        xc = cb
        for k in range(4):
            xc = xc + cw[k:k + 1, :] * xe_ref[pl.ds(r0 + k * SUBLANES, LRU_RB), :]
        for g in range(ngroups):
            sl = slice(g * gw, (g + 1) * gw)
            xg = xc[:, sl]
            pre = _dot(xg.astype(BF16), wg_ref[g])
            t_r = jnp.tanh(pre[:, :gw] + half_ba[:, sl])
            t_i = jnp.tanh(pre[:, gw:] + half_bx[:, sl])
            a = jnp.exp2(t_r * c2[:, sl] + c2[:, sl])
            y = 1.0 - a * a
            hx = 0.5 * xg
            u = (y * lax.rsqrt(jnp.maximum(y, SQRT_FLOOR))) * (hx * t_i + hx)
            a_ref[pl.ds(r0, LRU_RB), sl] = a
            u_ref[pl.ds(r0, LRU_RB), sl] = u
        return carry

    lax.fori_loop(0, t // LRU_RB, gate_block, 0)

    def slab(jj):
        j = (seg - 1 - jj) if rev else jj
        return j, pl.multiple_of(j * SUBLANES, SUBLANES)

    def seg_totals(jj, hp):
        h, p = hp
        _, r0 = slab(jj)
        a8 = a_ref[pl.ds(r0, SUBLANES), :]
        return a8 * h + u_ref[pl.ds(r0, SUBLANES), :], p * a8

    h_end, p_end = lax.fori_loop(0, seg, seg_totals,
                                 (jnp.zeros((SUBLANES, width), F32), jnp.ones((SUBLANES, width), F32)), unroll=4)

    cur = carry_ref[...]
    for r in (range(SUBLANES - 1, -1, -1) if rev else range(SUBLANES)):
        st_ref[r:r + 1, :] = cur
        cur = p_end[r:r + 1, :] * cur + h_end[r:r + 1, :]
    carry_ref[...] = cur
    hl_ref[...] = cur

    def emit(jj, h):
        j, r0 = slab(jj)
        h = a_ref[pl.ds(r0, SUBLANES), :] * h + u_ref[pl.ds(r0, SUBLANES), :]
        if final:
            o_ref[j] = (h + hb_ref[j]) * jax.nn.gelu(lg_ref[j])
        else:
            o_ref[j] = h
        return h

    lax.fori_loop(0, seg, emit, st_ref[...], unroll=4)


def _lru_call(lu, h0, cw, cb, wg, ba, bx, lam, *, seg, rev, hb=None, lg=None):
    b, n8, _, w = lu.shape
    nt = n8 // seg
    final = hb is not None
    cidx = (lambda i: nt - 1 - i) if rev else (lambda i: i)
    main = lambda: pl.BlockSpec((None, seg, SUBLANES, w), lambda bi, i: (bi, cidx(i), 0, 0))
    vec = lambda a: pl.BlockSpec(a.shape, lambda bi, i: (0,) * a.ndim)
    in_specs = [
        main(),
        pl.BlockSpec((None, 2, SUBLANES, w), lambda bi, i: (bi, jnp.maximum(cidx(i) * (seg // 2) - 1, 0), 0, 0)),
        pl.BlockSpec((None, 1, SUBLANES, w), lambda bi, i: (bi, jnp.minimum((cidx(i) + 1) * seg, n8 - 1), 0, 0)),
        pl.BlockSpec((None, 1, w), lambda bi, i: (bi, 0, 0)),
        vec(cw), vec(cb), vec(wg), vec(ba), vec(bx), vec(lam),
    ]
    args = [lu, lu, lu, h0, cw, cb, wg, ba, bx, lam]
    if final:
        in_specs += [main(), main()]
        args += [hb, lg]
    t = seg * SUBLANES
    kern = functools.partial(_lru_kernel, seg=seg, nt=nt, rev=rev, final=final)
    return pl.pallas_call(
        kern,
        grid=(b, nt),
        in_specs=in_specs,
        out_specs=[main(), pl.BlockSpec((None, 1, w), lambda bi, i: (bi, 0, 0))],
        out_shape=[jax.ShapeDtypeStruct(lu.shape, F32), jax.ShapeDtypeStruct((b, 1, w), F32)],
        scratch_shapes=[pltpu.VMEM((t + HALO, w), F32), pltpu.VMEM((t, w), F32), pltpu.VMEM((t, w), F32),
                        pltpu.VMEM((SUBLANES, w), F32), pltpu.VMEM((1, w), F32)],
        compiler_params=_cparams(("parallel", "arbitrary")),
        name="lru_rev" if rev else "lru_fwd",
    )(*args)


def _ssd_kernel(*refs, batched, nb, **kw):
    for bi in range(nb):
        _ssd_chunk(*[r.at[bi] if is_b else r for r, is_b in zip(refs, batched)], **kw)


def _ssd_chunk(*refs, nc, rev, mode):
    if mode == "second":
        xc_ref, dt_ref, h0_ref, bias_r_ref, bias_c_ref, alog_r_ref, alog_c_ref, yb_ref, o_ref, hl_ref, hst_ref = refs
    else:
        (xm_ref, xp_ref, xn_ref, dt_ref, h0_ref, cw_ref, cb_ref, bias_r_ref, bias_c_ref,
         alog_r_ref, alog_c_ref) = refs[:11]
        if mode == "first":
            dsk_ref, o_ref, xc_ref, hl_ref, xb_ref, hst_ref = refs[11:]
        else:
            hl_ref, xb_ref, hst_ref = refs[11:]

    t = SSD_CHUNK
    width = SSD_HEADS * SSD_HEAD_DIM
    gn = SSD_STATE
    i = pl.program_id(0)
    c = (nc - 1 - i) if rev else i
    d = 1 if rev else 0
    tl = 0 if rev else t - 1

    @pl.when(i == 0)
    def _():
        hst_ref[...] = h0_ref[...]

    slabs = []
    for s in range((width + 2 * SSD_GROUPS * gn) // LANES):
        ls = slice(s * LANES, (s + 1) * LANES)
        if mode == "second":
            slabs.append(xc_ref[:, ls])
            continue
        xb_ref[s, 0:SUBLANES, :] = jnp.where(c > 0, xp_ref[:, ls], 0.0)
        xb_ref[s, SUBLANES:SUBLANES + t, :] = xm_ref[:, ls]
        xb_ref[s, SUBLANES + t:, :] = jnp.where(c < nc - 1, xn_ref[:, ls], 0.0)
        xc = cb_ref[:, ls]
        for k in range(4):
            xc = xc + cw_ref[k:k + 1, ls] * xb_ref[s, SUBLANES - 2 + k:SUBLANES - 2 + k + t, :]
        slabs.append(_silu(xc))
        if mode == "first":
            xc_ref[:, ls] = slabs[-1].astype(BF16)
    nx = width // LANES
    x_slabs = slabs[:nx]
    bm = slabs[nx:nx + SSD_GROUPS]
    cm = slabs[nx + SSD_GROUPS:]

    ti = lax.broadcasted_iota(jnp.int32, (t, t), 0)
    si = lax.broadcasted_iota(jnp.int32, (t, t), 1)
    inc = (si >= ti) if rev else (si <= ti)
    inc_b = jnp.where(inc, 1.0, 0.0).astype(BF16)
    inc_t_b = jnp.where((ti >= si) if rev else (ti <= si), 1.0, 0.0).astype(BF16)

    dtraw = dt_ref[...]
    dt = jax.nn.softplus(dtraw + bias_r_ref[...])
    da = dt * (-jnp.exp(alog_r_ref[...]))
    cs = sum(_dot(inc_b, p) for p in _split_bf16(da, 3))
    q0 = SSD_HEADS * d
    dtraw_t = dtraw.T[q0:q0 + SSD_HEADS, :]
    dt_t = jax.nn.softplus(dtraw_t + bias_c_ref[q0:q0 + SSD_HEADS, :])
    da_t = dt_t * (-jnp.exp(alog_c_ref[q0:q0 + SSD_HEADS, :]))
    cs_t = sum(_dot(p, inc_t_b) for p in _split_bf16(da_t, 3))
    w1_t = dt_t * jnp.exp(cs_t[:, tl:tl + 1] - cs_t)
    ecs = jnp.exp(cs)
    cs2 = cs * LOG2_E
    lcs2_t = (cs_t - jnp.log(dt_t)) * LOG2_E

    lane = lax.broadcasted_iota(jnp.int32, (t, LANES), 1)
    lo = lane < SSD_HEAD_DIM
    hg = SSD_HEADS // SSD_GROUPS
    gw = hg * SSD_HEAD_DIM

    if mode != "state":
        scores = [lax.dot_general(cm[g].astype(BF16), bm[g].astype(BF16),
                                  (((1,), (1,)), ((), ())), preferred_element_type=F32)
                  for g in range(SSD_GROUPS)]
        z_off = [_dot(cm[g].astype(BF16), hst_ref[:, g * gw:(g + 1) * gw].astype(BF16))
                 for g in range(SSD_GROUPS)]
    bm_t = [bm[g].astype(F32).T for g in range(SSD_GROUPS)]
    keep_lo = jnp.where(lo, 1.0, 0.0).astype(BF16)
    keep_hi = jnp.where(lo, 0.0, 1.0).astype(BF16)

    ys = []
    for pr in range(SSD_HEADS // 2):
        g = (2 * pr) // hg
        xp = x_slabs[pr].astype(BF16)
        rhs = jnp.concatenate([xp * keep_lo, xp * keep_hi], axis=0)
        lhs_s = jnp.concatenate([bm_t[g] * w1_t[2 * pr + e:2 * pr + e + 1, :] for e in range(2)],
                                axis=1).astype(BF16)
        if mode != "state":
            ms = []
            for e in range(2):
                hh = 2 * pr + e
                q = q0 + hh
                lmat = jnp.where(inc, jnp.exp2(cs2[:, q:q + 1] - lcs2_t[hh:hh + 1, :]), 0.0)
                ms.append((scores[g] * lmat).astype(BF16))
            lhs = jnp.concatenate([jnp.concatenate(ms, axis=1), lhs_s], axis=0)
            res = _dot(lhs, rhs)
            y_diag, s_new = res[:t], res[t:]
            e_pair = jnp.where(lo, ecs[:, q0 + 2 * pr:q0 + 2 * pr + 1], ecs[:, q0 + 2 * pr + 1:q0 + 2 * pr + 2])
            col = (pr * LANES) % gw
            ys.append(y_diag + z_off[g][:, col:col + LANES] * e_pair)
        else:
            s_new = _dot(lhs_s, rhs)
        dec_pair = jnp.where(lo[0:1, :], ecs[tl:tl + 1, q0 + 2 * pr:q0 + 2 * pr + 1],
                             ecs[tl:tl + 1, q0 + 2 * pr + 1:q0 + 2 * pr + 2])
        hst_ref[:, pr * LANES:(pr + 1) * LANES] = hst_ref[:, pr * LANES:(pr + 1) * LANES] * dec_pair + s_new

    @pl.when(i == nc - 1)
    def _():
        hl_ref[...] = hst_ref[...]

    if mode == "state":
        return
    y = jnp.concatenate(ys, axis=1)
    if mode == "first":
        o_ref[...] = y + jnp.concatenate(x_slabs, axis=1) * dsk_ref[...]
    else:
        o_ref[...] = y + yb_ref[...]


def _ssd_call(xin, dt, h0, params, *, rev, mode, yb=None, dsk=None):
    cw, cb, bias_r, bias_c, alog_r, alog_c = params
    b, nc, t, cdim = xin.shape
    width = SSD_HEADS * SSD_HEAD_DIM
    cidx = (lambda i: nc - 1 - i) if rev else (lambda i: i)
    tb = t // SUBLANES
    blk = lambda wd: pl.BlockSpec((b, None, t, wd), lambda i: (0, cidx(i), 0, 0))
    prev = pl.BlockSpec((b, None, SUBLANES, cdim), lambda i: (0, jnp.maximum(cidx(i) - 1, 0), tb - 1, 0))
    nxt = pl.BlockSpec((b, None, SUBLANES, cdim), lambda i: (0, jnp.minimum(cidx(i) + 1, nc - 1), 0, 0))
    vec = lambda a: pl.BlockSpec(a.shape, lambda i: (0,) * a.ndim)
    state = lambda: pl.BlockSpec((b, SSD_STATE, width), lambda i: (0, 0, 0))
    small = [bias_r, bias_c, alog_r, alog_c]
    y_shape = jax.ShapeDtypeStruct((b, nc, t, width), F32)
    h_shape = jax.ShapeDtypeStruct((b, SSD_STATE, width), F32)
    scratch = [pltpu.VMEM((b, SSD_STATE, width), F32)]
    if mode == "second":
        in_specs = [blk(cdim), blk(LANES), state()] + [vec(a) for a in small] + [blk(width)]
        args = [xin, dt, h0] + small + [yb]
        batched = [True] * 3 + [False] * 4 + [True]
        out_specs, out_shape = [blk(width), state()], [y_shape, h_shape]
    else:
        in_specs = [blk(cdim), prev, nxt, blk(LANES), state(), vec(cw), vec(cb)] + [vec(a) for a in small]
        args = [xin, xin, xin, dt, h0, cw, cb] + small
        batched = [True] * 5 + [False] * 6
        out_specs, out_shape = [state()], [h_shape]
        scratch = [pltpu.VMEM((b, cdim // LANES, t + 2 * SUBLANES, LANES), F32)] + scratch
        if mode == "first":
            in_specs.append(vec(dsk))
            args.append(dsk)
            batched.append(False)
            out_specs = [blk(width), blk(cdim)] + out_specs
            out_shape = [y_shape, jax.ShapeDtypeStruct((b, nc, t, cdim), BF16)] + out_shape
    batched += [True] * (len(out_specs) + len(scratch))
    kern = functools.partial(_ssd_kernel, batched=tuple(batched), nb=b, nc=nc, rev=rev, mode=mode)
    return pl.pallas_call(
        kern,
        grid=(nc,),
        in_specs=in_specs,
        out_specs=out_specs,
        out_shape=out_shape,
        scratch_shapes=scratch,
        compiler_params=_cparams(("arbitrary",)),
        name=f"ssd_{mode}_{'rev' if rev else 'fwd'}",
    )(*args)


def _outproj_kernel(x_ref, lru_ref, y_ref, z_ref, g_ref, ng_ref, w1_ref, w2_ref, o_ref):
    lru = _gather_groups(lru_ref).astype(BF16)
    mix = _dot(lru, w1_ref[...])
    y = _gather_groups(y_ref)
    gated = y * _silu(z_ref[...])
    ms = jnp.mean(gated * gated, axis=-1, keepdims=True)
    ssd = (gated * lax.rsqrt(ms + EPS) * ng_ref[...]).astype(BF16)
    mix = mix + _dot(ssd, w2_ref[...])
    o_ref[...] = x_ref[...] + g_ref[...] * mix


def _outproj_call(x, lru, y, z, gate, norm_g, w_out):
    b, l, d = x.shape
    seg = TILE // SUBLANES
    half = w_out.shape[0] // 2
    assert lru.shape[3] == half and y.shape[3] == half
    tok = lambda wd: pl.BlockSpec((None, TILE, wd), lambda bi, i: (bi, i, 0))
    return pl.pallas_call(
        _outproj_kernel,
        grid=(b, l // TILE),
        in_specs=[tok(d),
                  pl.BlockSpec((None, seg, SUBLANES, lru.shape[3]), lambda bi, i: (bi, i, 0, 0)),
                  pl.BlockSpec((None, GRID_W, SUBLANES, y.shape[3]), lambda bi, i: (bi, 0, i, 0)),
                  tok(z.shape[2]),
                  pl.BlockSpec((None, 1, d), lambda bi, i: (bi, 0, 0)),
                  pl.BlockSpec(norm_g.shape, lambda bi, i: (0, 0)),
                  pl.BlockSpec((half, d), lambda bi, i: (0, 0)),
                  pl.BlockSpec((half, d), lambda bi, i: (1, 0))],
        out_specs=tok(d),
        out_shape=jax.ShapeDtypeStruct((b, l, d), F32),
        compiler_params=_cparams(("parallel", "parallel")),
        name="outproj",
    )(x, lru, y, z, gate, norm_g, w_out, w_out)


FFN_RB = 128


def _ffn_kernel(xm_ref, xp_ref, xn_ref, sh_ref, sc_ref, gt_ref, ng_ref, fg_ref,
                wup_ref, cwv_ref, cwg_ref, cbv_ref, cbg_ref, wdn_ref, o_ref,
                f_ref, uv_ref, ug_ref, act_ref, acc_ref, *, tm, nt, nj):
    i = pl.program_id(1)
    j = pl.program_id(2)
    nslab = uv_ref.shape[0]
    fb = nslab * LANES
    piece = 2 * LANES

    def up_piece(u_ref, half, q):
        col = pl.multiple_of((half * nj + j) * fb + q * piece, piece)
        res = _dot(f_ref[...], wup_ref[:, pl.ds(col, piece)])
        u_ref[2 * q] = res[:, :LANES]
        u_ref[2 * q + 1] = res[:, LANES:]

    def conv_slab(s):
        ls = slice(s * LANES, (s + 1) * LANES)
        for rb in range(tm // FFN_RB):
            r0 = SUBLANES - 1 + rb * FFN_RB
            val, gate = cbv_ref[:, ls].astype(BF16), cbg_ref[:, ls].astype(BF16)
            for k in range(3):
                val = val + cwv_ref[k:k + 1, ls].astype(BF16) * uv_ref[s, r0 + k:r0 + k + FFN_RB, :].astype(BF16)
                gate = gate + cwg_ref[k:k + 1, ls].astype(BF16) * ug_ref[s, r0 + k:r0 + k + FFN_RB, :].astype(BF16)
            act_ref[rb * FFN_RB:(rb + 1) * FFN_RB, ls] = jax.nn.gelu(gate) * val

    @pl.when(j == 0)
    def _():
        ng, sh, sc = ng_ref[...], sh_ref[...], sc_ref[...]
        fp = _norm_mod(xp_ref[...], ng, sh, sc)
        fn = _norm_mod(xn_ref[...], ng, sh, sc)
        f_ref[0:SUBLANES, :] = jnp.where(i > 0, fp, 0.0).astype(BF16)
        f_ref[SUBLANES:SUBLANES + tm, :] = _norm_mod(xm_ref[...], ng, sh, sc).astype(BF16)
        f_ref[SUBLANES + tm:, :] = jnp.where(i < nt - 1, fn, 0.0).astype(BF16)
        acc_ref[...] = jnp.zeros_like(acc_ref)

    for q in range(nslab // 2):
        up_piece(uv_ref, 0, q)
        up_piece(ug_ref, 1, q)
        conv_slab(2 * q)
        conv_slab(2 * q + 1)
    acc_ref[...] += _dot(act_ref[...], wdn_ref[j])

    @pl.when(j == nj - 1)
    def _():
        x2 = xm_ref[...] + gt_ref[...] * acc_ref[...]
        ms = jnp.mean(x2 * x2, axis=-1, keepdims=True)
        o_ref[...] = x2 * lax.rsqrt(ms + EPS) * fg_ref[...]


def _ffn_call(x1, shift, scale, gate, norm_g, final_g, w_up, conv_w, conv_b, w_down, tm, fb):
    b, l, d = x1.shape
    dff = w_down.shape[0]
    nt = l // tm
    nj = dff // fb
    tb = tm // SUBLANES
    nb8 = l // SUBLANES
    row = lambda: pl.BlockSpec((None, 1, d), lambda bi, i, j: (bi, 0, 0))
    vec = lambda: pl.BlockSpec((1, d), lambda bi, i, j: (0, 0))
    w_down = w_down.reshape(nj, fb, d)
    resident = lambda a: pl.BlockSpec(a.shape, lambda bi, i, j: (0,) * a.ndim, pipeline_mode=pl.Buffered(1))
    in_specs = [
        pl.BlockSpec((None, tm, d), lambda bi, i, j: (bi, i, 0)),
        pl.BlockSpec((None, SUBLANES, d), lambda bi, i, j: (bi, jnp.maximum(i * tb - 1, 0), 0)),
        pl.BlockSpec((None, SUBLANES, d), lambda bi, i, j: (bi, jnp.minimum((i + 1) * tb, nb8 - 1), 0)),
        row(), row(), row(), vec(), vec(),
        resident(w_up),
        pl.BlockSpec((3, fb), lambda bi, i, j: (0, j)),
        pl.BlockSpec((3, fb), lambda bi, i, j: (0, nj + j)),
        pl.BlockSpec((1, fb), lambda bi, i, j: (0, j)),
        pl.BlockSpec((1, fb), lambda bi, i, j: (0, nj + j)),
        resident(w_down),
    ]
    kern = functools.partial(_ffn_kernel, tm=tm, nt=nt, nj=nj)
    slab = lambda: pltpu.VMEM((fb // LANES, tm + 2 * SUBLANES, LANES), F32)
    return pl.pallas_call(
        kern,
        grid=(b, nt, nj),
        in_specs=in_specs,
        out_specs=pl.BlockSpec((None, tm, d), lambda bi, i, j: (bi, i, 0)),
        out_shape=jax.ShapeDtypeStruct((b, l, d), F32),
        scratch_shapes=[pltpu.VMEM((tm + 2 * SUBLANES, d), BF16), slab(), slab(),
                        pltpu.VMEM((tm, fb), BF16), pltpu.VMEM((tm, d), F32)],
        compiler_params=_cparams(("parallel", "parallel", "arbitrary")),
        name="ffn",
    )(x1, x1, x1, shift, scale, gate, norm_g, final_g, w_up, conv_w, conv_w, conv_b, conv_b, w_down)


def _block_diag_gates(wa, wx, heads_per_group):
    h, hd, _ = wa.shape
    ng = h // heads_per_group
    eye = jnp.eye(heads_per_group, dtype=wa.dtype)

    def bd(w):
        w = w.reshape(ng, heads_per_group, hd, hd)
        return jnp.einsum('gaij,ab->gaibj', w, eye).reshape(ng, heads_per_group * hd, heads_per_group * hd)

    return (0.5 * jnp.concatenate([bd(wa), bd(wx)], axis=-1)).astype(BF16)


def _pad_lanes(v, n=LANES):
    return jnp.pad(v, ((0, 0), (0, n - v.shape[1])))


def kernel(x, c, ctx, c_ctx, ada_w, ada_b, norm1_g, w_in, lru_conv_w, lru_conv_b, lru_wa, lru_ba, lru_wx, lru_bx,
           lru_lambda, ssd_conv_w, ssd_conv_b, ssd_a_log, ssd_dt_bias, ssd_d, ssd_norm_g, w_out, norm2_g,
           ffn_w_up, ffn_conv_w, ffn_conv_b, ffn_w_down, final_norm_g):
    b, l, d = x.shape
    lctx = ctx.shape[1]
    lw = lru_conv_w.shape[2]
    sw = SSD_HEADS * SSD_HEAD_DIM
    cdim = ssd_conv_w.shape[2]
    rows = l // GRID_W
    assert ada_w.shape[0] == 1, "single layer"
    assert rows == SSD_CHUNK, "an SSD chunk is one column of the latent grid"
    assert lctx % SSD_CHUNK == 0 and l % TILE == 0

    s_in = jnp.zeros((SUBLANES, d), F32).at[:b].set(c).at[b].set(c_ctx)
    mod = _mod_call(s_in, ada_w[0], ada_b)
    mod_lat = [m.reshape(b, 1, d) for m in jnp.split(mod[:b], 6, axis=-1)]
    mod_ctx = [jnp.broadcast_to(m.reshape(1, 1, d), (b, 1, d)) for m in jnp.split(mod[b:b + 1], 6, axis=-1)]
    sh1, sc1, g1, sh2, sc2, g2 = mod_lat
    csh1, csc1 = mod_ctx[0], mod_ctx[1]

    wi = w_in[0].astype(BF16)
    o1, o2, o3, o4 = lw, 2 * lw, 2 * lw + sw, 2 * lw + sw + cdim
    assert o1 % lw == 0 and o2 % sw == 0 and o3 % cdim == 0
    w_lu, w_lg, w_z, w_xbc = (wi, lw, 0), (wi, lw, o1 // lw), (wi, sw, o2 // sw), (wi, cdim, o3 // cdim)
    w_dt = (_pad_lanes(wi[:, o4:]), LANES, 0)
    seg = TILE // SUBLANES
    interleaved = lambda wd: ("groups", (l // SUBLANES, SUBLANES, wd), (seg, SUBLANES, wd), lambda i: (i, 0, 0))
    column = lambda wd: ("groups", (GRID_W, rows, wd), (GRID_W, SUBLANES, wd), lambda i: (0, i, 0))
    lu_l, lg_l, z_l, xbc_l, dt_l = _inproj_call(x, sh1, sc1, norm1_g, [
        (w_lu,) + interleaved(lw), (w_lg,) + interleaved(lw),
        (w_z, "raster", (l, sw), (TILE, sw), lambda i: (i, 0)),
        (w_xbc,) + column(cdim), (w_dt,) + column(LANES)], TILE)
    cseg = lctx // SUBLANES
    nctx = lctx // SSD_CHUNK
    chunks = lambda wd: ("chunks", (nctx, SSD_CHUNK, wd), (nctx, SSD_CHUNK, wd), lambda i: (0, 0, 0))
    lu_c, xbc_c, dt_c = _inproj_call(ctx, csh1, csc1, norm1_g, [
        (w_lu, "groups", (cseg, SUBLANES, lw), (cseg, SUBLANES, lw), lambda i: (0, 0, 0)),
        (w_xbc,) + chunks(cdim), (w_dt,) + chunks(LANES)], lctx)

    hpg = 4
    lcw, lcb = lru_conv_w[0], lru_conv_b
    zeros_w = jnp.zeros((b, 1, lw), F32)
    lru_args = []
    for dr in range(2):
        lru_args.append((lcw, lcb, _block_diag_gates(lru_wa[0, dr], lru_wx[0, dr], hpg),
                         lru_ba[0, dr][None], lru_bx[0, dr][None], lru_lambda[0, dr][None]))
    _, hc_f = _lru_call(lu_c, zeros_w, *lru_args[0], seg=cseg, rev=False)
    _, hc_b = _lru_call(lu_c, zeros_w, *lru_args[1], seg=cseg, rev=True)
    h_b, _ = _lru_call(lu_l, hc_b, *lru_args[1], seg=seg, rev=True)
    lru_out, _ = _lru_call(lu_l, hc_f, *lru_args[0], seg=seg, rev=False, hb=h_b, lg=lg_l)

    ssd_params = (ssd_conv_w[0], ssd_conv_b,
                  _pad_lanes(ssd_dt_bias[0].reshape(1, -1)), _pad_lanes(ssd_dt_bias[0].reshape(1, -1)).T,
                  _pad_lanes(ssd_a_log[0].reshape(1, -1)), _pad_lanes(ssd_a_log[0].reshape(1, -1)).T)
    zero_state = jnp.zeros((b, SSD_STATE, sw), F32)
    (sc_f,) = _ssd_call(xbc_c, dt_c, zero_state, ssd_params, rev=False, mode="state")
    (sc_b,) = _ssd_call(xbc_c, dt_c, zero_state, ssd_params, rev=True, mode="state")
    dsk = jnp.repeat(ssd_d[0], SSD_HEAD_DIM)[None]
    y_b, xc_l, _ = _ssd_call(xbc_l, dt_l, sc_b, ssd_params, rev=True, mode="first", dsk=dsk)
    y_l, _ = _ssd_call(xc_l, dt_l, sc_f, ssd_params, rev=False, mode="second", yb=y_b)

    x1 = _outproj_call(x, lru_out, y_l, z_l, g1, ssd_norm_g, w_out[0].astype(BF16))
    return _ffn_call(x1, sh2, sc2, g2, norm2_g, final_norm_g[None], ffn_w_up[0].astype(BF16), ffn_conv_w[0],
                     ffn_conv_b, ffn_w_down[0].astype(BF16), 512, 1024)
```

```python
import functools

import jax
import jax.numpy as jnp
from jax import lax
from jax.experimental import pallas as pl
from jax.experimental.pallas import tpu as pltpu

F32 = jnp.float32
BF16 = jnp.bfloat16

EPS = 1e-6
GRID_W = 64
LRU_C = 8.0
SSD_HEADS = 16
SSD_HEAD_DIM = 64
SSD_GROUPS = 2
SSD_STATE = 128
SSD_CHUNK = 128
SUBLANES = 8
LANES = 128
VMEM_LIMIT = 56 * 1024 * 1024
TILE = SUBLANES * GRID_W


def _cparams(sem):
    return pltpu.CompilerParams(dimension_semantics=sem, vmem_limit_bytes=VMEM_LIMIT)


def _split_bf16(v, terms):
    parts = []
    rem = v
    for _ in range(terms):
        p = rem.astype(BF16)
        parts.append(p)
        rem = rem - p.astype(F32)
    return parts


def _dot(a, b):
    return jnp.dot(a, b, preferred_element_type=F32)


def _sigmoid(x):
    return 0.5 * jnp.tanh(0.5 * x) + 0.5


def _silu(x):
    h = 0.5 * x
    return h * jnp.tanh(h) + h


def _gather_groups(ref):
    return jnp.concatenate([ref[:, s, :] for s in range(SUBLANES)], axis=0)


def _scatter_groups(ref, val):
    seg = val.shape[0] // SUBLANES
    for s in range(SUBLANES):
        ref[:, s, :] = val[s * seg:(s + 1) * seg, :]


def _mod_kernel(s_ref, w_ref, b_ref, o_ref):
    s = _silu(s_ref[...])
    s_hi, s_lo = _split_bf16(s, 2)
    w_hi, w_lo = _split_bf16(w_ref[...], 2)
    acc = _dot(s_hi, w_hi) + _dot(s_hi, w_lo) + _dot(s_lo, w_hi)
    o_ref[...] = acc + b_ref[...]


def _mod_call(s, w, b):
    rows, d = s.shape
    n = w.shape[1]
    nb = 1536
    return pl.pallas_call(
        _mod_kernel,
        grid=(n // nb,),
        in_specs=[pl.BlockSpec((rows, d), lambda j: (0, 0)),
                  pl.BlockSpec((d, nb), lambda j: (0, j)),
                  pl.BlockSpec((1, nb), lambda j: (0, j))],
        out_specs=pl.BlockSpec((rows, nb), lambda j: (0, j)),
        out_shape=jax.ShapeDtypeStruct((rows, n), F32),
        compiler_params=_cparams(("parallel",)),
        name="mod",
    )(s, w, b)


def _norm_mod(x, g, shift, scale):
    ms = jnp.mean(x * x, axis=-1, keepdims=True)
    y = x * lax.rsqrt(ms + EPS) * g
    return y * (1.0 + scale) + shift


def _inproj_kernel(x_ref, sh_ref, sc_ref, g_ref, *refs, kinds):
    n = len(kinds)
    w_refs, o_refs = refs[:n], refs[n:]
    ngrp, seg, d = x_ref.shape
    g, sh, sc = g_ref[...], sh_ref[...], sc_ref[...]
    lhs = {}
    if any(k != "groups" for k in kinds):
        lhs["rows"] = _norm_mod(x_ref[...].reshape(ngrp * seg, d), g, sh, sc).astype(BF16)
    if "groups" in kinds:
        xg = jnp.concatenate([x_ref[:, m, :] for m in range(seg)], axis=0)
        lhs["groups"] = _norm_mod(xg, g, sh, sc).astype(BF16)
    piece = 2 * LANES
    for w_ref, o_ref, kind in zip(w_refs, o_refs, kinds):
        cols = w_ref.shape[1]
        for c0 in range(0, cols, piece):
            cs = slice(c0, min(c0 + piece, cols))
            if kind == "groups":
                res = _dot(lhs["groups"], w_ref[:, cs])
                o_ref[:, :, cs] = res.reshape(seg, ngrp, res.shape[1])
                continue
            res = _dot(lhs["rows"], w_ref[:, cs])
            if kind == "raster":
                o_ref[:, cs] = res
            else:
                rows = o_ref.shape[1]
                for ci in range(o_ref.shape[0]):
                    o_ref[ci, :, cs] = res[ci * rows:(ci + 1) * rows, :]


def _inproj_call(x, shift, scale, g, outs, tm):
    b, l, d = x.shape
    seg = tm // SUBLANES
    x = x.reshape(b, l // seg, seg, d)
    row = lambda: pl.BlockSpec((None, 1, d), lambda bi, i: (bi, 0, 0))
    in_specs = [pl.BlockSpec((None, SUBLANES, seg, d), lambda bi, i: (bi, i, 0, 0)), row(), row(),
                pl.BlockSpec((1, d), lambda bi, i: (0, 0))]
    in_specs += [pl.BlockSpec((d, o[0][1]), functools.partial(lambda bi, i, cb: (0, cb), cb=o[0][2])) for o in outs]
    out_specs, out_shape = [], []
    for _, _, shape, block, imap in outs:
        out_specs.append(pl.BlockSpec((None,) + block, functools.partial(lambda bi, i, f: (bi,) + f(i), f=imap)))
        out_shape.append(jax.ShapeDtypeStruct((b,) + shape, F32))
    kern = functools.partial(_inproj_kernel, kinds=tuple(o[1] for o in outs))
    return pl.pallas_call(
        kern,
        grid=(b, l // tm),
        in_specs=in_specs,
        out_specs=out_specs,
        out_shape=out_shape,
        compiler_params=_cparams(("parallel", "parallel")),
        name="inproj",
    )(x, shift, scale, g, *[o[0][0] for o in outs])


LRU_RB = 64
SQRT_FLOOR = 1e-30
LOG2_E = 1.4426950408889634
HALO = 3 * SUBLANES


def _lru_kernel(*refs, seg, nt, rev, final):
    (lu_ref, prev_ref, next_ref, h0_ref, cw_ref, cb_ref, wg_ref, ba_ref, bx_ref, lam_ref) = refs[:10]
    pos = 10
    if final:
        hb_ref, lg_ref = refs[pos:pos + 2]
        pos += 2
    o_ref, hl_ref = refs[pos:pos + 2]
    lo_ref, hi_ref, a_ref, u_ref, st_ref, carry_ref = refs[pos + 2:]

    t = seg * SUBLANES
    i = pl.program_id(1)
    c = (nt - 1 - i) if rev else i

    @pl.when(i == 0)
    def _():
        carry_ref[...] = h0_ref[...]

    width = cw_ref.shape[1]
    row = lax.broadcasted_iota(jnp.int32, (SUBLANES, width), 0)

    spb = LRU_RB // SUBLANES
    nblk = seg // spb
    assert nblk >= 2

    def before(own, other):
        return jnp.where(row == 0, pltpu.roll(jnp.where(c > 0, other, 0.0), 1, 0), pltpu.roll(own, 1, 0))

    lo_ref[0:SUBLANES, :] = before(lu_ref[seg - 2], prev_ref[0])
    lo_ref[SUBLANES:2 * SUBLANES, :] = before(lu_ref[seg - 1], prev_ref[1])
    lo_ref[2 * SUBLANES:, :] = lu_ref[0:spb + 1].reshape((spb + 1) * SUBLANES, width)
    hi_ref[0:(spb + 2) * SUBLANES, :] = lu_ref[seg - spb - 2:seg].reshape((spb + 2) * SUBLANES, width)
    hi_ref[(spb + 2) * SUBLANES:, :] = jnp.where(
        row == SUBLANES - 1, pltpu.roll(jnp.where(c < nt - 1, next_ref[0], 0.0), SUBLANES - 1, 0),
        pltpu.roll(lu_ref[0], SUBLANES - 1, 0))

    cw = cw_ref[...]
    cb = cb_ref[...]
    half_ba = 0.5 * ba_ref[...]
    half_bx = 0.5 * bx_ref[...]
    log_decay = -LRU_C * jax.nn.softplus(-lam_ref[...])
    c2 = log_decay * (0.5 * LOG2_E)
    ngroups = wg_ref.shape[0]
    gw = width // ngroups

    def gate_block(r0, taps):
        xc = cb
        for k in range(4):
            xc = xc + cw[k:k + 1, :] * taps[k]
        for g in range(ngroups):
            sl = slice(g * gw, (g + 1) * gw)
            xg = xc[:, sl]
            pre = _dot(xg.astype(BF16), wg_ref[g])
            t_r = jnp.tanh(pre[:, :gw] + half_ba[:, sl])
            t_i = jnp.tanh(pre[:, gw:] + half_bx[:, sl])
            a = jnp.exp2(t_r * c2[:, sl] + c2[:, sl])
            y = 1.0 - a * a
            hx = 0.5 * xg
            u = (y * lax.rsqrt(jnp.maximum(y, SQRT_FLOOR))) * (hx * t_i + hx)
            a_ref[pl.ds(r0, LRU_RB), sl] = a
            u_ref[pl.ds(r0, LRU_RB), sl] = u

    def edge_taps(ref):
        return [ref[k * SUBLANES:k * SUBLANES + LRU_RB, :] for k in range(4)]

    def interior_block(rb, carry):
        taps = [lu_ref[pl.ds(rb * spb - 2 + k, spb)].reshape(LRU_RB, width) for k in range(4)]
        gate_block(pl.multiple_of(rb * LRU_RB, LRU_RB), taps)
        return carry

    gate_block(0, edge_taps(lo_ref))
    lax.fori_loop(1, nblk - 1, interior_block, 0)
    gate_block((nblk - 1) * LRU_RB, edge_taps(hi_ref))

    def slab(jj):
        j = (seg - 1 - jj) if rev else jj
        return j, pl.multiple_of(j * SUBLANES, SUBLANES)

    def seg_totals(jj, hp):
        h, p = hp
        _, r0 = slab(jj)
        a8 = a_ref[pl.ds(r0, SUBLANES), :]
        return a8 * h + u_ref[pl.ds(r0, SUBLANES), :], p * a8

    h_end, p_end = lax.fori_loop(0, seg, seg_totals,
                                 (jnp.zeros((SUBLANES, width), F32), jnp.ones((SUBLANES, width), F32)), unroll=4)

    cur = carry_ref[...]
    for r in (range(SUBLANES - 1, -1, -1) if rev else range(SUBLANES)):
        st_ref[r:r + 1, :] = cur
        cur = p_end[r:r + 1, :] * cur + h_end[r:r + 1, :]
    carry_ref[...] = cur
    hl_ref[...] = cur

    def emit(jj, h):
        j, r0 = slab(jj)
        h = a_ref[pl.ds(r0, SUBLANES), :] * h + u_ref[pl.ds(r0, SUBLANES), :]
        if final:
            o_ref[j] = (h + hb_ref[j]) * jax.nn.gelu(lg_ref[j])
        else:
            o_ref[j] = h
        return h

    lax.fori_loop(0, seg, emit, st_ref[...], unroll=4)


def _lru_call(lu, h0, cw, cb, wg, ba, bx, lam, *, seg, rev, hb=None, lg=None):
    b, n8, _, w = lu.shape
    nt = n8 // seg
    final = hb is not None
    cidx = (lambda i: nt - 1 - i) if rev else (lambda i: i)
    main = lambda: pl.BlockSpec((None, seg, SUBLANES, w), lambda bi, i: (bi, cidx(i), 0, 0))
    vec = lambda a: pl.BlockSpec(a.shape, lambda bi, i: (0,) * a.ndim)
    in_specs = [
        main(),
        pl.BlockSpec((None, 2, SUBLANES, w), lambda bi, i: (bi, jnp.maximum(cidx(i) * (seg // 2) - 1, 0), 0, 0)),
        pl.BlockSpec((None, 1, SUBLANES, w), lambda bi, i: (bi, jnp.minimum((cidx(i) + 1) * seg, n8 - 1), 0, 0)),
        pl.BlockSpec((None, 1, w), lambda bi, i: (bi, 0, 0)),
        vec(cw), vec(cb), vec(wg), vec(ba), vec(bx), vec(lam),
    ]
    args = [lu, lu, lu, h0, cw, cb, wg, ba, bx, lam]
    if final:
        in_specs += [main(), main()]
        args += [hb, lg]
    t = seg * SUBLANES
    kern = functools.partial(_lru_kernel, seg=seg, nt=nt, rev=rev, final=final)
    return pl.pallas_call(
        kern,
        grid=(b, nt),
        in_specs=in_specs,
        out_specs=[main(), pl.BlockSpec((None, 1, w), lambda bi, i: (bi, 0, 0))],
        out_shape=[jax.ShapeDtypeStruct(lu.shape, F32), jax.ShapeDtypeStruct((b, 1, w), F32)],
        scratch_shapes=[pltpu.VMEM((LRU_RB + HALO, w), F32), pltpu.VMEM((LRU_RB + HALO, w), F32),
                        pltpu.VMEM((t, w), F32), pltpu.VMEM((t, w), F32),
                        pltpu.VMEM((SUBLANES, w), F32), pltpu.VMEM((1, w), F32)],
        compiler_params=_cparams(("parallel", "arbitrary")),
        name="lru_rev" if rev else "lru_fwd",
    )(*args)


def _ssd_kernel(*refs, batched, nb, **kw):
    for bi in range(nb):
        _ssd_chunk(*[r.at[bi] if is_b else r for r, is_b in zip(refs, batched)], **kw)


def _ssd_chunk(*refs, nc, rev, mode):
    if mode == "second":
        xc_ref, dt_ref, h0_ref, bias_r_ref, bias_c_ref, alog_r_ref, alog_c_ref, yb_ref, o_ref, hl_ref, hst_ref = refs
    else:
        (xm_ref, xp_ref, xn_ref, dt_ref, h0_ref, cw_ref, cb_ref, bias_r_ref, bias_c_ref,
         alog_r_ref, alog_c_ref) = refs[:11]
        if mode == "first":
            dsk_ref, o_ref, xc_ref, hl_ref, xb_ref, hst_ref = refs[11:]
        else:
            hl_ref, xb_ref, hst_ref = refs[11:]

    t = SSD_CHUNK
    width = SSD_HEADS * SSD_HEAD_DIM
    gn = SSD_STATE
    i = pl.program_id(0)
    c = (nc - 1 - i) if rev else i
    d = 1 if rev else 0
    tl = 0 if rev else t - 1

    @pl.when(i == 0)
    def _():
        hst_ref[...] = h0_ref[...]

    slabs = []
    for s in range((width + 2 * SSD_GROUPS * gn) // LANES):
        ls = slice(s * LANES, (s + 1) * LANES)
        if mode == "second":
            slabs.append(xc_ref[:, ls])
            continue
        xb_ref[s, 0:SUBLANES, :] = jnp.where(c > 0, xp_ref[:, ls], 0.0)
        xb_ref[s, SUBLANES:SUBLANES + t, :] = xm_ref[:, ls]
        xb_ref[s, SUBLANES + t:, :] = jnp.where(c < nc - 1, xn_ref[:, ls], 0.0)
        xc = cb_ref[:, ls]
        for k in range(4):
            xc = xc + cw_ref[k:k + 1, ls] * xb_ref[s, SUBLANES - 2 + k:SUBLANES - 2 + k + t, :]
        slabs.append(_silu(xc))
        if mode == "first":
            xc_ref[:, ls] = slabs[-1].astype(BF16)
    nx = width // LANES
    x_slabs = slabs[:nx]
    bm = slabs[nx:nx + SSD_GROUPS]
    cm = slabs[nx + SSD_GROUPS:]

    ti = lax.broadcasted_iota(jnp.int32, (t, t), 0)
    si = lax.broadcasted_iota(jnp.int32, (t, t), 1)
    inc = (si >= ti) if rev else (si <= ti)
    inc_b = jnp.where(inc, 1.0, 0.0).astype(BF16)
    inc_t_b = jnp.where((ti >= si) if rev else (ti <= si), 1.0, 0.0).astype(BF16)

    dtraw = dt_ref[...]
    dt = jax.nn.softplus(dtraw + bias_r_ref[...])
    da = dt * (-jnp.exp(alog_r_ref[...]))
    cs = sum(_dot(inc_b, p) for p in _split_bf16(da, 3))
    q0 = SSD_HEADS * d
    dtraw_t = dtraw.T[q0:q0 + SSD_HEADS, :]
    dt_t = jax.nn.softplus(dtraw_t + bias_c_ref[q0:q0 + SSD_HEADS, :])
    da_t = dt_t * (-jnp.exp(alog_c_ref[q0:q0 + SSD_HEADS, :]))
    cs_t = sum(_dot(p, inc_t_b) for p in _split_bf16(da_t, 3))
    w1_t = dt_t * jnp.exp(cs_t[:, tl:tl + 1] - cs_t)
    ecs = jnp.exp(cs)
    cs2 = cs * LOG2_E
    lcs2_t = (cs_t - jnp.log(dt_t)) * LOG2_E

    lane = lax.broadcasted_iota(jnp.int32, (t, LANES), 1)
    lo = lane < SSD_HEAD_DIM
    hg = SSD_HEADS // SSD_GROUPS
    gw = hg * SSD_HEAD_DIM

    if mode != "state":
        scores = [lax.dot_general(cm[g].astype(BF16), bm[g].astype(BF16),
                                  (((1,), (1,)), ((), ())), preferred_element_type=F32)
                  for g in range(SSD_GROUPS)]
        z_off = [_dot(cm[g].astype(BF16), hst_ref[:, g * gw:(g + 1) * gw].astype(BF16))
                 for g in range(SSD_GROUPS)]
    bm_t = [bm[g].astype(F32).T for g in range(SSD_GROUPS)]
    keep_lo = jnp.where(lo, 1.0, 0.0).astype(BF16)
    keep_hi = jnp.where(lo, 0.0, 1.0).astype(BF16)

    ys = []
    for pr in range(SSD_HEADS // 2):
        g = (2 * pr) // hg
        xp = x_slabs[pr].astype(BF16)
        rhs = jnp.concatenate([xp * keep_lo, xp * keep_hi], axis=0)
        lhs_s = jnp.concatenate([bm_t[g] * w1_t[2 * pr + e:2 * pr + e + 1, :] for e in range(2)],
                                axis=1).astype(BF16)
        if mode != "state":
            ms = []
            for e in range(2):
                hh = 2 * pr + e
                q = q0 + hh
                lmat = jnp.where(inc, jnp.exp2(cs2[:, q:q + 1] - lcs2_t[hh:hh + 1, :]), 0.0)
                ms.append((scores[g] * lmat).astype(BF16))
            lhs = jnp.concatenate([jnp.concatenate(ms, axis=1), lhs_s], axis=0)
            res = _dot(lhs, rhs)
            y_diag, s_new = res[:t], res[t:]
            e_pair = jnp.where(lo, ecs[:, q0 + 2 * pr:q0 + 2 * pr + 1], ecs[:, q0 + 2 * pr + 1:q0 + 2 * pr + 2])
            col = (pr * LANES) % gw
            ys.append(y_diag + z_off[g][:, col:col + LANES] * e_pair)
        else:
            s_new = _dot(lhs_s, rhs)
        dec_pair = jnp.where(lo[0:1, :], ecs[tl:tl + 1, q0 + 2 * pr:q0 + 2 * pr + 1],
                             ecs[tl:tl + 1, q0 + 2 * pr + 1:q0 + 2 * pr + 2])
        hst_ref[:, pr * LANES:(pr + 1) * LANES] = hst_ref[:, pr * LANES:(pr + 1) * LANES] * dec_pair + s_new

    @pl.when(i == nc - 1)
    def _():
        hl_ref[...] = hst_ref[...]

    if mode == "state":
        return
    y = jnp.concatenate(ys, axis=1)
    if mode == "first":
        o_ref[...] = y + jnp.concatenate(x_slabs, axis=1) * dsk_ref[...]
    else:
        o_ref[...] = y + yb_ref[...]


def _ssd_call(xin, dt, h0, params, *, rev, mode, yb=None, dsk=None):
    cw, cb, bias_r, bias_c, alog_r, alog_c = params
    b, nc, t, cdim = xin.shape
    width = SSD_HEADS * SSD_HEAD_DIM
    cidx = (lambda i: nc - 1 - i) if rev else (lambda i: i)
    tb = t // SUBLANES
    blk = lambda wd: pl.BlockSpec((b, None, t, wd), lambda i: (0, cidx(i), 0, 0))
    prev = pl.BlockSpec((b, None, SUBLANES, cdim), lambda i: (0, jnp.maximum(cidx(i) - 1, 0), tb - 1, 0))
    nxt = pl.BlockSpec((b, None, SUBLANES, cdim), lambda i: (0, jnp.minimum(cidx(i) + 1, nc - 1), 0, 0))
    vec = lambda a: pl.BlockSpec(a.shape, lambda i: (0,) * a.ndim)
    state = lambda: pl.BlockSpec((b, SSD_STATE, width), lambda i: (0, 0, 0))
    small = [bias_r, bias_c, alog_r, alog_c]
    y_shape = jax.ShapeDtypeStruct((b, nc, t, width), F32)
    h_shape = jax.ShapeDtypeStruct((b, SSD_STATE, width), F32)
    scratch = [pltpu.VMEM((b, SSD_STATE, width), F32)]
    if mode == "second":
        in_specs = [blk(cdim), blk(LANES), state()] + [vec(a) for a in small] + [blk(width)]
        args = [xin, dt, h0] + small + [yb]
        batched = [True] * 3 + [False] * 4 + [True]
        out_specs, out_shape = [blk(width), state()], [y_shape, h_shape]
    else:
        in_specs = [blk(cdim), prev, nxt, blk(LANES), state(), vec(cw), vec(cb)] + [vec(a) for a in small]
        args = [xin, xin, xin, dt, h0, cw, cb] + small
        batched = [True] * 5 + [False] * 6
        out_specs, out_shape = [state()], [h_shape]
        scratch = [pltpu.VMEM((b, cdim // LANES, t + 2 * SUBLANES, LANES), F32)] + scratch
        if mode == "first":
            in_specs.append(vec(dsk))
            args.append(dsk)
            batched.append(False)
            out_specs = [blk(width), blk(cdim)] + out_specs
            out_shape = [y_shape, jax.ShapeDtypeStruct((b, nc, t, cdim), BF16)] + out_shape
    batched += [True] * (len(out_specs) + len(scratch))
    kern = functools.partial(_ssd_kernel, batched=tuple(batched), nb=b, nc=nc, rev=rev, mode=mode)
    return pl.pallas_call(
        kern,
        grid=(nc,),
        in_specs=in_specs,
        out_specs=out_specs,
        out_shape=out_shape,
        scratch_shapes=scratch,
        compiler_params=_cparams(("arbitrary",)),
        name=f"ssd_{mode}_{'rev' if rev else 'fwd'}",
    )(*args)


def _outproj_kernel(x_ref, lru_ref, y_ref, z_ref, g_ref, ng_ref, w1_ref, w2_ref, o_ref):
    lru = _gather_groups(lru_ref).astype(BF16)
    mix = _dot(lru, w1_ref[...])
    y = _gather_groups(y_ref)
    gated = y * _silu(z_ref[...])
    ms = jnp.mean(gated * gated, axis=-1, keepdims=True)
    ssd = (gated * lax.rsqrt(ms + EPS) * ng_ref[...]).astype(BF16)
    mix = mix + _dot(ssd, w2_ref[...])
    o_ref[...] = x_ref[...] + g_ref[...] * mix


def _outproj_call(x, lru, y, z, gate, norm_g, w_out):
    b, l, d = x.shape
    seg = TILE // SUBLANES
    half = w_out.shape[0] // 2
    assert lru.shape[3] == half and y.shape[3] == half
    tok = lambda wd: pl.BlockSpec((None, TILE, wd), lambda bi, i: (bi, i, 0))
    return pl.pallas_call(
        _outproj_kernel,
        grid=(b, l // TILE),
        in_specs=[tok(d),
                  pl.BlockSpec((None, seg, SUBLANES, lru.shape[3]), lambda bi, i: (bi, i, 0, 0)),
                  pl.BlockSpec((None, GRID_W, SUBLANES, y.shape[3]), lambda bi, i: (bi, 0, i, 0)),
                  tok(z.shape[2]),
                  pl.BlockSpec((None, 1, d), lambda bi, i: (bi, 0, 0)),
                  pl.BlockSpec(norm_g.shape, lambda bi, i: (0, 0)),
                  pl.BlockSpec((half, d), lambda bi, i: (0, 0)),
                  pl.BlockSpec((half, d), lambda bi, i: (1, 0))],
        out_specs=tok(d),
        out_shape=jax.ShapeDtypeStruct((b, l, d), F32),
        compiler_params=_cparams(("parallel", "parallel")),
        name="outproj",
    )(x, lru, y, z, gate, norm_g, w_out, w_out)


FFN_RB = 128


def _ffn_kernel(xm_ref, xp_ref, xn_ref, sh_ref, sc_ref, gt_ref, ng_ref, fg_ref,
                wup_ref, cwv_ref, cwg_ref, cbv_ref, cbg_ref, wdn_ref, o_ref,
                f_ref, uv_ref, ug_ref, act_ref, acc_ref, *, tm, nt, nj):
    i = pl.program_id(1)
    j = pl.program_id(2)
    nslab = uv_ref.shape[0]
    fb = nslab * LANES
    piece = 2 * LANES

    def up_piece(u_ref, half, q):
        col = pl.multiple_of((half * nj + j) * fb + q * piece, piece)
        res = _dot(f_ref[...], wup_ref[:, pl.ds(col, piece)])
        u_ref[2 * q] = res[:, :LANES]
        u_ref[2 * q + 1] = res[:, LANES:]

    def conv_slab(s):
        ls = slice(s * LANES, (s + 1) * LANES)
        for rb in range(tm // FFN_RB):
            r0 = SUBLANES - 1 + rb * FFN_RB
            val, gate = cbv_ref[:, ls].astype(BF16), cbg_ref[:, ls].astype(BF16)
            for k in range(3):
                val = val + cwv_ref[k:k + 1, ls].astype(BF16) * uv_ref[s, r0 + k:r0 + k + FFN_RB, :].astype(BF16)
                gate = gate + cwg_ref[k:k + 1, ls].astype(BF16) * ug_ref[s, r0 + k:r0 + k + FFN_RB, :].astype(BF16)
            act_ref[rb * FFN_RB:(rb + 1) * FFN_RB, ls] = jax.nn.gelu(gate) * val

    @pl.when(j == 0)
    def _():
        ng, sh, sc = ng_ref[...], sh_ref[...], sc_ref[...]
        fp = _norm_mod(xp_ref[...], ng, sh, sc)
        fn = _norm_mod(xn_ref[...], ng, sh, sc)
        f_ref[0:SUBLANES, :] = jnp.where(i > 0, fp, 0.0).astype(BF16)
        f_ref[SUBLANES:SUBLANES + tm, :] = _norm_mod(xm_ref[...], ng, sh, sc).astype(BF16)
        f_ref[SUBLANES + tm:, :] = jnp.where(i < nt - 1, fn, 0.0).astype(BF16)
        acc_ref[...] = jnp.zeros_like(acc_ref)

    for q in range(nslab // 2):
        up_piece(uv_ref, 0, q)
        up_piece(ug_ref, 1, q)
        conv_slab(2 * q)
        conv_slab(2 * q + 1)
    acc_ref[...] += _dot(act_ref[...], wdn_ref[j])

    @pl.when(j == nj - 1)
    def _():
        x2 = xm_ref[...] + gt_ref[...] * acc_ref[...]
        ms = jnp.mean(x2 * x2, axis=-1, keepdims=True)
        o_ref[...] = x2 * lax.rsqrt(ms + EPS) * fg_ref[...]


def _ffn_call(x1, shift, scale, gate, norm_g, final_g, w_up, conv_w, conv_b, w_down, tm, fb):
    b, l, d = x1.shape
    dff = w_down.shape[0]
    nt = l // tm
    nj = dff // fb
    tb = tm // SUBLANES
    nb8 = l // SUBLANES
    row = lambda: pl.BlockSpec((None, 1, d), lambda bi, i, j: (bi, 0, 0))
    vec = lambda: pl.BlockSpec((1, d), lambda bi, i, j: (0, 0))
    w_down = w_down.reshape(nj, fb, d)
    resident = lambda a: pl.BlockSpec(a.shape, lambda bi, i, j: (0,) * a.ndim, pipeline_mode=pl.Buffered(1))
    in_specs = [
        pl.BlockSpec((None, tm, d), lambda bi, i, j: (bi, i, 0)),
        pl.BlockSpec((None, SUBLANES, d), lambda bi, i, j: (bi, jnp.maximum(i * tb - 1, 0), 0)),
        pl.BlockSpec((None, SUBLANES, d), lambda bi, i, j: (bi, jnp.minimum((i + 1) * tb, nb8 - 1), 0)),
        row(), row(), row(), vec(), vec(),
        resident(w_up),
        pl.BlockSpec((3, fb), lambda bi, i, j: (0, j)),
        pl.BlockSpec((3, fb), lambda bi, i, j: (0, nj + j)),
        pl.BlockSpec((1, fb), lambda bi, i, j: (0, j)),
        pl.BlockSpec((1, fb), lambda bi, i, j: (0, nj + j)),
        resident(w_down),
    ]
    kern = functools.partial(_ffn_kernel, tm=tm, nt=nt, nj=nj)
    slab = lambda: pltpu.VMEM((fb // LANES, tm + 2 * SUBLANES, LANES), F32)
    return pl.pallas_call(
        kern,
        grid=(b, nt, nj),
        in_specs=in_specs,
        out_specs=pl.BlockSpec((None, tm, d), lambda bi, i, j: (bi, i, 0)),
        out_shape=jax.ShapeDtypeStruct((b, l, d), F32),
        scratch_shapes=[pltpu.VMEM((tm + 2 * SUBLANES, d), BF16), slab(), slab(),
                        pltpu.VMEM((tm, fb), BF16), pltpu.VMEM((tm, d), F32)],
        compiler_params=_cparams(("parallel", "parallel", "arbitrary")),
        name="ffn",
    )(x1, x1, x1, shift, scale, gate, norm_g, final_g, w_up, conv_w, conv_w, conv_b, conv_b, w_down)


def _block_diag_gates(wa, wx, heads_per_group):
    h, hd, _ = wa.shape
    ng = h // heads_per_group
    eye = jnp.eye(heads_per_group, dtype=wa.dtype)

    def bd(w):
        w = w.reshape(ng, heads_per_group, hd, hd)
        return jnp.einsum('gaij,ab->gaibj', w, eye).reshape(ng, heads_per_group * hd, heads_per_group * hd)

    return (0.5 * jnp.concatenate([bd(wa), bd(wx)], axis=-1)).astype(BF16)


def _pad_lanes(v, n=LANES):
    return jnp.pad(v, ((0, 0), (0, n - v.shape[1])))


def kernel(x, c, ctx, c_ctx, ada_w, ada_b, norm1_g, w_in, lru_conv_w, lru_conv_b, lru_wa, lru_ba, lru_wx, lru_bx,
           lru_lambda, ssd_conv_w, ssd_conv_b, ssd_a_log, ssd_dt_bias, ssd_d, ssd_norm_g, w_out, norm2_g,
           ffn_w_up, ffn_conv_w, ffn_conv_b, ffn_w_down, final_norm_g):
    b, l, d = x.shape
    lctx = ctx.shape[1]
    lw = lru_conv_w.shape[2]
    sw = SSD_HEADS * SSD_HEAD_DIM
    cdim = ssd_conv_w.shape[2]
    rows = l // GRID_W
    assert ada_w.shape[0] == 1, "single layer"
    assert rows == SSD_CHUNK, "an SSD chunk is one column of the latent grid"
    assert lctx % SSD_CHUNK == 0 and l % TILE == 0

    s_in = jnp.zeros((SUBLANES, d), F32).at[:b].set(c).at[b].set(c_ctx)
    mod = _mod_call(s_in, ada_w[0], ada_b)
    mod_lat = [m.reshape(b, 1, d) for m in jnp.split(mod[:b], 6, axis=-1)]
    mod_ctx = [jnp.broadcast_to(m.reshape(1, 1, d), (b, 1, d)) for m in jnp.split(mod[b:b + 1], 6, axis=-1)]
    sh1, sc1, g1, sh2, sc2, g2 = mod_lat
    csh1, csc1 = mod_ctx[0], mod_ctx[1]

    wi = w_in[0].astype(BF16)
    o1, o2, o3, o4 = lw, 2 * lw, 2 * lw + sw, 2 * lw + sw + cdim
    assert o1 % lw == 0 and o2 % sw == 0 and o3 % cdim == 0
    w_lu, w_lg, w_z, w_xbc = (wi, lw, 0), (wi, lw, o1 // lw), (wi, sw, o2 // sw), (wi, cdim, o3 // cdim)
    w_dt = (_pad_lanes(wi[:, o4:]), LANES, 0)
    seg = TILE // SUBLANES
    interleaved = lambda wd: ("groups", (l // SUBLANES, SUBLANES, wd), (seg, SUBLANES, wd), lambda i: (i, 0, 0))
    column = lambda wd: ("groups", (GRID_W, rows, wd), (GRID_W, SUBLANES, wd), lambda i: (0, i, 0))
    lu_l, lg_l, z_l, xbc_l, dt_l = _inproj_call(x, sh1, sc1, norm1_g, [
        (w_lu,) + interleaved(lw), (w_lg,) + interleaved(lw),
        (w_z, "raster", (l, sw), (TILE, sw), lambda i: (i, 0)),
        (w_xbc,) + column(cdim), (w_dt,) + column(LANES)], TILE)
    cseg = lctx // SUBLANES
    nctx = lctx // SSD_CHUNK
    chunks = lambda wd: ("chunks", (nctx, SSD_CHUNK, wd), (nctx, SSD_CHUNK, wd), lambda i: (0, 0, 0))
    lu_c, xbc_c, dt_c = _inproj_call(ctx, csh1, csc1, norm1_g, [
        (w_lu, "groups", (cseg, SUBLANES, lw), (cseg, SUBLANES, lw), lambda i: (0, 0, 0)),
        (w_xbc,) + chunks(cdim), (w_dt,) + chunks(LANES)], lctx)

    hpg = 4
    lcw, lcb = lru_conv_w[0], lru_conv_b
    zeros_w = jnp.zeros((b, 1, lw), F32)
    lru_args = []
    for dr in range(2):
        lru_args.append((lcw, lcb, _block_diag_gates(lru_wa[0, dr], lru_wx[0, dr], hpg),
                         lru_ba[0, dr][None], lru_bx[0, dr][None], lru_lambda[0, dr][None]))
    _, hc_f = _lru_call(lu_c, zeros_w, *lru_args[0], seg=cseg, rev=False)
    _, hc_b = _lru_call(lu_c, zeros_w, *lru_args[1], seg=cseg, rev=True)
    h_b, _ = _lru_call(lu_l, hc_b, *lru_args[1], seg=seg, rev=True)
    lru_out, _ = _lru_call(lu_l, hc_f, *lru_args[0], seg=seg, rev=False, hb=h_b, lg=lg_l)

    ssd_params = (ssd_conv_w[0], ssd_conv_b,
                  _pad_lanes(ssd_dt_bias[0].reshape(1, -1)), _pad_lanes(ssd_dt_bias[0].reshape(1, -1)).T,
                  _pad_lanes(ssd_a_log[0].reshape(1, -1)), _pad_lanes(ssd_a_log[0].reshape(1, -1)).T)
    zero_state = jnp.zeros((b, SSD_STATE, sw), F32)
    (sc_f,) = _ssd_call(xbc_c, dt_c, zero_state, ssd_params, rev=False, mode="state")
    (sc_b,) = _ssd_call(xbc_c, dt_c, zero_state, ssd_params, rev=True, mode="state")
    dsk = jnp.repeat(ssd_d[0], SSD_HEAD_DIM)[None]
    y_b, xc_l, _ = _ssd_call(xbc_l, dt_l, sc_b, ssd_params, rev=True, mode="first", dsk=dsk)
    y_l, _ = _ssd_call(xc_l, dt_l, sc_f, ssd_params, rev=False, mode="second", yb=y_b)

    x1 = _outproj_call(x, lru_out, y_l, z_l, g1, ssd_norm_g, w_out[0].astype(BF16))
    return _ffn_call(x1, sh2, sc2, g2, norm2_g, final_norm_g[None], ffn_w_up[0].astype(BF16), ffn_conv_w[0],
                     ffn_conv_b, ffn_w_down[0].astype(BF16), 512, 1024)
```

```python
import functools

import jax
import jax.numpy as jnp
from jax import lax
from jax.experimental import pallas as pl
from jax.experimental.pallas import tpu as pltpu

F32 = jnp.float32
BF16 = jnp.bfloat16

EPS = 1e-6
GRID_W = 64
LRU_C = 8.0
SSD_HEADS = 16
SSD_HEAD_DIM = 64
SSD_GROUPS = 2
SSD_STATE = 128
SSD_CHUNK = 128
SUBLANES = 8
LANES = 128
VMEM_LIMIT = 56 * 1024 * 1024
TILE = SUBLANES * GRID_W


def _cparams(sem):
    return pltpu.CompilerParams(dimension_semantics=sem, vmem_limit_bytes=VMEM_LIMIT)


def _split_bf16(v, terms):
    parts = []
    rem = v
    for _ in range(terms):
        p = rem.astype(BF16)
        parts.append(p)
        rem = rem - p.astype(F32)
    return parts


def _dot(a, b):
    return jnp.dot(a, b, preferred_element_type=F32)


def _sigmoid(x):
    return 0.5 * jnp.tanh(0.5 * x) + 0.5


def _silu(x):
    h = 0.5 * x
    return h * jnp.tanh(h) + h


def _gather_groups(ref):
    return jnp.concatenate([ref[:, s, :] for s in range(SUBLANES)], axis=0)


def _scatter_groups(ref, val):
    seg = val.shape[0] // SUBLANES
    for s in range(SUBLANES):
        ref[:, s, :] = val[s * seg:(s + 1) * seg, :]


def _mod_kernel(s_ref, w_ref, b_ref, o_ref):
    s = _silu(s_ref[...])
    s_hi, s_lo = _split_bf16(s, 2)
    w_hi, w_lo = _split_bf16(w_ref[...], 2)
    acc = _dot(s_hi, w_hi) + _dot(s_hi, w_lo) + _dot(s_lo, w_hi)
    o_ref[...] = acc + b_ref[...]


def _mod_call(s, w, b):
    rows, d = s.shape
    n = w.shape[1]
    nb = 1536
    return pl.pallas_call(
        _mod_kernel,
        grid=(n // nb,),
        in_specs=[pl.BlockSpec((rows, d), lambda j: (0, 0)),
                  pl.BlockSpec((d, nb), lambda j: (0, j)),
                  pl.BlockSpec((1, nb), lambda j: (0, j))],
        out_specs=pl.BlockSpec((rows, nb), lambda j: (0, j)),
        out_shape=jax.ShapeDtypeStruct((rows, n), F32),
        compiler_params=_cparams(("parallel",)),
        name="mod",
    )(s, w, b)


def _norm_mod(x, g, shift, scale):
    ms = jnp.mean(x * x, axis=-1, keepdims=True)
    y = x * lax.rsqrt(ms + EPS) * g
    return y * (1.0 + scale) + shift


def _inproj_kernel(x_ref, sh_ref, sc_ref, g_ref, *refs, kinds):
    n = len(kinds)
    w_refs, o_refs = refs[:n], refs[n:2 * n]
    ngrp, seg, d = x_ref.shape
    g, sh, sc = g_ref[...], sh_ref[...], sc_ref[...]
    lhs = {}
    if any(k != "groups" for k in kinds):
        lhs["rows"] = _norm_mod(x_ref[...].reshape(ngrp * seg, d), g, sh, sc).astype(BF16)
    if "groups" in kinds:
        xg = jnp.concatenate([x_ref[:, m, :] for m in range(seg)], axis=0)
        lhs["groups"] = _norm_mod(xg, g, sh, sc).astype(BF16)
        if len(refs) > 2 * n:
            refs[2 * n][...] = xg.reshape(seg, ngrp, d)
    piece = 2 * LANES
    for w_ref, o_ref, kind in zip(w_refs, o_refs, kinds):
        cols = w_ref.shape[1]
        for c0 in range(0, cols, piece):
            cs = slice(c0, min(c0 + piece, cols))
            if kind == "groups":
                res = _dot(lhs["groups"], w_ref[:, cs])
                o_ref[:, :, cs] = res.reshape(seg, ngrp, res.shape[1])
                continue
            res = _dot(lhs["rows"], w_ref[:, cs])
            if kind == "raster":
                o_ref[:, cs] = res
            else:
                rows = o_ref.shape[1]
                for ci in range(o_ref.shape[0]):
                    o_ref[ci, :, cs] = res[ci * rows:(ci + 1) * rows, :]


def _inproj_call(x, shift, scale, g, outs, tm, emit_x=False):
    b, l, d = x.shape
    seg = tm // SUBLANES
    x = x.reshape(b, l // seg, seg, d)
    row = lambda: pl.BlockSpec((None, 1, d), lambda bi, i: (bi, 0, 0))
    in_specs = [pl.BlockSpec((None, SUBLANES, seg, d), lambda bi, i: (bi, i, 0, 0)), row(), row(),
                pl.BlockSpec((1, d), lambda bi, i: (0, 0))]
    in_specs += [pl.BlockSpec((d, o[0][1]), functools.partial(lambda bi, i, cb: (0, cb), cb=o[0][2])) for o in outs]
    out_specs, out_shape = [], []
    for _, _, shape, block, imap in outs:
        out_specs.append(pl.BlockSpec((None,) + block, functools.partial(lambda bi, i, f: (bi,) + f(i), f=imap)))
        out_shape.append(jax.ShapeDtypeStruct((b,) + shape, F32))
    if emit_x:
        out_specs.append(pl.BlockSpec((None, seg, SUBLANES, d), lambda bi, i: (bi, i, 0, 0)))
        out_shape.append(jax.ShapeDtypeStruct((b, l // SUBLANES, SUBLANES, d), F32))
    kern = functools.partial(_inproj_kernel, kinds=tuple(o[1] for o in outs))
    return pl.pallas_call(
        kern,
        grid=(b, l // tm),
        in_specs=in_specs,
        out_specs=out_specs,
        out_shape=out_shape,
        compiler_params=_cparams(("parallel", "parallel")),
        name="inproj",
    )(x, shift, scale, g, *[o[0][0] for o in outs])


LRU_RB = 64
SQRT_FLOOR = 1e-30
LOG2_E = 1.4426950408889634
HALO = 3 * SUBLANES


def _lru_kernel(*refs, seg, nt, rev, final):
    (lu_ref, prev_ref, next_ref, h0_ref, cw_ref, cb_ref, wg_ref, ba_ref, bx_ref, lam_ref) = refs[:10]
    pos = 10
    if final:
        hb_ref, lg_ref = refs[pos:pos + 2]
        pos += 2
    o_ref, hl_ref = refs[pos:pos + 2]
    lo_ref, hi_ref, a_ref, u_ref, st_ref, carry_ref = refs[pos + 2:]

    t = seg * SUBLANES
    i = pl.program_id(1)
    c = (nt - 1 - i) if rev else i

    @pl.when(i == 0)
    def _():
        carry_ref[...] = h0_ref[...]

    width = cw_ref.shape[1]
    row = lax.broadcasted_iota(jnp.int32, (SUBLANES, width), 0)

    spb = LRU_RB // SUBLANES
    nblk = seg // spb
    assert nblk >= 2

    def before(own, other):
        return jnp.where(row == 0, pltpu.roll(jnp.where(c > 0, other, 0.0), 1, 0), pltpu.roll(own, 1, 0))

    lo_ref[0:SUBLANES, :] = before(lu_ref[seg - 2], prev_ref[0])
    lo_ref[SUBLANES:2 * SUBLANES, :] = before(lu_ref[seg - 1], prev_ref[1])
    lo_ref[2 * SUBLANES:, :] = lu_ref[0:spb + 1].reshape((spb + 1) * SUBLANES, width)
    hi_ref[0:(spb + 2) * SUBLANES, :] = lu_ref[seg - spb - 2:seg].reshape((spb + 2) * SUBLANES, width)
    hi_ref[(spb + 2) * SUBLANES:, :] = jnp.where(
        row == SUBLANES - 1, pltpu.roll(jnp.where(c < nt - 1, next_ref[0], 0.0), SUBLANES - 1, 0),
        pltpu.roll(lu_ref[0], SUBLANES - 1, 0))

    cw = cw_ref[...]
    cb = cb_ref[...]
    half_ba = 0.5 * ba_ref[...]
    half_bx = 0.5 * bx_ref[...]
    log_decay = -LRU_C * jax.nn.softplus(-lam_ref[...])
    c2 = log_decay * (0.5 * LOG2_E)
    ngroups = wg_ref.shape[0]
    gw = width // ngroups

    def gate_block(r0, taps):
        xc = cb
        for k in range(4):
            xc = xc + cw[k:k + 1, :] * taps[k]
        for g in range(ngroups):
            sl = slice(g * gw, (g + 1) * gw)
            xg = xc[:, sl]
            pre = _dot(xg.astype(BF16), wg_ref[g])
            t_r = jnp.tanh(pre[:, :gw] + half_ba[:, sl])
            t_i = jnp.tanh(pre[:, gw:] + half_bx[:, sl])
            a = jnp.exp2(t_r * c2[:, sl] + c2[:, sl])
            y = 1.0 - a * a
            hx = 0.5 * xg
            u = (y * lax.rsqrt(jnp.maximum(y, SQRT_FLOOR))) * (hx * t_i + hx)
            a_ref[pl.ds(r0, LRU_RB), sl] = a
            u_ref[pl.ds(r0, LRU_RB), sl] = u

    def edge_taps(ref):
        return [ref[k * SUBLANES:k * SUBLANES + LRU_RB, :] for k in range(4)]

    def interior_block(rb, carry):
        taps = [lu_ref[pl.ds(rb * spb - 2 + k, spb)].reshape(LRU_RB, width) for k in range(4)]
        gate_block(pl.multiple_of(rb * LRU_RB, LRU_RB), taps)
        return carry

    gate_block(0, edge_taps(lo_ref))
    lax.fori_loop(1, nblk - 1, interior_block, 0)
    gate_block((nblk - 1) * LRU_RB, edge_taps(hi_ref))

    def slab(jj):
        j = (seg - 1 - jj) if rev else jj
        return j, pl.multiple_of(j * SUBLANES, SUBLANES)

    def seg_totals(jj, hp):
        h, p = hp
        _, r0 = slab(jj)
        a8 = a_ref[pl.ds(r0, SUBLANES), :]
        return a8 * h + u_ref[pl.ds(r0, SUBLANES), :], p * a8

    h_end, p_end = lax.fori_loop(0, seg, seg_totals,
                                 (jnp.zeros((SUBLANES, width), F32), jnp.ones((SUBLANES, width), F32)), unroll=4)

    cur = carry_ref[...]
    for r in (range(SUBLANES - 1, -1, -1) if rev else range(SUBLANES)):
        st_ref[r:r + 1, :] = cur
        cur = p_end[r:r + 1, :] * cur + h_end[r:r + 1, :]
    carry_ref[...] = cur
    hl_ref[...] = cur

    def emit(jj, h):
        j, r0 = slab(jj)
        h = a_ref[pl.ds(r0, SUBLANES), :] * h + u_ref[pl.ds(r0, SUBLANES), :]
        if final:
            o_ref[j] = (h + hb_ref[j]) * jax.nn.gelu(lg_ref[j])
        else:
            o_ref[j] = h
        return h

    lax.fori_loop(0, seg, emit, st_ref[...], unroll=4)


def _lru_call(lu, h0, cw, cb, wg, ba, bx, lam, *, seg, rev, hb=None, lg=None):
    b, n8, _, w = lu.shape
    nt = n8 // seg
    final = hb is not None
    cidx = (lambda i: nt - 1 - i) if rev else (lambda i: i)
    main = lambda: pl.BlockSpec((None, seg, SUBLANES, w), lambda bi, i: (bi, cidx(i), 0, 0))
    vec = lambda a: pl.BlockSpec(a.shape, lambda bi, i: (0,) * a.ndim)
    in_specs = [
        main(),
        pl.BlockSpec((None, 2, SUBLANES, w), lambda bi, i: (bi, jnp.maximum(cidx(i) * (seg // 2) - 1, 0), 0, 0)),
        pl.BlockSpec((None, 1, SUBLANES, w), lambda bi, i: (bi, jnp.minimum((cidx(i) + 1) * seg, n8 - 1), 0, 0)),
        pl.BlockSpec((None, 1, w), lambda bi, i: (bi, 0, 0)),
        vec(cw), vec(cb), vec(wg), vec(ba), vec(bx), vec(lam),
    ]
    args = [lu, lu, lu, h0, cw, cb, wg, ba, bx, lam]
    if final:
        in_specs += [main(), main()]
        args += [hb, lg]
    t = seg * SUBLANES
    kern = functools.partial(_lru_kernel, seg=seg, nt=nt, rev=rev, final=final)
    return pl.pallas_call(
        kern,
        grid=(b, nt),
        in_specs=in_specs,
        out_specs=[main(), pl.BlockSpec((None, 1, w), lambda bi, i: (bi, 0, 0))],
        out_shape=[jax.ShapeDtypeStruct(lu.shape, F32), jax.ShapeDtypeStruct((b, 1, w), F32)],
        scratch_shapes=[pltpu.VMEM((LRU_RB + HALO, w), F32), pltpu.VMEM((LRU_RB + HALO, w), F32),
                        pltpu.VMEM((t, w), F32), pltpu.VMEM((t, w), F32),
                        pltpu.VMEM((SUBLANES, w), F32), pltpu.VMEM((1, w), F32)],
        compiler_params=_cparams(("parallel", "arbitrary")),
        name="lru_rev" if rev else "lru_fwd",
    )(*args)


def _ssd_kernel(*refs, batched, nb, **kw):
    for bi in range(nb):
        _ssd_chunk(*[r.at[bi] if is_b else r for r, is_b in zip(refs, batched)], **kw)


def _ssd_chunk(*refs, nc, rev, mode):
    if mode == "second":
        xc_ref, dt_ref, h0_ref, bias_r_ref, bias_c_ref, alog_r_ref, alog_c_ref, yb_ref, o_ref, hl_ref, hst_ref = refs
    else:
        (xm_ref, xp_ref, xn_ref, dt_ref, h0_ref, cw_ref, cb_ref, bias_r_ref, bias_c_ref,
         alog_r_ref, alog_c_ref) = refs[:11]
        if mode == "first":
            dsk_ref, o_ref, xc_ref, hl_ref, xb_ref, hst_ref = refs[11:]
        else:
            hl_ref, xb_ref, hst_ref = refs[11:]

    t = SSD_CHUNK
    width = SSD_HEADS * SSD_HEAD_DIM
    gn = SSD_STATE
    i = pl.program_id(0)
    c = (nc - 1 - i) if rev else i
    d = 1 if rev else 0
    tl = 0 if rev else t - 1

    @pl.when(i == 0)
    def _():
        hst_ref[...] = h0_ref[...]

    slabs = []
    for s in range((width + 2 * SSD_GROUPS * gn) // LANES):
        ls = slice(s * LANES, (s + 1) * LANES)
        if mode == "second":
            slabs.append(xc_ref[:, ls])
            continue
        xb_ref[s, 0:SUBLANES, :] = jnp.where(c > 0, xp_ref[:, ls], 0.0)
        xb_ref[s, SUBLANES:SUBLANES + t, :] = xm_ref[:, ls]
        xb_ref[s, SUBLANES + t:, :] = jnp.where(c < nc - 1, xn_ref[:, ls], 0.0)
        xc = cb_ref[:, ls]
        for k in range(4):
            xc = xc + cw_ref[k:k + 1, ls] * xb_ref[s, SUBLANES - 2 + k:SUBLANES - 2 + k + t, :]
        slabs.append(_silu(xc))
        if mode == "first":
            xc_ref[:, ls] = slabs[-1].astype(BF16)
    nx = width // LANES
    x_slabs = slabs[:nx]
    bm = slabs[nx:nx + SSD_GROUPS]
    cm = slabs[nx + SSD_GROUPS:]

    ti = lax.broadcasted_iota(jnp.int32, (t, t), 0)
    si = lax.broadcasted_iota(jnp.int32, (t, t), 1)
    inc = (si >= ti) if rev else (si <= ti)
    inc_b = jnp.where(inc, 1.0, 0.0).astype(BF16)
    inc_t_b = jnp.where((ti >= si) if rev else (ti <= si), 1.0, 0.0).astype(BF16)

    dtraw = dt_ref[...]
    dt = jax.nn.softplus(dtraw + bias_r_ref[...])
    da = dt * (-jnp.exp(alog_r_ref[...]))
    cs = sum(_dot(inc_b, p) for p in _split_bf16(da, 3))
    q0 = SSD_HEADS * d
    dtraw_t = dtraw.T[q0:q0 + SSD_HEADS, :]
    dt_t = jax.nn.softplus(dtraw_t + bias_c_ref[q0:q0 + SSD_HEADS, :])
    da_t = dt_t * (-jnp.exp(alog_c_ref[q0:q0 + SSD_HEADS, :]))
    cs_t = sum(_dot(p, inc_t_b) for p in _split_bf16(da_t, 3))
    w1_t = dt_t * jnp.exp(cs_t[:, tl:tl + 1] - cs_t)
    ecs = jnp.exp(cs)
    cs2 = cs * LOG2_E
    lcs2_t = (cs_t - jnp.log(dt_t)) * LOG2_E

    lane = lax.broadcasted_iota(jnp.int32, (t, LANES), 1)
    lo = lane < SSD_HEAD_DIM
    hg = SSD_HEADS // SSD_GROUPS
    gw = hg * SSD_HEAD_DIM

    if mode != "state":
        scores = [lax.dot_general(cm[g].astype(BF16), bm[g].astype(BF16),
                                  (((1,), (1,)), ((), ())), preferred_element_type=F32)
                  for g in range(SSD_GROUPS)]
        z_off = [_dot(cm[g].astype(BF16), hst_ref[:, g * gw:(g + 1) * gw].astype(BF16))
                 for g in range(SSD_GROUPS)]
    bm_t = [bm[g].astype(F32).T for g in range(SSD_GROUPS)]
    keep_lo = jnp.where(lo, 1.0, 0.0).astype(BF16)
    keep_hi = jnp.where(lo, 0.0, 1.0).astype(BF16)

    ys = []
    for pr in range(SSD_HEADS // 2):
        g = (2 * pr) // hg
        xp = x_slabs[pr].astype(BF16)
        rhs = jnp.concatenate([xp * keep_lo, xp * keep_hi], axis=0)
        lhs_s = jnp.concatenate([bm_t[g] * w1_t[2 * pr + e:2 * pr + e + 1, :] for e in range(2)],
                                axis=1).astype(BF16)
        if mode != "state":
            ms = []
            for e in range(2):
                hh = 2 * pr + e
                q = q0 + hh
                lmat = jnp.where(inc, jnp.exp2(cs2[:, q:q + 1] - lcs2_t[hh:hh + 1, :]), 0.0)
                ms.append((scores[g] * lmat).astype(BF16))
            lhs = jnp.concatenate([jnp.concatenate(ms, axis=1), lhs_s], axis=0)
            res = _dot(lhs, rhs)
            y_diag, s_new = res[:t], res[t:]
            e_pair = jnp.where(lo, ecs[:, q0 + 2 * pr:q0 + 2 * pr + 1], ecs[:, q0 + 2 * pr + 1:q0 + 2 * pr + 2])
            col = (pr * LANES) % gw
            ys.append(y_diag + z_off[g][:, col:col + LANES] * e_pair)
        else:
            s_new = _dot(lhs_s, rhs)
        dec_pair = jnp.where(lo[0:1, :], ecs[tl:tl + 1, q0 + 2 * pr:q0 + 2 * pr + 1],
                             ecs[tl:tl + 1, q0 + 2 * pr + 1:q0 + 2 * pr + 2])
        hst_ref[:, pr * LANES:(pr + 1) * LANES] = hst_ref[:, pr * LANES:(pr + 1) * LANES] * dec_pair + s_new

    @pl.when(i == nc - 1)
    def _():
        hl_ref[...] = hst_ref[...]

    if mode == "state":
        return
    y = jnp.concatenate(ys, axis=1)
    if mode == "first":
        o_ref[...] = y + jnp.concatenate(x_slabs, axis=1) * dsk_ref[...]
    else:
        o_ref[...] = y + yb_ref[...]


def _ssd_call(xin, dt, h0, params, *, rev, mode, yb=None, dsk=None):
    cw, cb, bias_r, bias_c, alog_r, alog_c = params
    b, nc, t, cdim = xin.shape
    width = SSD_HEADS * SSD_HEAD_DIM
    cidx = (lambda i: nc - 1 - i) if rev else (lambda i: i)
    tb = t // SUBLANES
    blk = lambda wd: pl.BlockSpec((b, None, t, wd), lambda i: (0, cidx(i), 0, 0))
    prev = pl.BlockSpec((b, None, SUBLANES, cdim), lambda i: (0, jnp.maximum(cidx(i) - 1, 0), tb - 1, 0))
    nxt = pl.BlockSpec((b, None, SUBLANES, cdim), lambda i: (0, jnp.minimum(cidx(i) + 1, nc - 1), 0, 0))
    vec = lambda a: pl.BlockSpec(a.shape, lambda i: (0,) * a.ndim)
    state = lambda: pl.BlockSpec((b, SSD_STATE, width), lambda i: (0, 0, 0))
    small = [bias_r, bias_c, alog_r, alog_c]
    y_shape = jax.ShapeDtypeStruct((b, nc, t, width), F32)
    h_shape = jax.ShapeDtypeStruct((b, SSD_STATE, width), F32)
    scratch = [pltpu.VMEM((b, SSD_STATE, width), F32)]
    if mode == "second":
        in_specs = [blk(cdim), blk(LANES), state()] + [vec(a) for a in small] + [blk(width)]
        args = [xin, dt, h0] + small + [yb]
        batched = [True] * 3 + [False] * 4 + [True]
        out_specs, out_shape = [blk(width), state()], [y_shape, h_shape]
    else:
        in_specs = [blk(cdim), prev, nxt, blk(LANES), state(), vec(cw), vec(cb)] + [vec(a) for a in small]
        args = [xin, xin, xin, dt, h0, cw, cb] + small
        batched = [True] * 5 + [False] * 6
        out_specs, out_shape = [state()], [h_shape]
        scratch = [pltpu.VMEM((b, cdim // LANES, t + 2 * SUBLANES, LANES), F32)] + scratch
        if mode == "first":
            in_specs.append(vec(dsk))
            args.append(dsk)
            batched.append(False)
            out_specs = [blk(width), blk(cdim)] + out_specs
            out_shape = [y_shape, jax.ShapeDtypeStruct((b, nc, t, cdim), BF16)] + out_shape
    batched += [True] * (len(out_specs) + len(scratch))
    kern = functools.partial(_ssd_kernel, batched=tuple(batched), nb=b, nc=nc, rev=rev, mode=mode)
    return pl.pallas_call(
        kern,
        grid=(nc,),
        in_specs=in_specs,
        out_specs=out_specs,
        out_shape=out_shape,
        scratch_shapes=scratch,
        compiler_params=_cparams(("arbitrary",)),
        name=f"ssd_{mode}_{'rev' if rev else 'fwd'}",
    )(*args)


def _outproj_kernel(x_ref, lru_ref, y_ref, z_ref, g_ref, ng_ref, w1_ref, w2_ref, o_ref):
    rows = o_ref.shape[0] * o_ref.shape[1]
    flat = lambda ref: ref[...].reshape(rows, ref.shape[2])
    mix = _dot(flat(lru_ref).astype(BF16), w1_ref[...])
    gated = flat(y_ref) * _silu(flat(z_ref))
    ms = jnp.mean(gated * gated, axis=-1, keepdims=True)
    ssd = (gated * lax.rsqrt(ms + EPS) * ng_ref[...]).astype(BF16)
    mix = mix + _dot(ssd, w2_ref[...])
    o_ref[...] = (flat(x_ref) + g_ref[...] * mix).reshape(o_ref.shape)


def _outproj_call(x, lru, y, z, gate, norm_g, w_out):
    b, n8, _, d = x.shape
    seg = TILE // SUBLANES
    half = w_out.shape[0] // 2
    assert lru.shape[3] == half and y.shape[3] == half
    tok = lambda wd: pl.BlockSpec((None, seg, SUBLANES, wd), lambda bi, i: (bi, i, 0, 0))
    return pl.pallas_call(
        _outproj_kernel,
        grid=(b, n8 // seg),
        in_specs=[tok(d), tok(half),
                  pl.BlockSpec((None, GRID_W, SUBLANES, half), lambda bi, i: (bi, 0, i, 0)),
                  tok(z.shape[3]),
                  pl.BlockSpec((None, 1, d), lambda bi, i: (bi, 0, 0)),
                  pl.BlockSpec(norm_g.shape, lambda bi, i: (0, 0)),
                  pl.BlockSpec((half, d), lambda bi, i: (0, 0)),
                  pl.BlockSpec((half, d), lambda bi, i: (1, 0))],
        out_specs=tok(d),
        out_shape=jax.ShapeDtypeStruct(x.shape, F32),
        compiler_params=_cparams(("parallel", "parallel")),
        name="outproj",
    )(x, lru, y, z, gate, norm_g, w_out, w_out)


FFN_RB = 128


def _ffn_kernel(xm_ref, xp_ref, xn_ref, sh_ref, sc_ref, gt_ref, ng_ref, fg_ref,
                wup_ref, cwv_ref, cwg_ref, cbv_ref, cbg_ref, wdn_ref, o_ref,
                f_ref, uv_ref, ug_ref, act_ref, acc_ref, xr_ref, *, tm, nt, nj):
    i = pl.program_id(1)
    j = pl.program_id(2)
    nslab = uv_ref.shape[0]
    fb = nslab * LANES
    piece = 2 * LANES

    def up_piece(u_ref, half, q):
        col = pl.multiple_of((half * nj + j) * fb + q * piece, piece)
        res = _dot(f_ref[...], wup_ref[:, pl.ds(col, piece)])
        u_ref[2 * q] = res[:, :LANES]
        u_ref[2 * q + 1] = res[:, LANES:]

    def conv_slab(s):
        ls = slice(s * LANES, (s + 1) * LANES)
        for rb in range(tm // FFN_RB):
            r0 = SUBLANES - 1 + rb * FFN_RB
            val, gate = cbv_ref[:, ls].astype(BF16), cbg_ref[:, ls].astype(BF16)
            for k in range(3):
                val = val + cwv_ref[k:k + 1, ls].astype(BF16) * uv_ref[s, r0 + k:r0 + k + FFN_RB, :].astype(BF16)
                gate = gate + cwg_ref[k:k + 1, ls].astype(BF16) * ug_ref[s, r0 + k:r0 + k + FFN_RB, :].astype(BF16)
            act_ref[rb * FFN_RB:(rb + 1) * FFN_RB, ls] = jax.nn.gelu(gate) * val

    @pl.when(j == 0)
    def _():
        ng, sh, sc = ng_ref[...], sh_ref[...], sc_ref[...]
        xr_ref[...] = _gather_groups(xm_ref)
        fp = _norm_mod(xp_ref[:, SUBLANES - 1, :], ng, sh, sc)
        fn = _norm_mod(xn_ref[:, 0, :], ng, sh, sc)
        f_ref[0:SUBLANES, :] = jnp.where(i > 0, fp, 0.0).astype(BF16)
        f_ref[SUBLANES:SUBLANES + tm, :] = _norm_mod(xr_ref[...], ng, sh, sc).astype(BF16)
        f_ref[SUBLANES + tm:, :] = jnp.where(i < nt - 1, fn, 0.0).astype(BF16)
        acc_ref[...] = jnp.zeros_like(acc_ref)

    for q in range(nslab // 2):
        up_piece(uv_ref, 0, q)
        up_piece(ug_ref, 1, q)
        conv_slab(2 * q)
        conv_slab(2 * q + 1)
    acc_ref[...] += _dot(act_ref[...], wdn_ref[j])

    @pl.when(j == nj - 1)
    def _():
        x2 = xr_ref[...] + gt_ref[...] * acc_ref[...]
        ms = jnp.mean(x2 * x2, axis=-1, keepdims=True)
        o_ref[...] = x2 * lax.rsqrt(ms + EPS) * fg_ref[...]


def _ffn_call(x1, shift, scale, gate, norm_g, final_g, w_up, conv_w, conv_b, w_down, tm, fb):
    b, n8, _, d = x1.shape
    l = n8 * SUBLANES
    dff = w_down.shape[0]
    nt = l // tm
    nj = dff // fb
    seg = tm // SUBLANES
    row = lambda: pl.BlockSpec((None, 1, d), lambda bi, i, j: (bi, 0, 0))
    vec = lambda: pl.BlockSpec((1, d), lambda bi, i, j: (0, 0))
    w_down = w_down.reshape(nj, fb, d)
    resident = lambda a: pl.BlockSpec(a.shape, lambda bi, i, j: (0,) * a.ndim, pipeline_mode=pl.Buffered(1))
    halo = lambda f: pl.BlockSpec((None, SUBLANES, SUBLANES, d), lambda bi, i, j: (bi, f(i), 0, 0))
    in_specs = [
        pl.BlockSpec((None, seg, SUBLANES, d), lambda bi, i, j: (bi, i, 0, 0)),
        halo(lambda i: jnp.maximum(i * (seg // SUBLANES) - 1, 0)),
        halo(lambda i: jnp.minimum((i + 1) * (seg // SUBLANES), n8 // SUBLANES - 1)),
        row(), row(), row(), vec(), vec(),
        resident(w_up),
        pl.BlockSpec((3, fb), lambda bi, i, j: (0, j)),
        pl.BlockSpec((3, fb), lambda bi, i, j: (0, nj + j)),
        pl.BlockSpec((1, fb), lambda bi, i, j: (0, j)),
        pl.BlockSpec((1, fb), lambda bi, i, j: (0, nj + j)),
        resident(w_down),
    ]
    kern = functools.partial(_ffn_kernel, tm=tm, nt=nt, nj=nj)
    slab = lambda: pltpu.VMEM((fb // LANES, tm + 2 * SUBLANES, LANES), F32)
    return pl.pallas_call(
        kern,
        grid=(b, nt, nj),
        in_specs=in_specs,
        out_specs=pl.BlockSpec((None, tm, d), lambda bi, i, j: (bi, i, 0)),
        out_shape=jax.ShapeDtypeStruct((b, l, d), F32),
        scratch_shapes=[pltpu.VMEM((tm + 2 * SUBLANES, d), BF16), slab(), slab(),
                        pltpu.VMEM((tm, fb), BF16), pltpu.VMEM((tm, d), F32), pltpu.VMEM((tm, d), F32)],
        compiler_params=_cparams(("parallel", "parallel", "arbitrary")),
        name="ffn",
    )(x1, x1, x1, shift, scale, gate, norm_g, final_g, w_up, conv_w, conv_w, conv_b, conv_b, w_down)


def _block_diag_gates(wa, wx, heads_per_group):
    h, hd, _ = wa.shape
    ng = h // heads_per_group
    eye = jnp.eye(heads_per_group, dtype=wa.dtype)

    def bd(w):
        w = w.reshape(ng, heads_per_group, hd, hd)
        return jnp.einsum('gaij,ab->gaibj', w, eye).reshape(ng, heads_per_group * hd, heads_per_group * hd)

    return (0.5 * jnp.concatenate([bd(wa), bd(wx)], axis=-1)).astype(BF16)


def _pad_lanes(v, n=LANES):
    return jnp.pad(v, ((0, 0), (0, n - v.shape[1])))


def kernel(x, c, ctx, c_ctx, ada_w, ada_b, norm1_g, w_in, lru_conv_w, lru_conv_b, lru_wa, lru_ba, lru_wx, lru_bx,
           lru_lambda, ssd_conv_w, ssd_conv_b, ssd_a_log, ssd_dt_bias, ssd_d, ssd_norm_g, w_out, norm2_g,
           ffn_w_up, ffn_conv_w, ffn_conv_b, ffn_w_down, final_norm_g):
    b, l, d = x.shape
    lctx = ctx.shape[1]
    lw = lru_conv_w.shape[2]
    sw = SSD_HEADS * SSD_HEAD_DIM
    cdim = ssd_conv_w.shape[2]
    rows = l // GRID_W
    assert ada_w.shape[0] == 1, "single layer"
    assert rows == SSD_CHUNK, "an SSD chunk is one column of the latent grid"
    assert lctx % SSD_CHUNK == 0 and l % TILE == 0

    s_in = jnp.zeros((SUBLANES, d), F32).at[:b].set(c).at[b].set(c_ctx)
    mod = _mod_call(s_in, ada_w[0], ada_b)
    mod_lat = [m.reshape(b, 1, d) for m in jnp.split(mod[:b], 6, axis=-1)]
    mod_ctx = [jnp.broadcast_to(m.reshape(1, 1, d), (b, 1, d)) for m in jnp.split(mod[b:b + 1], 6, axis=-1)]
    sh1, sc1, g1, sh2, sc2, g2 = mod_lat
    csh1, csc1 = mod_ctx[0], mod_ctx[1]

    wi = w_in[0].astype(BF16)
    o1, o2, o3, o4 = lw, 2 * lw, 2 * lw + sw, 2 * lw + sw + cdim
    assert o1 % lw == 0 and o2 % sw == 0 and o3 % cdim == 0
    w_lu, w_lg, w_z, w_xbc = (wi, lw, 0), (wi, lw, o1 // lw), (wi, sw, o2 // sw), (wi, cdim, o3 // cdim)
    w_dt = (_pad_lanes(wi[:, o4:]), LANES, 0)
    seg = TILE // SUBLANES
    interleaved = lambda wd: ("groups", (l // SUBLANES, SUBLANES, wd), (seg, SUBLANES, wd), lambda i: (i, 0, 0))
    column = lambda wd: ("groups", (GRID_W, rows, wd), (GRID_W, SUBLANES, wd), lambda i: (0, i, 0))
    lu_l, lg_l, z_l, xbc_l, dt_l, x_g = _inproj_call(x, sh1, sc1, norm1_g, [
        (w_lu,) + interleaved(lw), (w_lg,) + interleaved(lw), (w_z,) + interleaved(sw),
        (w_xbc,) + column(cdim), (w_dt,) + column(LANES)], TILE, emit_x=True)
    cseg = lctx // SUBLANES
    nctx = lctx // SSD_CHUNK
    chunks = lambda wd: ("chunks", (nctx, SSD_CHUNK, wd), (nctx, SSD_CHUNK, wd), lambda i: (0, 0, 0))
    lu_c, xbc_c, dt_c = _inproj_call(ctx, csh1, csc1, norm1_g, [
        (w_lu, "groups", (cseg, SUBLANES, lw), (cseg, SUBLANES, lw), lambda i: (0, 0, 0)),
        (w_xbc,) + chunks(cdim), (w_dt,) + chunks(LANES)], lctx)

    hpg = 4
    lcw, lcb = lru_conv_w[0], lru_conv_b
    zeros_w = jnp.zeros((b, 1, lw), F32)
    lru_args = []
    for dr in range(2):
        lru_args.append((lcw, lcb, _block_diag_gates(lru_wa[0, dr], lru_wx[0, dr], hpg),
                         lru_ba[0, dr][None], lru_bx[0, dr][None], lru_lambda[0, dr][None]))
    _, hc_f = _lru_call(lu_c, zeros_w, *lru_args[0], seg=cseg, rev=False)
    _, hc_b = _lru_call(lu_c, zeros_w, *lru_args[1], seg=cseg, rev=True)
    h_b, _ = _lru_call(lu_l, hc_b, *lru_args[1], seg=seg, rev=True)
    lru_out, _ = _lru_call(lu_l, hc_f, *lru_args[0], seg=seg, rev=False, hb=h_b, lg=lg_l)

    ssd_params = (ssd_conv_w[0], ssd_conv_b,
                  _pad_lanes(ssd_dt_bias[0].reshape(1, -1)), _pad_lanes(ssd_dt_bias[0].reshape(1, -1)).T,
                  _pad_lanes(ssd_a_log[0].reshape(1, -1)), _pad_lanes(ssd_a_log[0].reshape(1, -1)).T)
    zero_state = jnp.zeros((b, SSD_STATE, sw), F32)
    (sc_f,) = _ssd_call(xbc_c, dt_c, zero_state, ssd_params, rev=False, mode="state")
    (sc_b,) = _ssd_call(xbc_c, dt_c, zero_state, ssd_params, rev=True, mode="state")
    dsk = jnp.repeat(ssd_d[0], SSD_HEAD_DIM)[None]
    y_b, xc_l, _ = _ssd_call(xbc_l, dt_l, sc_b, ssd_params, rev=True, mode="first", dsk=dsk)
    y_l, _ = _ssd_call(xc_l, dt_l, sc_f, ssd_params, rev=False, mode="second", yb=y_b)

    x1 = _outproj_call(x_g, lru_out, y_l, z_l, g1, ssd_norm_g, w_out[0].astype(BF16))
    return _ffn_call(x1, sh2, sc2, g2, norm2_g, final_norm_g[None], ffn_w_up[0].astype(BF16), ffn_conv_w[0],
                     ffn_conv_b, ffn_w_down[0].astype(BF16), TILE, 1024)
```

```python
import functools

import jax
import jax.numpy as jnp
from jax import lax
from jax.experimental import pallas as pl
from jax.experimental.pallas import tpu as pltpu

F32 = jnp.float32
BF16 = jnp.bfloat16

EPS = 1e-6
GRID_W = 64
LRU_C = 8.0
SSD_HEADS = 16
SSD_HEAD_DIM = 64
SSD_GROUPS = 2
SSD_STATE = 128
SSD_CHUNK = 128
SUBLANES = 8
LANES = 128
VMEM_LIMIT = 56 * 1024 * 1024
TILE = SUBLANES * GRID_W


def _cparams(sem):
    return pltpu.CompilerParams(dimension_semantics=sem, vmem_limit_bytes=VMEM_LIMIT)


def _split_bf16(v, terms):
    parts = []
    rem = v
    for _ in range(terms):
        p = rem.astype(BF16)
        parts.append(p)
        rem = rem - p.astype(F32)
    return parts


def _dot(a, b):
    return jnp.dot(a, b, preferred_element_type=F32)


def _sigmoid(x):
    return 0.5 * jnp.tanh(0.5 * x) + 0.5


def _silu(x):
    h = 0.5 * x
    return h * jnp.tanh(h) + h


def _gather_groups(ref):
    return jnp.concatenate([ref[:, s, :] for s in range(SUBLANES)], axis=0)


def _scatter_groups(ref, val):
    seg = val.shape[0] // SUBLANES
    for s in range(SUBLANES):
        ref[:, s, :] = val[s * seg:(s + 1) * seg, :]


def _mod_kernel(s_ref, w_ref, b_ref, o_ref):
    s = _silu(s_ref[...])
    s_hi, s_lo = _split_bf16(s, 2)
    w_hi, w_lo = _split_bf16(w_ref[...], 2)
    acc = _dot(s_hi, w_hi) + _dot(s_hi, w_lo) + _dot(s_lo, w_hi)
    o_ref[...] = acc + b_ref[...]


def _mod_call(s, w, b):
    rows, d = s.shape
    n = w.shape[1]
    nb = 1536
    return pl.pallas_call(
        _mod_kernel,
        grid=(n // nb,),
        in_specs=[pl.BlockSpec((rows, d), lambda j: (0, 0)),
                  pl.BlockSpec((d, nb), lambda j: (0, j)),
                  pl.BlockSpec((1, nb), lambda j: (0, j))],
        out_specs=pl.BlockSpec((rows, nb), lambda j: (0, j)),
        out_shape=jax.ShapeDtypeStruct((rows, n), F32),
        compiler_params=_cparams(("parallel",)),
        name="mod",
    )(s, w, b)


def _norm_mod(x, g, shift, scale):
    ms = jnp.mean(x * x, axis=-1, keepdims=True)
    y = x * lax.rsqrt(ms + EPS) * g
    return y * (1.0 + scale) + shift


def _inproj_kernel(x_ref, sh_ref, sc_ref, g_ref, *refs, kinds):
    n = len(kinds)
    w_refs, o_refs = refs[:n], refs[n:2 * n]
    ngrp, seg, d = x_ref.shape
    g, sh, sc = g_ref[...], sh_ref[...], sc_ref[...]
    lhs = {}
    if any(k != "groups" for k in kinds):
        lhs["rows"] = _norm_mod(x_ref[...].reshape(ngrp * seg, d), g, sh, sc).astype(BF16)
    if "groups" in kinds:
        xg = jnp.concatenate([x_ref[:, m, :] for m in range(seg)], axis=0)
        lhs["groups"] = _norm_mod(xg, g, sh, sc).astype(BF16)
        if len(refs) > 2 * n:
            refs[2 * n][...] = xg.reshape(seg, ngrp, d)
    piece = 2 * LANES
    for w_ref, o_ref, kind in zip(w_refs, o_refs, kinds):
        cols = w_ref.shape[1]
        for c0 in range(0, cols, piece):
            cs = slice(c0, min(c0 + piece, cols))
            if kind == "groups":
                res = _dot(lhs["groups"], w_ref[:, cs])
                o_ref[:, :, cs] = res.reshape(seg, ngrp, res.shape[1])
                continue
            res = _dot(lhs["rows"], w_ref[:, cs])
            if kind == "raster":
                o_ref[:, cs] = res
            else:
                rows = o_ref.shape[1]
                for ci in range(o_ref.shape[0]):
                    o_ref[ci, :, cs] = res[ci * rows:(ci + 1) * rows, :]


def _inproj_call(x, shift, scale, g, outs, tm, emit_x=False):
    b, l, d = x.shape
    seg = tm // SUBLANES
    x = x.reshape(b, l // seg, seg, d)
    row = lambda: pl.BlockSpec((None, 1, d), lambda bi, i: (bi, 0, 0))
    in_specs = [pl.BlockSpec((None, SUBLANES, seg, d), lambda bi, i: (bi, i, 0, 0)), row(), row(),
                pl.BlockSpec((1, d), lambda bi, i: (0, 0))]
    in_specs += [pl.BlockSpec((d, o[0][1]), functools.partial(lambda bi, i, cb: (0, cb), cb=o[0][2])) for o in outs]
    out_specs, out_shape = [], []
    for _, _, shape, block, imap in outs:
        out_specs.append(pl.BlockSpec((None,) + block, functools.partial(lambda bi, i, f: (bi,) + f(i), f=imap)))
        out_shape.append(jax.ShapeDtypeStruct((b,) + shape, F32))
    if emit_x:
        out_specs.append(pl.BlockSpec((None, seg, SUBLANES, d), lambda bi, i: (bi, i, 0, 0)))
        out_shape.append(jax.ShapeDtypeStruct((b, l // SUBLANES, SUBLANES, d), F32))
    kern = functools.partial(_inproj_kernel, kinds=tuple(o[1] for o in outs))
    return pl.pallas_call(
        kern,
        grid=(b, l // tm),
        in_specs=in_specs,
        out_specs=out_specs,
        out_shape=out_shape,
        compiler_params=_cparams(("parallel", "parallel")),
        name="inproj",
    )(x, shift, scale, g, *[o[0][0] for o in outs])


LRU_RB = 64
SQRT_FLOOR = 1e-30
LOG2_E = 1.4426950408889634
HALO = 3 * SUBLANES


def _lru_kernel(*refs, seg, nt, rev, final):
    (lu_ref, prev_ref, next_ref, h0_ref, cw_ref, cb_ref, wg_ref, ba_ref, bx_ref, lam_ref) = refs[:10]
    pos = 10
    if final:
        hb_ref, lg_ref = refs[pos:pos + 2]
        pos += 2
    o_ref, hl_ref = refs[pos:pos + 2]
    lo_ref, hi_ref, a_ref, u_ref, st_ref, carry_ref = refs[pos + 2:]

    t = seg * SUBLANES
    i = pl.program_id(1)
    c = (nt - 1 - i) if rev else i

    @pl.when(i == 0)
    def _():
        carry_ref[...] = h0_ref[...]

    width = cw_ref.shape[1]
    row = lax.broadcasted_iota(jnp.int32, (SUBLANES, width), 0)

    spb = LRU_RB // SUBLANES
    nblk = seg // spb
    assert nblk >= 2

    def before(own, other):
        return jnp.where(row == 0, pltpu.roll(jnp.where(c > 0, other, 0.0), 1, 0), pltpu.roll(own, 1, 0))

    lo_ref[0:SUBLANES, :] = before(lu_ref[seg - 2], prev_ref[0])
    lo_ref[SUBLANES:2 * SUBLANES, :] = before(lu_ref[seg - 1], prev_ref[1])
    lo_ref[2 * SUBLANES:, :] = lu_ref[0:spb + 1].reshape((spb + 1) * SUBLANES, width)
    hi_ref[0:(spb + 2) * SUBLANES, :] = lu_ref[seg - spb - 2:seg].reshape((spb + 2) * SUBLANES, width)
    hi_ref[(spb + 2) * SUBLANES:, :] = jnp.where(
        row == SUBLANES - 1, pltpu.roll(jnp.where(c < nt - 1, next_ref[0], 0.0), SUBLANES - 1, 0),
        pltpu.roll(lu_ref[0], SUBLANES - 1, 0))

    cw = cw_ref[...]
    cb = cb_ref[...]
    half_ba = 0.5 * ba_ref[...]
    half_bx = 0.5 * bx_ref[...]
    log_decay = -LRU_C * jax.nn.softplus(-lam_ref[...])
    c2 = log_decay * (0.5 * LOG2_E)
    ngroups = wg_ref.shape[0]
    gw = width // ngroups

    def gate_block(r0, taps):
        xc = cb
        for k in range(4):
            xc = xc + cw[k:k + 1, :] * taps[k]
        for g in range(ngroups):
            sl = slice(g * gw, (g + 1) * gw)
            xg = xc[:, sl]
            pre = _dot(xg.astype(BF16), wg_ref[g])
            t_r = jnp.tanh(pre[:, :gw] + half_ba[:, sl])
            t_i = jnp.tanh(pre[:, gw:] + half_bx[:, sl])
            a = jnp.exp2(t_r * c2[:, sl] + c2[:, sl])
            y = 1.0 - a * a
            hx = 0.5 * xg
            u = (y * lax.rsqrt(jnp.maximum(y, SQRT_FLOOR))) * (hx * t_i + hx)
            a_ref[pl.ds(r0, LRU_RB), sl] = a
            u_ref[pl.ds(r0, LRU_RB), sl] = u

    def edge_taps(ref):
        return [ref[k * SUBLANES:k * SUBLANES + LRU_RB, :] for k in range(4)]

    def interior_block(rb, carry):
        taps = [lu_ref[pl.ds(rb * spb - 2 + k, spb)].reshape(LRU_RB, width) for k in range(4)]
        gate_block(pl.multiple_of(rb * LRU_RB, LRU_RB), taps)
        return carry

    gate_block(0, edge_taps(lo_ref))
    lax.fori_loop(1, nblk - 1, interior_block, 0)
    gate_block((nblk - 1) * LRU_RB, edge_taps(hi_ref))

    def slab(jj):
        j = (seg - 1 - jj) if rev else jj
        return j, pl.multiple_of(j * SUBLANES, SUBLANES)

    def seg_totals(jj, hp):
        h, p = hp
        _, r0 = slab(jj)
        a8 = a_ref[pl.ds(r0, SUBLANES), :]
        return a8 * h + u_ref[pl.ds(r0, SUBLANES), :], p * a8

    h_end, p_end = lax.fori_loop(0, seg, seg_totals,
                                 (jnp.zeros((SUBLANES, width), F32), jnp.ones((SUBLANES, width), F32)), unroll=4)

    cur = carry_ref[...]
    for r in (range(SUBLANES - 1, -1, -1) if rev else range(SUBLANES)):
        st_ref[r:r + 1, :] = cur
        cur = p_end[r:r + 1, :] * cur + h_end[r:r + 1, :]
    carry_ref[...] = cur
    hl_ref[...] = cur

    def emit(jj, h):
        j, r0 = slab(jj)
        h = a_ref[pl.ds(r0, SUBLANES), :] * h + u_ref[pl.ds(r0, SUBLANES), :]
        if final:
            o_ref[j] = (h + hb_ref[j]) * jax.nn.gelu(lg_ref[j])
        else:
            o_ref[j] = h
        return h

    lax.fori_loop(0, seg, emit, st_ref[...], unroll=4)


def _lru_call(lu, h0, cw, cb, wg, ba, bx, lam, *, seg, rev, hb=None, lg=None):
    b, n8, _, w = lu.shape
    nt = n8 // seg
    final = hb is not None
    cidx = (lambda i: nt - 1 - i) if rev else (lambda i: i)
    main = lambda: pl.BlockSpec((None, seg, SUBLANES, w), lambda bi, i: (bi, cidx(i), 0, 0))
    vec = lambda a: pl.BlockSpec(a.shape, lambda bi, i: (0,) * a.ndim)
    in_specs = [
        main(),
        pl.BlockSpec((None, 2, SUBLANES, w), lambda bi, i: (bi, jnp.maximum(cidx(i) * (seg // 2) - 1, 0), 0, 0)),
        pl.BlockSpec((None, 1, SUBLANES, w), lambda bi, i: (bi, jnp.minimum((cidx(i) + 1) * seg, n8 - 1), 0, 0)),
        pl.BlockSpec((None, 1, w), lambda bi, i: (bi, 0, 0)),
        vec(cw), vec(cb), vec(wg), vec(ba), vec(bx), vec(lam),
    ]
    args = [lu, lu, lu, h0, cw, cb, wg, ba, bx, lam]
    if final:
        in_specs += [main(), main()]
        args += [hb, lg]
    t = seg * SUBLANES
    kern = functools.partial(_lru_kernel, seg=seg, nt=nt, rev=rev, final=final)
    return pl.pallas_call(
        kern,
        grid=(b, nt),
        in_specs=in_specs,
        out_specs=[main(), pl.BlockSpec((None, 1, w), lambda bi, i: (bi, 0, 0))],
        out_shape=[jax.ShapeDtypeStruct(lu.shape, F32), jax.ShapeDtypeStruct((b, 1, w), F32)],
        scratch_shapes=[pltpu.VMEM((LRU_RB + HALO, w), F32), pltpu.VMEM((LRU_RB + HALO, w), F32),
                        pltpu.VMEM((t, w), F32), pltpu.VMEM((t, w), F32),
                        pltpu.VMEM((SUBLANES, w), F32), pltpu.VMEM((1, w), F32)],
        compiler_params=_cparams(("parallel", "arbitrary")),
        name="lru_rev" if rev else "lru_fwd",
    )(*args)


def _ssd_kernel(*refs, batched, nb, **kw):
    for bi in range(nb):
        _ssd_chunk(*[r.at[bi] if is_b else r for r, is_b in zip(refs, batched)], **kw)


def _ssd_chunk(*refs, nc, rev, mode):
    if mode == "second":
        xc_ref, dt_ref, h0_ref, bias_r_ref, bias_c_ref, alog_r_ref, alog_c_ref, yb_ref, o_ref, hl_ref, hst_ref = refs
    else:
        (xm_ref, xp_ref, xn_ref, dt_ref, h0_ref, cw_ref, cb_ref, bias_r_ref, bias_c_ref,
         alog_r_ref, alog_c_ref) = refs[:11]
        if mode == "first":
            dsk_ref, o_ref, xc_ref, hl_ref, xb_ref, hst_ref = refs[11:]
        else:
            hl_ref, xb_ref, hst_ref = refs[11:]

    t = SSD_CHUNK
    width = SSD_HEADS * SSD_HEAD_DIM
    gn = SSD_STATE
    i = pl.program_id(0)
    c = (nc - 1 - i) if rev else i
    d = 1 if rev else 0
    tl = 0 if rev else t - 1

    @pl.when(i == 0)
    def _():
        hst_ref[...] = h0_ref[...]

    slabs = []
    for s in range((width + 2 * SSD_GROUPS * gn) // LANES):
        ls = slice(s * LANES, (s + 1) * LANES)
        if mode == "second":
            slabs.append(xc_ref[:, ls])
            continue
        xb_ref[s, 0:SUBLANES, :] = jnp.where(c > 0, xp_ref[:, ls], 0.0)
        xb_ref[s, SUBLANES:SUBLANES + t, :] = xm_ref[:, ls]
        xb_ref[s, SUBLANES + t:, :] = jnp.where(c < nc - 1, xn_ref[:, ls], 0.0)
        xc = cb_ref[:, ls]
        for k in range(4):
            xc = xc + cw_ref[k:k + 1, ls] * xb_ref[s, SUBLANES - 2 + k:SUBLANES - 2 + k + t, :]
        slabs.append(_silu(xc))
        if mode == "first":
            xc_ref[:, ls] = slabs[-1].astype(BF16)
    nx = width // LANES
    x_slabs = slabs[:nx]
    bm = slabs[nx:nx + SSD_GROUPS]
    cm = slabs[nx + SSD_GROUPS:]

    ti = lax.broadcasted_iota(jnp.int32, (t, t), 0)
    si = lax.broadcasted_iota(jnp.int32, (t, t), 1)
    inc = (si >= ti) if rev else (si <= ti)
    inc_b = jnp.where(inc, 1.0, 0.0).astype(BF16)
    inc_t_b = jnp.where((ti >= si) if rev else (ti <= si), 1.0, 0.0).astype(BF16)

    dtraw = dt_ref[...]
    dt = jax.nn.softplus(dtraw + bias_r_ref[...])
    da = dt * (-jnp.exp(alog_r_ref[...]))
    cs = sum(_dot(inc_b, p) for p in _split_bf16(da, 3))
    q0 = SSD_HEADS * d
    dtraw_t = dtraw.T[q0:q0 + SSD_HEADS, :]
    dt_t = jax.nn.softplus(dtraw_t + bias_c_ref[q0:q0 + SSD_HEADS, :])
    da_t = dt_t * (-jnp.exp(alog_c_ref[q0:q0 + SSD_HEADS, :]))
    cs_t = sum(_dot(p, inc_t_b) for p in _split_bf16(da_t, 3))
    w1_t = dt_t * jnp.exp(cs_t[:, tl:tl + 1] - cs_t)
    ecs = jnp.exp(cs)
    cs2 = cs * LOG2_E
    lcs2_t = (cs_t - jnp.log(dt_t)) * LOG2_E

    lane = lax.broadcasted_iota(jnp.int32, (t, LANES), 1)
    lo = lane < SSD_HEAD_DIM
    hg = SSD_HEADS // SSD_GROUPS
    gw = hg * SSD_HEAD_DIM

    if mode != "state":
        scores = [lax.dot_general(cm[g].astype(BF16), bm[g].astype(BF16),
                                  (((1,), (1,)), ((), ())), preferred_element_type=F32)
                  for g in range(SSD_GROUPS)]
        z_off = [_dot(cm[g].astype(BF16), hst_ref[:, g * gw:(g + 1) * gw].astype(BF16))
                 for g in range(SSD_GROUPS)]
    bm_t = [bm[g].astype(F32).T for g in range(SSD_GROUPS)]
    keep_lo = jnp.where(lo, 1.0, 0.0).astype(BF16)
    keep_hi = jnp.where(lo, 0.0, 1.0).astype(BF16)

    ys = []
    for pr in range(SSD_HEADS // 2):
        g = (2 * pr) // hg
        xp = x_slabs[pr].astype(BF16)
        rhs = jnp.concatenate([xp * keep_lo, xp * keep_hi], axis=0)
        lhs_s = jnp.concatenate([bm_t[g] * w1_t[2 * pr + e:2 * pr + e + 1, :] for e in range(2)],
                                axis=1).astype(BF16)
        if mode != "state":
            ms = []
            for e in range(2):
                hh = 2 * pr + e
                q = q0 + hh
                lmat = jnp.where(inc, jnp.exp2(cs2[:, q:q + 1] - lcs2_t[hh:hh + 1, :]), 0.0)
                ms.append((scores[g] * lmat).astype(BF16))
            lhs = jnp.concatenate([jnp.concatenate(ms, axis=1), lhs_s], axis=0)
            res = _dot(lhs, rhs)
            y_diag, s_new = res[:t], res[t:]
            e_pair = jnp.where(lo, ecs[:, q0 + 2 * pr:q0 + 2 * pr + 1], ecs[:, q0 + 2 * pr + 1:q0 + 2 * pr + 2])
            col = (pr * LANES) % gw
            ys.append(y_diag + z_off[g][:, col:col + LANES] * e_pair)
        else:
            s_new = _dot(lhs_s, rhs)
        dec_pair = jnp.where(lo[0:1, :], ecs[tl:tl + 1, q0 + 2 * pr:q0 + 2 * pr + 1],
                             ecs[tl:tl + 1, q0 + 2 * pr + 1:q0 + 2 * pr + 2])
        hst_ref[:, pr * LANES:(pr + 1) * LANES] = hst_ref[:, pr * LANES:(pr + 1) * LANES] * dec_pair + s_new

    @pl.when(i == nc - 1)
    def _():
        hl_ref[...] = hst_ref[...]

    if mode == "state":
        return
    y = jnp.concatenate(ys, axis=1)
    if mode == "first":
        o_ref[...] = y + jnp.concatenate(x_slabs, axis=1) * dsk_ref[...]
    else:
        o_ref[...] = y + yb_ref[...]


def _ssd_call(xin, dt, h0, params, *, rev, mode, yb=None, dsk=None):
    cw, cb, bias_r, bias_c, alog_r, alog_c = params
    b, nc, t, cdim = xin.shape
    width = SSD_HEADS * SSD_HEAD_DIM
    cidx = (lambda i: nc - 1 - i) if rev else (lambda i: i)
    tb = t // SUBLANES
    blk = lambda wd: pl.BlockSpec((b, None, t, wd), lambda i: (0, cidx(i), 0, 0))
    prev = pl.BlockSpec((b, None, SUBLANES, cdim), lambda i: (0, jnp.maximum(cidx(i) - 1, 0), tb - 1, 0))
    nxt = pl.BlockSpec((b, None, SUBLANES, cdim), lambda i: (0, jnp.minimum(cidx(i) + 1, nc - 1), 0, 0))
    vec = lambda a: pl.BlockSpec(a.shape, lambda i: (0,) * a.ndim)
    state = lambda: pl.BlockSpec((b, SSD_STATE, width), lambda i: (0, 0, 0))
    small = [bias_r, bias_c, alog_r, alog_c]
    y_shape = jax.ShapeDtypeStruct((b, nc, t, width), F32)
    h_shape = jax.ShapeDtypeStruct((b, SSD_STATE, width), F32)
    scratch = [pltpu.VMEM((b, SSD_STATE, width), F32)]
    if mode == "second":
        in_specs = [blk(cdim), blk(LANES), state()] + [vec(a) for a in small] + [blk(width)]
        args = [xin, dt, h0] + small + [yb]
        batched = [True] * 3 + [False] * 4 + [True]
        out_specs, out_shape = [blk(width), state()], [y_shape, h_shape]
    else:
        in_specs = [blk(cdim), prev, nxt, blk(LANES), state(), vec(cw), vec(cb)] + [vec(a) for a in small]
        args = [xin, xin, xin, dt, h0, cw, cb] + small
        batched = [True] * 5 + [False] * 6
        out_specs, out_shape = [state()], [h_shape]
        scratch = [pltpu.VMEM((b, cdim // LANES, t + 2 * SUBLANES, LANES), F32)] + scratch
        if mode == "first":
            in_specs.append(vec(dsk))
            args.append(dsk)
            batched.append(False)
            out_specs = [blk(width), blk(cdim)] + out_specs
            out_shape = [y_shape, jax.ShapeDtypeStruct((b, nc, t, cdim), BF16)] + out_shape
    batched += [True] * (len(out_specs) + len(scratch))
    kern = functools.partial(_ssd_kernel, batched=tuple(batched), nb=b, nc=nc, rev=rev, mode=mode)
    return pl.pallas_call(
        kern,
        grid=(nc,),
        in_specs=in_specs,
        out_specs=out_specs,
        out_shape=out_shape,
        scratch_shapes=scratch,
        compiler_params=_cparams(("arbitrary",)),
        name=f"ssd_{mode}_{'rev' if rev else 'fwd'}",
    )(*args)


def _outproj_kernel(x_ref, lru_ref, y_ref, z_ref, g_ref, ng_ref, w1_ref, w2_ref, o_ref, x1_ref):
    rows = x1_ref.shape[0] * x1_ref.shape[1]
    flat = lambda ref: ref[...].reshape(rows, ref.shape[2])
    mix = _dot(flat(lru_ref).astype(BF16), w1_ref[...])
    gated = flat(y_ref) * _silu(flat(z_ref))
    ms = jnp.mean(gated * gated, axis=-1, keepdims=True)
    ssd = (gated * lax.rsqrt(ms + EPS) * ng_ref[...]).astype(BF16)
    mix = mix + _dot(ssd, w2_ref[...])
    x1_ref[...] = (flat(x_ref) + g_ref[...] * mix).reshape(x1_ref.shape)
    o_ref[...] = _gather_groups(x1_ref)


def _outproj_call(x, lru, y, z, gate, norm_g, w_out):
    b, n8, _, d = x.shape
    seg = TILE // SUBLANES
    half = w_out.shape[0] // 2
    assert lru.shape[3] == half and y.shape[3] == half
    tok = lambda wd: pl.BlockSpec((None, seg, SUBLANES, wd), lambda bi, i: (bi, i, 0, 0))
    return pl.pallas_call(
        _outproj_kernel,
        grid=(b, n8 // seg),
        in_specs=[tok(d), tok(half),
                  pl.BlockSpec((None, GRID_W, SUBLANES, half), lambda bi, i: (bi, 0, i, 0)),
                  tok(z.shape[3]),
                  pl.BlockSpec((None, 1, d), lambda bi, i: (bi, 0, 0)),
                  pl.BlockSpec(norm_g.shape, lambda bi, i: (0, 0)),
                  pl.BlockSpec((half, d), lambda bi, i: (0, 0)),
                  pl.BlockSpec((half, d), lambda bi, i: (1, 0))],
        out_specs=pl.BlockSpec((None, TILE, d), lambda bi, i: (bi, i, 0)),
        out_shape=jax.ShapeDtypeStruct((b, n8 * SUBLANES, d), F32),
        scratch_shapes=[pltpu.VMEM((seg, SUBLANES, d), F32)],
        compiler_params=_cparams(("parallel", "parallel")),
        name="outproj",
    )(x, lru, y, z, gate, norm_g, w_out, w_out)


FFN_RB = 128


def _ffn_kernel(xm_ref, xp_ref, xn_ref, sh_ref, sc_ref, gt_ref, ng_ref, fg_ref,
                wup_ref, cwv_ref, cwg_ref, cbv_ref, cbg_ref, wdn_ref, o_ref,
                f_ref, uv_ref, ug_ref, act_ref, acc_ref, *, tm, nt, nj):
    i = pl.program_id(1)
    j = pl.program_id(2)
    nslab = uv_ref.shape[0]
    fb = nslab * LANES
    piece = 2 * LANES

    def up_piece(u_ref, half, q):
        col = pl.multiple_of((half * nj + j) * fb + q * piece, piece)
        res = _dot(f_ref[...], wup_ref[:, pl.ds(col, piece)])
        u_ref[2 * q] = res[:, :LANES]
        u_ref[2 * q + 1] = res[:, LANES:]

    def conv_slab(s):
        ls = slice(s * LANES, (s + 1) * LANES)
        for rb in range(tm // FFN_RB):
            r0 = SUBLANES - 1 + rb * FFN_RB
            val, gate = cbv_ref[:, ls].astype(BF16), cbg_ref[:, ls].astype(BF16)
            for k in range(3):
                val = val + cwv_ref[k:k + 1, ls].astype(BF16) * uv_ref[s, r0 + k:r0 + k + FFN_RB, :].astype(BF16)
                gate = gate + cwg_ref[k:k + 1, ls].astype(BF16) * ug_ref[s, r0 + k:r0 + k + FFN_RB, :].astype(BF16)
            act_ref[rb * FFN_RB:(rb + 1) * FFN_RB, ls] = jax.nn.gelu(gate) * val

    @pl.when(j == 0)
    def _():
        ng, sh, sc = ng_ref[...], sh_ref[...], sc_ref[...]
        fp = _norm_mod(xp_ref[...], ng, sh, sc)
        fn = _norm_mod(xn_ref[...], ng, sh, sc)
        f_ref[0:SUBLANES, :] = jnp.where(i > 0, fp, 0.0).astype(BF16)
        f_ref[SUBLANES:SUBLANES + tm, :] = _norm_mod(xm_ref[...], ng, sh, sc).astype(BF16)
        f_ref[SUBLANES + tm:, :] = jnp.where(i < nt - 1, fn, 0.0).astype(BF16)
        acc_ref[...] = jnp.zeros_like(acc_ref)

    for q in range(nslab // 2):
        up_piece(uv_ref, 0, q)
        up_piece(ug_ref, 1, q)
        conv_slab(2 * q)
        conv_slab(2 * q + 1)
    acc_ref[...] += _dot(act_ref[...], wdn_ref[j])

    @pl.when(j == nj - 1)
    def _():
        x2 = xm_ref[...] + gt_ref[...] * acc_ref[...]
        ms = jnp.mean(x2 * x2, axis=-1, keepdims=True)
        o_ref[...] = x2 * lax.rsqrt(ms + EPS) * fg_ref[...]


def _ffn_call(x1, shift, scale, gate, norm_g, final_g, w_up, conv_w, conv_b, w_down, tm, fb):
    b, l, d = x1.shape
    dff = w_down.shape[0]
    nt = l // tm
    nj = dff // fb
    tb = tm // SUBLANES
    nb8 = l // SUBLANES
    row = lambda: pl.BlockSpec((None, 1, d), lambda bi, i, j: (bi, 0, 0))
    vec = lambda: pl.BlockSpec((1, d), lambda bi, i, j: (0, 0))
    w_down = w_down.reshape(nj, fb, d)
    resident = lambda a: pl.BlockSpec(a.shape, lambda bi, i, j: (0,) * a.ndim, pipeline_mode=pl.Buffered(1))
    in_specs = [
        pl.BlockSpec((None, tm, d), lambda bi, i, j: (bi, i, 0)),
        pl.BlockSpec((None, SUBLANES, d), lambda bi, i, j: (bi, jnp.maximum(i * tb - 1, 0), 0)),
        pl.BlockSpec((None, SUBLANES, d), lambda bi, i, j: (bi, jnp.minimum((i + 1) * tb, nb8 - 1), 0)),
        row(), row(), row(), vec(), vec(),
        resident(w_up),
        pl.BlockSpec((3, fb), lambda bi, i, j: (0, j)),
        pl.BlockSpec((3, fb), lambda bi, i, j: (0, nj + j)),
        pl.BlockSpec((1, fb), lambda bi, i, j: (0, j)),
        pl.BlockSpec((1, fb), lambda bi, i, j: (0, nj + j)),
        resident(w_down),
    ]
    kern = functools.partial(_ffn_kernel, tm=tm, nt=nt, nj=nj)
    slab = lambda: pltpu.VMEM((fb // LANES, tm + 2 * SUBLANES, LANES), F32)
    return pl.pallas_call(
        kern,
        grid=(b, nt, nj),
        in_specs=in_specs,
        out_specs=pl.BlockSpec((None, tm, d), lambda bi, i, j: (bi, i, 0)),
        out_shape=jax.ShapeDtypeStruct((b, l, d), F32),
        scratch_shapes=[pltpu.VMEM((tm + 2 * SUBLANES, d), BF16), slab(), slab(),
                        pltpu.VMEM((tm, fb), BF16), pltpu.VMEM((tm, d), F32)],
        compiler_params=_cparams(("parallel", "parallel", "arbitrary")),
        name="ffn",
    )(x1, x1, x1, shift, scale, gate, norm_g, final_g, w_up, conv_w, conv_w, conv_b, conv_b, w_down)


def _block_diag_gates(wa, wx, heads_per_group):
    h, hd, _ = wa.shape
    ng = h // heads_per_group
    eye = jnp.eye(heads_per_group, dtype=wa.dtype)

    def bd(w):
        w = w.reshape(ng, heads_per_group, hd, hd)
        return jnp.einsum('gaij,ab->gaibj', w, eye).reshape(ng, heads_per_group * hd, heads_per_group * hd)

    return (0.5 * jnp.concatenate([bd(wa), bd(wx)], axis=-1)).astype(BF16)


def _pad_lanes(v, n=LANES):
    return jnp.pad(v, ((0, 0), (0, n - v.shape[1])))


def kernel(x, c, ctx, c_ctx, ada_w, ada_b, norm1_g, w_in, lru_conv_w, lru_conv_b, lru_wa, lru_ba, lru_wx, lru_bx,
           lru_lambda, ssd_conv_w, ssd_conv_b, ssd_a_log, ssd_dt_bias, ssd_d, ssd_norm_g, w_out, norm2_g,
           ffn_w_up, ffn_conv_w, ffn_conv_b, ffn_w_down, final_norm_g):
    b, l, d = x.shape
    lctx = ctx.shape[1]
    lw = lru_conv_w.shape[2]
    sw = SSD_HEADS * SSD_HEAD_DIM
    cdim = ssd_conv_w.shape[2]
    rows = l // GRID_W
    assert ada_w.shape[0] == 1, "single layer"
    assert rows == SSD_CHUNK, "an SSD chunk is one column of the latent grid"
    assert lctx % SSD_CHUNK == 0 and l % TILE == 0

    s_in = jnp.zeros((SUBLANES, d), F32).at[:b].set(c).at[b].set(c_ctx)
    mod = _mod_call(s_in, ada_w[0], ada_b)
    mod_lat = [m.reshape(b, 1, d) for m in jnp.split(mod[:b], 6, axis=-1)]
    mod_ctx = [jnp.broadcast_to(m.reshape(1, 1, d), (b, 1, d)) for m in jnp.split(mod[b:b + 1], 6, axis=-1)]
    sh1, sc1, g1, sh2, sc2, g2 = mod_lat
    csh1, csc1 = mod_ctx[0], mod_ctx[1]

    wi = w_in[0].astype(BF16)
    o1, o2, o3, o4 = lw, 2 * lw, 2 * lw + sw, 2 * lw + sw + cdim
    assert o1 % lw == 0 and o2 % sw == 0 and o3 % cdim == 0
    w_lu, w_lg, w_z, w_xbc = (wi, lw, 0), (wi, lw, o1 // lw), (wi, sw, o2 // sw), (wi, cdim, o3 // cdim)
    w_dt = (_pad_lanes(wi[:, o4:]), LANES, 0)
    seg = TILE // SUBLANES
    interleaved = lambda wd: ("groups", (l // SUBLANES, SUBLANES, wd), (seg, SUBLANES, wd), lambda i: (i, 0, 0))
    column = lambda wd: ("groups", (GRID_W, rows, wd), (GRID_W, SUBLANES, wd), lambda i: (0, i, 0))
    lu_l, lg_l, z_l, xbc_l, dt_l, x_g = _inproj_call(x, sh1, sc1, norm1_g, [
        (w_lu,) + interleaved(lw), (w_lg,) + interleaved(lw), (w_z,) + interleaved(sw),
        (w_xbc,) + column(cdim), (w_dt,) + column(LANES)], TILE, emit_x=True)
    cseg = lctx // SUBLANES
    nctx = lctx // SSD_CHUNK
    chunks = lambda wd: ("chunks", (nctx, SSD_CHUNK, wd), (nctx, SSD_CHUNK, wd), lambda i: (0, 0, 0))
    lu_c, xbc_c, dt_c = _inproj_call(ctx, csh1, csc1, norm1_g, [
        (w_lu, "groups", (cseg, SUBLANES, lw), (cseg, SUBLANES, lw), lambda i: (0, 0, 0)),
        (w_xbc,) + chunks(cdim), (w_dt,) + chunks(LANES)], lctx)

    hpg = 4
    lcw, lcb = lru_conv_w[0], lru_conv_b
    zeros_w = jnp.zeros((b, 1, lw), F32)
    lru_args = []
    for dr in range(2):
        lru_args.append((lcw, lcb, _block_diag_gates(lru_wa[0, dr], lru_wx[0, dr], hpg),
                         lru_ba[0, dr][None], lru_bx[0, dr][None], lru_lambda[0, dr][None]))
    _, hc_f = _lru_call(lu_c, zeros_w, *lru_args[0], seg=cseg, rev=False)
    _, hc_b = _lru_call(lu_c, zeros_w, *lru_args[1], seg=cseg, rev=True)
    h_b, _ = _lru_call(lu_l, hc_b, *lru_args[1], seg=seg, rev=True)
    lru_out, _ = _lru_call(lu_l, hc_f, *lru_args[0], seg=seg, rev=False, hb=h_b, lg=lg_l)

    ssd_params = (ssd_conv_w[0], ssd_conv_b,
                  _pad_lanes(ssd_dt_bias[0].reshape(1, -1)), _pad_lanes(ssd_dt_bias[0].reshape(1, -1)).T,
                  _pad_lanes(ssd_a_log[0].reshape(1, -1)), _pad_lanes(ssd_a_log[0].reshape(1, -1)).T)
    zero_state = jnp.zeros((b, SSD_STATE, sw), F32)
    (sc_f,) = _ssd_call(xbc_c, dt_c, zero_state, ssd_params, rev=False, mode="state")
    (sc_b,) = _ssd_call(xbc_c, dt_c, zero_state, ssd_params, rev=True, mode="state")
    dsk = jnp.repeat(ssd_d[0], SSD_HEAD_DIM)[None]
    y_b, xc_l, _ = _ssd_call(xbc_l, dt_l, sc_b, ssd_params, rev=True, mode="first", dsk=dsk)
    y_l, _ = _ssd_call(xc_l, dt_l, sc_f, ssd_params, rev=False, mode="second", yb=y_b)

    x1 = _outproj_call(x_g, lru_out, y_l, z_l, g1, ssd_norm_g, w_out[0].astype(BF16))
    return _ffn_call(x1, sh2, sc2, g2, norm2_g, final_norm_g[None], ffn_w_up[0].astype(BF16), ffn_conv_w[0],
                     ffn_conv_b, ffn_w_down[0].astype(BF16), TILE, 1024)
```

```python
import functools

import jax
import jax.numpy as jnp
from jax import lax
from jax.experimental import pallas as pl
from jax.experimental.pallas import tpu as pltpu

F32 = jnp.float32
BF16 = jnp.bfloat16

EPS = 1e-6
GRID_W = 64
LRU_C = 8.0
SSD_HEADS = 16
SSD_HEAD_DIM = 64
SSD_GROUPS = 2
SSD_STATE = 128
SSD_CHUNK = 128
SUBLANES = 8
LANES = 128
VMEM_LIMIT = 56 * 1024 * 1024
TILE = SUBLANES * GRID_W


def _cparams(sem):
    return pltpu.CompilerParams(dimension_semantics=sem, vmem_limit_bytes=VMEM_LIMIT)


def _split_bf16(v, terms):
    parts = []
    rem = v
    for _ in range(terms):
        p = rem.astype(BF16)
        parts.append(p)
        rem = rem - p.astype(F32)
    return parts


def _dot(a, b):
    return jnp.dot(a, b, preferred_element_type=F32)


def _sigmoid(x):
    return 0.5 * jnp.tanh(0.5 * x) + 0.5


def _silu(x):
    h = 0.5 * x
    return h * jnp.tanh(h) + h


def _gather_groups(ref):
    return jnp.concatenate([ref[:, s, :] for s in range(SUBLANES)], axis=0)


def _scatter_groups(ref, val):
    seg = val.shape[0] // SUBLANES
    for s in range(SUBLANES):
        ref[:, s, :] = val[s * seg:(s + 1) * seg, :]


def _mod_kernel(s_ref, w_ref, b_ref, o_ref):
    s = _silu(s_ref[...])
    s_hi, s_lo = _split_bf16(s, 2)
    w_hi, w_lo = _split_bf16(w_ref[...], 2)
    acc = _dot(s_hi, w_hi) + _dot(s_hi, w_lo) + _dot(s_lo, w_hi)
    o_ref[...] = acc + b_ref[...]


def _mod_call(s, w, b):
    rows, d = s.shape
    n = w.shape[1]
    nb = 1536
    return pl.pallas_call(
        _mod_kernel,
        grid=(n // nb,),
        in_specs=[pl.BlockSpec((rows, d), lambda j: (0, 0)),
                  pl.BlockSpec((d, nb), lambda j: (0, j)),
                  pl.BlockSpec((1, nb), lambda j: (0, j))],
        out_specs=pl.BlockSpec((rows, nb), lambda j: (0, j)),
        out_shape=jax.ShapeDtypeStruct((rows, n), F32),
        compiler_params=_cparams(("parallel",)),
        name="mod",
    )(s, w, b)


def _norm_mod(x, g, shift, scale):
    ms = jnp.mean(x * x, axis=-1, keepdims=True)
    return (x * lax.rsqrt(ms + EPS)) * (g * (1.0 + scale)) + shift


def _inproj_kernel(x_ref, sh_ref, sc_ref, g_ref, *refs, kinds):
    n = len(kinds)
    w_refs, o_refs = refs[:n], refs[n:2 * n]
    ngrp, seg, d = x_ref.shape
    g, sh, sc = g_ref[...], sh_ref[...], sc_ref[...]
    lhs = {}
    if any(k != "groups" for k in kinds):
        lhs["rows"] = _norm_mod(x_ref[...].reshape(ngrp * seg, d), g, sh, sc).astype(BF16)
    if "groups" in kinds:
        xg = jnp.concatenate([x_ref[:, m, :] for m in range(seg)], axis=0)
        lhs["groups"] = _norm_mod(xg, g, sh, sc).astype(BF16)
        if len(refs) > 2 * n:
            refs[2 * n][...] = xg.reshape(seg, ngrp, d)
    piece = 2 * LANES
    for w_ref, o_ref, kind in zip(w_refs, o_refs, kinds):
        cols = w_ref.shape[1]
        for c0 in range(0, cols, piece):
            cs = slice(c0, min(c0 + piece, cols))
            if kind == "groups":
                res = _dot(lhs["groups"], w_ref[:, cs])
                o_ref[:, :, cs] = res.reshape(seg, ngrp, res.shape[1])
                continue
            res = _dot(lhs["rows"], w_ref[:, cs])
            if kind == "raster":
                o_ref[:, cs] = res
            else:
                rows = o_ref.shape[1]
                for ci in range(o_ref.shape[0]):
                    o_ref[ci, :, cs] = res[ci * rows:(ci + 1) * rows, :]


def _inproj_call(x, shift, scale, g, outs, tm, emit_x=False):
    b, l, d = x.shape
    seg = tm // SUBLANES
    x = x.reshape(b, l // seg, seg, d)
    row = lambda: pl.BlockSpec((None, 1, d), lambda bi, i: (bi, 0, 0))
    in_specs = [pl.BlockSpec((None, SUBLANES, seg, d), lambda bi, i: (bi, i, 0, 0)), row(), row(),
                pl.BlockSpec((1, d), lambda bi, i: (0, 0))]
    in_specs += [pl.BlockSpec((d, o[0][1]), functools.partial(lambda bi, i, cb: (0, cb), cb=o[0][2])) for o in outs]
    out_specs, out_shape = [], []
    for _, _, shape, block, imap in outs:
        out_specs.append(pl.BlockSpec((None,) + block, functools.partial(lambda bi, i, f: (bi,) + f(i), f=imap)))
        out_shape.append(jax.ShapeDtypeStruct((b,) + shape, F32))
    if emit_x:
        out_specs.append(pl.BlockSpec((None, seg, SUBLANES, d), lambda bi, i: (bi, i, 0, 0)))
        out_shape.append(jax.ShapeDtypeStruct((b, l // SUBLANES, SUBLANES, d), F32))
    kern = functools.partial(_inproj_kernel, kinds=tuple(o[1] for o in outs))
    return pl.pallas_call(
        kern,
        grid=(b, l // tm),
        in_specs=in_specs,
        out_specs=out_specs,
        out_shape=out_shape,
        compiler_params=_cparams(("parallel", "parallel")),
        name="inproj",
    )(x, shift, scale, g, *[o[0][0] for o in outs])


LRU_RB = 64
SQRT_FLOOR = 1e-30
LOG2_E = 1.4426950408889634
HALO = 3 * SUBLANES


def _lru_kernel(*refs, seg, nt, rev, final):
    (lu_ref, prev_ref, next_ref, h0_ref, cw_ref, cb_ref, wg_ref, ba_ref, bx_ref, lam_ref) = refs[:10]
    pos = 10
    if final:
        hb_ref, lg_ref = refs[pos:pos + 2]
        pos += 2
    o_ref, hl_ref = refs[pos:pos + 2]
    lo_ref, hi_ref, a_ref, u_ref, st_ref, carry_ref = refs[pos + 2:]

    t = seg * SUBLANES
    i = pl.program_id(1)
    c = (nt - 1 - i) if rev else i

    @pl.when(i == 0)
    def _():
        carry_ref[...] = h0_ref[...]

    width = cw_ref.shape[1]
    row = lax.broadcasted_iota(jnp.int32, (SUBLANES, width), 0)

    spb = LRU_RB // SUBLANES
    nblk = seg // spb
    assert nblk >= 2

    def before(own, other):
        return jnp.where(row == 0, pltpu.roll(jnp.where(c > 0, other, 0.0), 1, 0), pltpu.roll(own, 1, 0))

    lo_ref[0:SUBLANES, :] = before(lu_ref[seg - 2], prev_ref[0])
    lo_ref[SUBLANES:2 * SUBLANES, :] = before(lu_ref[seg - 1], prev_ref[1])
    lo_ref[2 * SUBLANES:, :] = lu_ref[0:spb + 1].reshape((spb + 1) * SUBLANES, width)
    hi_ref[0:(spb + 2) * SUBLANES, :] = lu_ref[seg - spb - 2:seg].reshape((spb + 2) * SUBLANES, width)
    hi_ref[(spb + 2) * SUBLANES:, :] = jnp.where(
        row == SUBLANES - 1, pltpu.roll(jnp.where(c < nt - 1, next_ref[0], 0.0), SUBLANES - 1, 0),
        pltpu.roll(lu_ref[0], SUBLANES - 1, 0))

    cw = cw_ref[...]
    cb = cb_ref[...]
    half_ba = 0.5 * ba_ref[...]
    half_bx = 0.5 * bx_ref[...]
    log_decay = -LRU_C * jax.nn.softplus(-lam_ref[...])
    c2 = log_decay * (0.5 * LOG2_E)
    ngroups = wg_ref.shape[0]
    gw = width // ngroups

    def gate_block(r0, taps):
        xc = cb
        for k in range(4):
            xc = xc + cw[k:k + 1, :] * taps[k]
        for g in range(ngroups):
            sl = slice(g * gw, (g + 1) * gw)
            xg = xc[:, sl]
            pre = _dot(xg.astype(BF16), wg_ref[g])
            t_r = jnp.tanh(pre[:, :gw] + half_ba[:, sl])
            t_i = jnp.tanh(pre[:, gw:] + half_bx[:, sl])
            a = jnp.exp2(t_r * c2[:, sl] + c2[:, sl])
            y = 1.0 - a * a
            hx = 0.5 * xg
            u = (y * lax.rsqrt(jnp.maximum(y, SQRT_FLOOR))) * (hx * t_i + hx)
            a_ref[pl.ds(r0, LRU_RB), sl] = a
            u_ref[pl.ds(r0, LRU_RB), sl] = u

    def edge_taps(ref):
        return [ref[k * SUBLANES:k * SUBLANES + LRU_RB, :] for k in range(4)]

    def interior_block(rb, carry):
        taps = [lu_ref[pl.ds(rb * spb - 2 + k, spb)].reshape(LRU_RB, width) for k in range(4)]
        gate_block(pl.multiple_of(rb * LRU_RB, LRU_RB), taps)
        return carry

    gate_block(0, edge_taps(lo_ref))
    lax.fori_loop(1, nblk - 1, interior_block, 0)
    gate_block((nblk - 1) * LRU_RB, edge_taps(hi_ref))

    def slab(jj):
        j = (seg - 1 - jj) if rev else jj
        return j, pl.multiple_of(j * SUBLANES, SUBLANES)

    def seg_totals(jj, hp):
        h, p = hp
        _, r0 = slab(jj)
        a8 = a_ref[pl.ds(r0, SUBLANES), :]
        return a8 * h + u_ref[pl.ds(r0, SUBLANES), :], p * a8

    h_end, p_end = lax.fori_loop(0, seg, seg_totals,
                                 (jnp.zeros((SUBLANES, width), F32), jnp.ones((SUBLANES, width), F32)), unroll=4)

    cur = carry_ref[...]
    for r in (range(SUBLANES - 1, -1, -1) if rev else range(SUBLANES)):
        st_ref[r:r + 1, :] = cur
        cur = p_end[r:r + 1, :] * cur + h_end[r:r + 1, :]
    carry_ref[...] = cur
    hl_ref[...] = cur

    def emit(jj, h):
        j, r0 = slab(jj)
        h = a_ref[pl.ds(r0, SUBLANES), :] * h + u_ref[pl.ds(r0, SUBLANES), :]
        if final:
            o_ref[j] = (h + hb_ref[j]) * jax.nn.gelu(lg_ref[j])
        else:
            o_ref[j] = h
        return h

    lax.fori_loop(0, seg, emit, st_ref[...], unroll=4)


def _lru_call(lu, h0, cw, cb, wg, ba, bx, lam, *, seg, rev, hb=None, lg=None):
    b, n8, _, w = lu.shape
    nt = n8 // seg
    final = hb is not None
    cidx = (lambda i: nt - 1 - i) if rev else (lambda i: i)
    main = lambda: pl.BlockSpec((None, seg, SUBLANES, w), lambda bi, i: (bi, cidx(i), 0, 0))
    vec = lambda a: pl.BlockSpec(a.shape, lambda bi, i: (0,) * a.ndim)
    in_specs = [
        main(),
        pl.BlockSpec((None, 2, SUBLANES, w), lambda bi, i: (bi, jnp.maximum(cidx(i) * (seg // 2) - 1, 0), 0, 0)),
        pl.BlockSpec((None, 1, SUBLANES, w), lambda bi, i: (bi, jnp.minimum((cidx(i) + 1) * seg, n8 - 1), 0, 0)),
        pl.BlockSpec((None, 1, w), lambda bi, i: (bi, 0, 0)),
        vec(cw), vec(cb), vec(wg), vec(ba), vec(bx), vec(lam),
    ]
    args = [lu, lu, lu, h0, cw, cb, wg, ba, bx, lam]
    if final:
        in_specs += [main(), main()]
        args += [hb, lg]
    t = seg * SUBLANES
    kern = functools.partial(_lru_kernel, seg=seg, nt=nt, rev=rev, final=final)
    return pl.pallas_call(
        kern,
        grid=(b, nt),
        in_specs=in_specs,
        out_specs=[main(), pl.BlockSpec((None, 1, w), lambda bi, i: (bi, 0, 0))],
        out_shape=[jax.ShapeDtypeStruct(lu.shape, F32), jax.ShapeDtypeStruct((b, 1, w), F32)],
        scratch_shapes=[pltpu.VMEM((LRU_RB + HALO, w), F32), pltpu.VMEM((LRU_RB + HALO, w), F32),
                        pltpu.VMEM((t, w), F32), pltpu.VMEM((t, w), F32),
                        pltpu.VMEM((SUBLANES, w), F32), pltpu.VMEM((1, w), F32)],
        compiler_params=_cparams(("parallel", "arbitrary")),
        name="lru_rev" if rev else "lru_fwd",
    )(*args)


def _ssd_kernel(*refs, batched, nb, **kw):
    for bi in range(nb):
        _ssd_chunk(*[r.at[bi] if is_b else r for r, is_b in zip(refs, batched)], **kw)


def _ssd_chunk(*refs, nc, rev, mode):
    if mode == "second":
        xc_ref, dt_ref, h0_ref, bias_r_ref, bias_c_ref, alog_r_ref, alog_c_ref, yb_ref, o_ref, hl_ref, hst_ref = refs
    else:
        (xm_ref, xp_ref, xn_ref, dt_ref, h0_ref, cw_ref, cb_ref, bias_r_ref, bias_c_ref,
         alog_r_ref, alog_c_ref) = refs[:11]
        if mode == "first":
            dsk_ref, o_ref, xc_ref, hl_ref, xb_ref, hst_ref = refs[11:]
        else:
            hl_ref, xb_ref, hst_ref = refs[11:]

    t = SSD_CHUNK
    width = SSD_HEADS * SSD_HEAD_DIM
    gn = SSD_STATE
    i = pl.program_id(0)
    c = (nc - 1 - i) if rev else i
    d = 1 if rev else 0
    tl = 0 if rev else t - 1

    @pl.when(i == 0)
    def _():
        hst_ref[...] = h0_ref[...]

    slabs = []
    for s in range((width + 2 * SSD_GROUPS * gn) // LANES):
        ls = slice(s * LANES, (s + 1) * LANES)
        if mode == "second":
            slabs.append(xc_ref[:, ls])
            continue
        xb_ref[s, 0:SUBLANES, :] = jnp.where(c > 0, xp_ref[:, ls], 0.0)
        xb_ref[s, SUBLANES:SUBLANES + t, :] = xm_ref[:, ls]
        xb_ref[s, SUBLANES + t:, :] = jnp.where(c < nc - 1, xn_ref[:, ls], 0.0)
        xc = cb_ref[:, ls]
        for k in range(4):
            xc = xc + cw_ref[k:k + 1, ls] * xb_ref[s, SUBLANES - 2 + k:SUBLANES - 2 + k + t, :]
        slabs.append(_silu(xc))
        if mode == "first":
            xc_ref[:, ls] = slabs[-1].astype(BF16)
    nx = width // LANES
    x_slabs = slabs[:nx]
    bm = slabs[nx:nx + SSD_GROUPS]
    cm = slabs[nx + SSD_GROUPS:]

    ti = lax.broadcasted_iota(jnp.int32, (t, t), 0)
    si = lax.broadcasted_iota(jnp.int32, (t, t), 1)
    inc = (si >= ti) if rev else (si <= ti)
    inc_b = jnp.where(inc, 1.0, 0.0).astype(BF16)
    inc_t_b = jnp.where((ti >= si) if rev else (ti <= si), 1.0, 0.0).astype(BF16)

    dtraw = dt_ref[...]
    dt = jax.nn.softplus(dtraw + bias_r_ref[...])
    da = dt * (-jnp.exp(alog_r_ref[...]))
    cs = sum(_dot(inc_b, p) for p in _split_bf16(da, 3))
    q0 = SSD_HEADS * d
    dtraw_t = dtraw.T[q0:q0 + SSD_HEADS, :]
    dt_t = jax.nn.softplus(dtraw_t + bias_c_ref[q0:q0 + SSD_HEADS, :])
    da_t = dt_t * (-jnp.exp(alog_c_ref[q0:q0 + SSD_HEADS, :]))
    cs_t = sum(_dot(p, inc_t_b) for p in _split_bf16(da_t, 3))
    w1_t = dt_t * jnp.exp(cs_t[:, tl:tl + 1] - cs_t)
    ecs = jnp.exp(cs)
    cs2 = cs * LOG2_E
    lcs2_t = (cs_t - jnp.log(dt_t)) * LOG2_E

    lane = lax.broadcasted_iota(jnp.int32, (t, LANES), 1)
    lo = lane < SSD_HEAD_DIM
    hg = SSD_HEADS // SSD_GROUPS
    gw = hg * SSD_HEAD_DIM

    if mode != "state":
        scores = [lax.dot_general(cm[g].astype(BF16), bm[g].astype(BF16),
                                  (((1,), (1,)), ((), ())), preferred_element_type=F32)
                  for g in range(SSD_GROUPS)]
        z_off = [_dot(cm[g].astype(BF16), hst_ref[:, g * gw:(g + 1) * gw].astype(BF16))
                 for g in range(SSD_GROUPS)]
    bm_t = [bm[g].astype(F32).T for g in range(SSD_GROUPS)]
    keep_lo = jnp.where(lo, 1.0, 0.0).astype(BF16)
    keep_hi = jnp.where(lo, 0.0, 1.0).astype(BF16)

    ys = []
    for pr in range(SSD_HEADS // 2):
        g = (2 * pr) // hg
        xp = x_slabs[pr].astype(BF16)
        rhs = jnp.concatenate([xp * keep_lo, xp * keep_hi], axis=0)
        lhs_s = jnp.concatenate([bm_t[g] * w1_t[2 * pr + e:2 * pr + e + 1, :] for e in range(2)],
                                axis=1).astype(BF16)
        if mode != "state":
            ms = []
            for e in range(2):
                hh = 2 * pr + e
                q = q0 + hh
                lmat = jnp.where(inc, jnp.exp2(cs2[:, q:q + 1] - lcs2_t[hh:hh + 1, :]), 0.0)
                ms.append((scores[g] * lmat).astype(BF16))
            lhs = jnp.concatenate([jnp.concatenate(ms, axis=1), lhs_s], axis=0)
            res = _dot(lhs, rhs)
            y_diag, s_new = res[:t], res[t:]
            e_pair = jnp.where(lo, ecs[:, q0 + 2 * pr:q0 + 2 * pr + 1], ecs[:, q0 + 2 * pr + 1:q0 + 2 * pr + 2])
            col = (pr * LANES) % gw
            ys.append(y_diag + z_off[g][:, col:col + LANES] * e_pair)
        else:
            s_new = _dot(lhs_s, rhs)
        dec_pair = jnp.where(lo[0:1, :], ecs[tl:tl + 1, q0 + 2 * pr:q0 + 2 * pr + 1],
                             ecs[tl:tl + 1, q0 + 2 * pr + 1:q0 + 2 * pr + 2])
        hst_ref[:, pr * LANES:(pr + 1) * LANES] = hst_ref[:, pr * LANES:(pr + 1) * LANES] * dec_pair + s_new

    @pl.when(i == nc - 1)
    def _():
        hl_ref[...] = hst_ref[...]

    if mode == "state":
        return
    y = jnp.concatenate(ys, axis=1)
    if mode == "first":
        o_ref[...] = y + jnp.concatenate(x_slabs, axis=1) * dsk_ref[...]
    else:
        o_ref[...] = y + yb_ref[...]


def _ssd_call(xin, dt, h0, params, *, rev, mode, yb=None, dsk=None):
    cw, cb, bias_r, bias_c, alog_r, alog_c = params
    b, nc, t, cdim = xin.shape
    width = SSD_HEADS * SSD_HEAD_DIM
    cidx = (lambda i: nc - 1 - i) if rev else (lambda i: i)
    tb = t // SUBLANES
    blk = lambda wd: pl.BlockSpec((b, None, t, wd), lambda i: (0, cidx(i), 0, 0))
    prev = pl.BlockSpec((b, None, SUBLANES, cdim), lambda i: (0, jnp.maximum(cidx(i) - 1, 0), tb - 1, 0))
    nxt = pl.BlockSpec((b, None, SUBLANES, cdim), lambda i: (0, jnp.minimum(cidx(i) + 1, nc - 1), 0, 0))
    vec = lambda a: pl.BlockSpec(a.shape, lambda i: (0,) * a.ndim)
    state = lambda: pl.BlockSpec((b, SSD_STATE, width), lambda i: (0, 0, 0))
    small = [bias_r, bias_c, alog_r, alog_c]
    y_shape = jax.ShapeDtypeStruct((b, nc, t, width), F32)
    h_shape = jax.ShapeDtypeStruct((b, SSD_STATE, width), F32)
    scratch = [pltpu.VMEM((b, SSD_STATE, width), F32)]
    if mode == "second":
        in_specs = [blk(cdim), blk(LANES), state()] + [vec(a) for a in small] + [blk(width)]
        args = [xin, dt, h0] + small + [yb]
        batched = [True] * 3 + [False] * 4 + [True]
        out_specs, out_shape = [blk(width), state()], [y_shape, h_shape]
    else:
        in_specs = [blk(cdim), prev, nxt, blk(LANES), state(), vec(cw), vec(cb)] + [vec(a) for a in small]
        args = [xin, xin, xin, dt, h0, cw, cb] + small
        batched = [True] * 5 + [False] * 6
        out_specs, out_shape = [state()], [h_shape]
        scratch = [pltpu.VMEM((b, cdim // LANES, t + 2 * SUBLANES, LANES), F32)] + scratch
        if mode == "first":
            in_specs.append(vec(dsk))
            args.append(dsk)
            batched.append(False)
            out_specs = [blk(width), blk(cdim)] + out_specs
            out_shape = [y_shape, jax.ShapeDtypeStruct((b, nc, t, cdim), BF16)] + out_shape
    batched += [True] * (len(out_specs) + len(scratch))
    kern = functools.partial(_ssd_kernel, batched=tuple(batched), nb=b, nc=nc, rev=rev, mode=mode)
    return pl.pallas_call(
        kern,
        grid=(nc,),
        in_specs=in_specs,
        out_specs=out_specs,
        out_shape=out_shape,
        scratch_shapes=scratch,
        compiler_params=_cparams(("arbitrary",)),
        name=f"ssd_{mode}_{'rev' if rev else 'fwd'}",
    )(*args)


def _outproj_kernel(x_ref, lru_ref, y_ref, z_ref, g_ref, ng_ref, w1_ref, w2_ref, o_ref, x1_ref):
    rows = x1_ref.shape[0] * x1_ref.shape[1]
    flat = lambda ref: ref[...].reshape(rows, ref.shape[2])
    mix = _dot(flat(lru_ref).astype(BF16), w1_ref[...])
    gated = flat(y_ref) * _silu(flat(z_ref))
    ms = jnp.mean(gated * gated, axis=-1, keepdims=True)
    ssd = (gated * lax.rsqrt(ms + EPS) * ng_ref[...]).astype(BF16)
    mix = mix + _dot(ssd, w2_ref[...])
    x1_ref[...] = (flat(x_ref) + g_ref[...] * mix).reshape(x1_ref.shape)
    o_ref[...] = _gather_groups(x1_ref)


def _outproj_call(x, lru, y, z, gate, norm_g, w_out):
    b, n8, _, d = x.shape
    seg = TILE // SUBLANES
    half = w_out.shape[0] // 2
    assert lru.shape[3] == half and y.shape[3] == half
    tok = lambda wd: pl.BlockSpec((None, seg, SUBLANES, wd), lambda bi, i: (bi, i, 0, 0))
    return pl.pallas_call(
        _outproj_kernel,
        grid=(b, n8 // seg),
        in_specs=[tok(d), tok(half),
                  pl.BlockSpec((None, GRID_W, SUBLANES, half), lambda bi, i: (bi, 0, i, 0)),
                  tok(z.shape[3]),
                  pl.BlockSpec((None, 1, d), lambda bi, i: (bi, 0, 0)),
                  pl.BlockSpec(norm_g.shape, lambda bi, i: (0, 0)),
                  pl.BlockSpec((half, d), lambda bi, i: (0, 0)),
                  pl.BlockSpec((half, d), lambda bi, i: (1, 0))],
        out_specs=pl.BlockSpec((None, TILE, d), lambda bi, i: (bi, i, 0)),
        out_shape=jax.ShapeDtypeStruct((b, n8 * SUBLANES, d), F32),
        scratch_shapes=[pltpu.VMEM((seg, SUBLANES, d), F32)],
        compiler_params=_cparams(("parallel", "parallel")),
        name="outproj",
    )(x, lru, y, z, gate, norm_g, w_out, w_out)


FFN_RB = 128


def _ffn_kernel(xm_ref, xp_ref, xn_ref, sh_ref, sc_ref, gt_ref, ng_ref, fg_ref,
                wup_ref, cwv_ref, cwg_ref, cbv_ref, cbg_ref, wdn_ref, o_ref,
                f_ref, uv_ref, ug_ref, act_ref, acc_ref, *, tm, nt, nj):
    i = pl.program_id(1)
    j = pl.program_id(2)
    nslab = uv_ref.shape[0]
    fb = nslab * LANES
    piece = 2 * LANES

    def up_piece(u_ref, half, q):
        col = pl.multiple_of((half * nj + j) * fb + q * piece, piece)
        res = _dot(f_ref[...], wup_ref[:, pl.ds(col, piece)])
        u_ref[2 * q] = res[:, :LANES]
        u_ref[2 * q + 1] = res[:, LANES:]

    def conv_slab(s):
        ls = slice(s * LANES, (s + 1) * LANES)
        for rb in range(tm // FFN_RB):
            r0 = SUBLANES - 1 + rb * FFN_RB
            val, gate = cbv_ref[:, ls].astype(BF16), cbg_ref[:, ls].astype(BF16)
            for k in range(3):
                val = val + cwv_ref[k:k + 1, ls].astype(BF16) * uv_ref[s, r0 + k:r0 + k + FFN_RB, :].astype(BF16)
                gate = gate + cwg_ref[k:k + 1, ls].astype(BF16) * ug_ref[s, r0 + k:r0 + k + FFN_RB, :].astype(BF16)
            act_ref[rb * FFN_RB:(rb + 1) * FFN_RB, ls] = jax.nn.gelu(gate) * val

    @pl.when(j == 0)
    def _():
        ng, sh, sc = ng_ref[...], sh_ref[...], sc_ref[...]
        fp = _norm_mod(xp_ref[...], ng, sh, sc)
        fn = _norm_mod(xn_ref[...], ng, sh, sc)
        f_ref[0:SUBLANES, :] = jnp.where(i > 0, fp, 0.0).astype(BF16)
        f_ref[SUBLANES:SUBLANES + tm, :] = _norm_mod(xm_ref[...], ng, sh, sc).astype(BF16)
        f_ref[SUBLANES + tm:, :] = jnp.where(i < nt - 1, fn, 0.0).astype(BF16)
        acc_ref[...] = jnp.zeros_like(acc_ref)

    for q in range(nslab // 2):
        up_piece(uv_ref, 0, q)
        up_piece(ug_ref, 1, q)
        conv_slab(2 * q)
        conv_slab(2 * q + 1)
    acc_ref[...] += _dot(act_ref[...], wdn_ref[j])

    @pl.when(j == nj - 1)
    def _():
        x2 = xm_ref[...] + gt_ref[...] * acc_ref[...]
        ms = jnp.mean(x2 * x2, axis=-1, keepdims=True)
        o_ref[...] = x2 * lax.rsqrt(ms + EPS) * fg_ref[...]


def _ffn_call(x1, shift, scale, gate, norm_g, final_g, w_up, conv_w, conv_b, w_down, tm, fb):
    b, l, d = x1.shape
    dff = w_down.shape[0]
    nt = l // tm
    nj = dff // fb
    tb = tm // SUBLANES
    nb8 = l // SUBLANES
    row = lambda: pl.BlockSpec((None, 1, d), lambda bi, i, j: (bi, 0, 0))
    vec = lambda: pl.BlockSpec((1, d), lambda bi, i, j: (0, 0))
    w_down = w_down.reshape(nj, fb, d)
    resident = lambda a: pl.BlockSpec(a.shape, lambda bi, i, j: (0,) * a.ndim, pipeline_mode=pl.Buffered(1))
    in_specs = [
        pl.BlockSpec((None, tm, d), lambda bi, i, j: (bi, i, 0)),
        pl.BlockSpec((None, SUBLANES, d), lambda bi, i, j: (bi, jnp.maximum(i * tb - 1, 0), 0)),
        pl.BlockSpec((None, SUBLANES, d), lambda bi, i, j: (bi, jnp.minimum((i + 1) * tb, nb8 - 1), 0)),
        row(), row(), row(), vec(), vec(),
        resident(w_up),
        pl.BlockSpec((3, fb), lambda bi, i, j: (0, j)),
        pl.BlockSpec((3, fb), lambda bi, i, j: (0, nj + j)),
        pl.BlockSpec((1, fb), lambda bi, i, j: (0, j)),
        pl.BlockSpec((1, fb), lambda bi, i, j: (0, nj + j)),
        resident(w_down),
    ]
    kern = functools.partial(_ffn_kernel, tm=tm, nt=nt, nj=nj)
    slab = lambda: pltpu.VMEM((fb // LANES, tm + 2 * SUBLANES, LANES), F32)
    return pl.pallas_call(
        kern,
        grid=(b, nt, nj),
        in_specs=in_specs,
        out_specs=pl.BlockSpec((None, tm, d), lambda bi, i, j: (bi, i, 0)),
        out_shape=jax.ShapeDtypeStruct((b, l, d), F32),
        scratch_shapes=[pltpu.VMEM((tm + 2 * SUBLANES, d), BF16), slab(), slab(),
                        pltpu.VMEM((tm, fb), BF16), pltpu.VMEM((tm, d), F32)],
        compiler_params=_cparams(("parallel", "parallel", "arbitrary")),
        name="ffn",
    )(x1, x1, x1, shift, scale, gate, norm_g, final_g, w_up, conv_w, conv_w, conv_b, conv_b, w_down)


def _block_diag_gates(wa, wx, heads_per_group):
    h, hd, _ = wa.shape
    ng = h // heads_per_group
    eye = jnp.eye(heads_per_group, dtype=wa.dtype)

    def bd(w):
        w = w.reshape(ng, heads_per_group, hd, hd)
        return jnp.einsum('gaij,ab->gaibj', w, eye).reshape(ng, heads_per_group * hd, heads_per_group * hd)

    return (0.5 * jnp.concatenate([bd(wa), bd(wx)], axis=-1)).astype(BF16)


def _pad_lanes(v, n=LANES):
    return jnp.pad(v, ((0, 0), (0, n - v.shape[1])))


def kernel(x, c, ctx, c_ctx, ada_w, ada_b, norm1_g, w_in, lru_conv_w, lru_conv_b, lru_wa, lru_ba, lru_wx, lru_bx,
           lru_lambda, ssd_conv_w, ssd_conv_b, ssd_a_log, ssd_dt_bias, ssd_d, ssd_norm_g, w_out, norm2_g,
           ffn_w_up, ffn_conv_w, ffn_conv_b, ffn_w_down, final_norm_g):
    b, l, d = x.shape
    lctx = ctx.shape[1]
    lw = lru_conv_w.shape[2]
    sw = SSD_HEADS * SSD_HEAD_DIM
    cdim = ssd_conv_w.shape[2]
    rows = l // GRID_W
    assert ada_w.shape[0] == 1, "single layer"
    assert rows == SSD_CHUNK, "an SSD chunk is one column of the latent grid"
    assert lctx % SSD_CHUNK == 0 and l % TILE == 0

    s_in = jnp.zeros((SUBLANES, d), F32).at[:b].set(c).at[b].set(c_ctx)
    mod = _mod_call(s_in, ada_w[0], ada_b)
    mod_lat = [m.reshape(b, 1, d) for m in jnp.split(mod[:b], 6, axis=-1)]
    mod_ctx = [jnp.broadcast_to(m.reshape(1, 1, d), (b, 1, d)) for m in jnp.split(mod[b:b + 1], 6, axis=-1)]
    sh1, sc1, g1, sh2, sc2, g2 = mod_lat
    csh1, csc1 = mod_ctx[0], mod_ctx[1]

    wi = w_in[0].astype(BF16)
    o1, o2, o3, o4 = lw, 2 * lw, 2 * lw + sw, 2 * lw + sw + cdim
    assert o1 % lw == 0 and o2 % sw == 0 and o3 % cdim == 0
    w_lu, w_lg, w_z, w_xbc = (wi, lw, 0), (wi, lw, o1 // lw), (wi, sw, o2 // sw), (wi, cdim, o3 // cdim)
    w_dt = (_pad_lanes(wi[:, o4:]), LANES, 0)
    seg = TILE // SUBLANES
    interleaved = lambda wd: ("groups", (l // SUBLANES, SUBLANES, wd), (seg, SUBLANES, wd), lambda i: (i, 0, 0))
    column = lambda wd: ("groups", (GRID_W, rows, wd), (GRID_W, SUBLANES, wd), lambda i: (0, i, 0))
    lu_l, lg_l, z_l, xbc_l, dt_l, x_g = _inproj_call(x, sh1, sc1, norm1_g, [
        (w_lu,) + interleaved(lw), (w_lg,) + interleaved(lw), (w_z,) + interleaved(sw),
        (w_xbc,) + column(cdim), (w_dt,) + column(LANES)], TILE, emit_x=True)
    cseg = lctx // SUBLANES
    nctx = lctx // SSD_CHUNK
    chunks = lambda wd: ("chunks", (nctx, SSD_CHUNK, wd), (nctx, SSD_CHUNK, wd), lambda i: (0, 0, 0))
    lu_c, xbc_c, dt_c = _inproj_call(ctx, csh1, csc1, norm1_g, [
        (w_lu, "groups", (cseg, SUBLANES, lw), (cseg, SUBLANES, lw), lambda i: (0, 0, 0)),
        (w_xbc,) + chunks(cdim), (w_dt,) + chunks(LANES)], lctx)

    hpg = 4
    lcw, lcb = lru_conv_w[0], lru_conv_b
    zeros_w = jnp.zeros((b, 1, lw), F32)
    lru_args = []
    for dr in range(2):
        lru_args.append((lcw, lcb, _block_diag_gates(lru_wa[0, dr], lru_wx[0, dr], hpg),
                         lru_ba[0, dr][None], lru_bx[0, dr][None], lru_lambda[0, dr][None]))
    _, hc_f = _lru_call(lu_c, zeros_w, *lru_args[0], seg=cseg, rev=False)
    _, hc_b = _lru_call(lu_c, zeros_w, *lru_args[1], seg=cseg, rev=True)
    h_b, _ = _lru_call(lu_l, hc_b, *lru_args[1], seg=seg, rev=True)
    lru_out, _ = _lru_call(lu_l, hc_f, *lru_args[0], seg=seg, rev=False, hb=h_b, lg=lg_l)

    ssd_params = (ssd_conv_w[0], ssd_conv_b,
                  _pad_lanes(ssd_dt_bias[0].reshape(1, -1)), _pad_lanes(ssd_dt_bias[0].reshape(1, -1)).T,
                  _pad_lanes(ssd_a_log[0].reshape(1, -1)), _pad_lanes(ssd_a_log[0].reshape(1, -1)).T)
    zero_state = jnp.zeros((b, SSD_STATE, sw), F32)
    (sc_f,) = _ssd_call(xbc_c, dt_c, zero_state, ssd_params, rev=False, mode="state")
    (sc_b,) = _ssd_call(xbc_c, dt_c, zero_state, ssd_params, rev=True, mode="state")
    dsk = jnp.repeat(ssd_d[0], SSD_HEAD_DIM)[None]
    y_b, xc_l, _ = _ssd_call(xbc_l, dt_l, sc_b, ssd_params, rev=True, mode="first", dsk=dsk)
    y_l, _ = _ssd_call(xc_l, dt_l, sc_f, ssd_params, rev=False, mode="second", yb=y_b)

    x1 = _outproj_call(x_g, lru_out, y_l, z_l, g1, ssd_norm_g, w_out[0].astype(BF16))
    return _ffn_call(x1, sh2, sc2, g2, norm2_g, final_norm_g[None], ffn_w_up[0].astype(BF16), ffn_conv_w[0],
                     ffn_conv_b, ffn_w_down[0].astype(BF16), TILE, 1024)
```

```python
import functools

import jax
import jax.numpy as jnp
from jax import lax
from jax.experimental import pallas as pl
from jax.experimental.pallas import tpu as pltpu

F32 = jnp.float32
BF16 = jnp.bfloat16

EPS = 1e-6
GRID_W = 64
LRU_C = 8.0
SSD_HEADS = 16
SSD_HEAD_DIM = 64
SSD_GROUPS = 2
SSD_STATE = 128
SSD_CHUNK = 128
SUBLANES = 8
LANES = 128
VMEM_LIMIT = 56 * 1024 * 1024
TILE = SUBLANES * GRID_W
MXU_WIDTH = 2 * LANES
FFN_BLOCK = 4 * MXU_WIDTH
MOD_BLOCK = 6 * MXU_WIDTH
LRU_GROUP_HEADS = 4


def _cparams(sem):
    return pltpu.CompilerParams(dimension_semantics=sem, vmem_limit_bytes=VMEM_LIMIT)


def _split_bf16(v, terms):
    parts = []
    rem = v
    for _ in range(terms):
        p = rem.astype(BF16)
        parts.append(p)
        rem = rem - p.astype(F32)
    return parts


def _dot(a, b):
    return jnp.dot(a, b, preferred_element_type=F32)


def _silu(x):
    h = 0.5 * x
    return h * jnp.tanh(h) + h


def _gather_groups(ref):
    return jnp.concatenate([ref[:, s, :] for s in range(SUBLANES)], axis=0)


def _mod_kernel(s_ref, w_ref, b_ref, o_ref):
    s = _silu(s_ref[...])
    s_hi, s_lo = _split_bf16(s, 2)
    w_hi, w_lo = _split_bf16(w_ref[...], 2)
    acc = _dot(s_hi, w_hi) + _dot(s_hi, w_lo) + _dot(s_lo, w_hi)
    o_ref[...] = acc + b_ref[...]


def _mod_call(s, w, b):
    rows, d = s.shape
    n = w.shape[1]
    nb = MOD_BLOCK
    return pl.pallas_call(
        _mod_kernel,
        grid=(n // nb,),
        in_specs=[pl.BlockSpec((rows, d), lambda j: (0, 0)),
                  pl.BlockSpec((d, nb), lambda j: (0, j)),
                  pl.BlockSpec((1, nb), lambda j: (0, j))],
        out_specs=pl.BlockSpec((rows, nb), lambda j: (0, j)),
        out_shape=jax.ShapeDtypeStruct((rows, n), F32),
        compiler_params=_cparams(("parallel",)),
        name="mod",
    )(s, w, b)


def _norm_mod(x, g, shift, scale):
    ms = jnp.mean(x * x, axis=-1, keepdims=True)
    y = x * lax.rsqrt(ms + EPS) * g
    return y * (1.0 + scale) + shift


def _inproj_kernel(x_ref, sh_ref, sc_ref, g_ref, *refs, kinds):
    n = len(kinds)
    w_refs, o_refs = refs[:n], refs[n:2 * n]
    ngrp, seg, d = x_ref.shape
    g, sh, sc = g_ref[...], sh_ref[...], sc_ref[...]
    lhs = {}
    if any(k != "groups" for k in kinds):
        lhs["rows"] = _norm_mod(x_ref[...].reshape(ngrp * seg, d), g, sh, sc).astype(BF16)
    if "groups" in kinds:
        xg = jnp.concatenate([x_ref[:, m, :] for m in range(seg)], axis=0)
        lhs["groups"] = _norm_mod(xg, g, sh, sc).astype(BF16)
        if len(refs) > 2 * n:
            refs[2 * n][...] = xg.reshape(seg, ngrp, d)
    piece = 2 * LANES
    for w_ref, o_ref, kind in zip(w_refs, o_refs, kinds):
        cols = w_ref.shape[1]
        for c0 in range(0, cols, piece):
            cs = slice(c0, min(c0 + piece, cols))
            if kind == "groups":
                res = _dot(lhs["groups"], w_ref[:, cs])
                o_ref[:, :, cs] = res.reshape(seg, ngrp, res.shape[1])
                continue
            res = _dot(lhs["rows"], w_ref[:, cs])
            rows = o_ref.shape[1]
            for ci in range(o_ref.shape[0]):
                o_ref[ci, :, cs] = res[ci * rows:(ci + 1) * rows, :]


def _inproj_call(x, shift, scale, g, outs, tm, emit_x=False):
    b, l, d = x.shape
    seg = tm // SUBLANES
    x = x.reshape(b, l // seg, seg, d)
    row = lambda: pl.BlockSpec((None, 1, d), lambda bi, i: (bi, 0, 0))
    in_specs = [pl.BlockSpec((None, SUBLANES, seg, d), lambda bi, i: (bi, i, 0, 0)), row(), row(),
                pl.BlockSpec((1, d), lambda bi, i: (0, 0))]
    in_specs += [pl.BlockSpec((d, o[0][1]), functools.partial(lambda bi, i, cb: (0, cb), cb=o[0][2])) for o in outs]
    out_specs, out_shape = [], []
    for _, _, shape, block, imap in outs:
        out_specs.append(pl.BlockSpec((None,) + block, functools.partial(lambda bi, i, f: (bi,) + f(i), f=imap)))
        out_shape.append(jax.ShapeDtypeStruct((b,) + shape, F32))
    if emit_x:
        out_specs.append(pl.BlockSpec((None, seg, SUBLANES, d), lambda bi, i: (bi, i, 0, 0)))
        out_shape.append(jax.ShapeDtypeStruct((b, l // SUBLANES, SUBLANES, d), F32))
    kern = functools.partial(_inproj_kernel, kinds=tuple(o[1] for o in outs))
    return pl.pallas_call(
        kern,
        grid=(b, l // tm),
        in_specs=in_specs,
        out_specs=out_specs,
        out_shape=out_shape,
        compiler_params=_cparams(("parallel", "parallel")),
        name="inproj",
    )(x, shift, scale, g, *[o[0][0] for o in outs])


LRU_RB = 64
SQRT_FLOOR = 1e-30
LOG2_E = 1.4426950408889634
HALO = 3 * SUBLANES


def _lru_kernel(*refs, seg, nt, rev, final):
    (lu_ref, prev_ref, next_ref, h0_ref, cw_ref, cb_ref, wg_ref, ba_ref, bx_ref, lam_ref) = refs[:10]
    pos = 10
    if final:
        hb_ref, lg_ref = refs[pos:pos + 2]
        pos += 2
    o_ref, hl_ref = refs[pos:pos + 2]
    lo_ref, hi_ref, a_ref, u_ref, st_ref, carry_ref = refs[pos + 2:]

    t = seg * SUBLANES
    i = pl.program_id(1)
    c = (nt - 1 - i) if rev else i

    @pl.when(i == 0)
    def _():
        carry_ref[...] = h0_ref[...]

    width = cw_ref.shape[1]
    row = lax.broadcasted_iota(jnp.int32, (SUBLANES, width), 0)

    spb = LRU_RB // SUBLANES
    nblk = seg // spb
    assert nblk >= 2

    def before(own, other):
        return jnp.where(row == 0, pltpu.roll(jnp.where(c > 0, other, 0.0), 1, 0), pltpu.roll(own, 1, 0))

    lo_ref[0:SUBLANES, :] = before(lu_ref[seg - 2], prev_ref[0])
    lo_ref[SUBLANES:2 * SUBLANES, :] = before(lu_ref[seg - 1], prev_ref[1])
    lo_ref[2 * SUBLANES:, :] = lu_ref[0:spb + 1].reshape((spb + 1) * SUBLANES, width)
    hi_ref[0:(spb + 2) * SUBLANES, :] = lu_ref[seg - spb - 2:seg].reshape((spb + 2) * SUBLANES, width)
    hi_ref[(spb + 2) * SUBLANES:, :] = jnp.where(
        row == SUBLANES - 1, pltpu.roll(jnp.where(c < nt - 1, next_ref[0], 0.0), SUBLANES - 1, 0),
        pltpu.roll(lu_ref[0], SUBLANES - 1, 0))

    cw = cw_ref[...]
    cb = cb_ref[...]
    half_ba = 0.5 * ba_ref[...]
    half_bx = 0.5 * bx_ref[...]
    log_decay = -LRU_C * jax.nn.softplus(-lam_ref[...])
    c2 = log_decay * (0.5 * LOG2_E)
    ngroups = wg_ref.shape[0]
    gw = width // ngroups

    def gate_block(r0, taps):
        xc = cb
        for k in range(4):
            xc = xc + cw[k:k + 1, :] * taps[k]
        for g in range(ngroups):
            sl = slice(g * gw, (g + 1) * gw)
            xg = xc[:, sl]
            pre = _dot(xg.astype(BF16), wg_ref[g])
            t_r = jnp.tanh(pre[:, :gw] + half_ba[:, sl])
            t_i = jnp.tanh(pre[:, gw:] + half_bx[:, sl])
            a = jnp.exp2(t_r * c2[:, sl] + c2[:, sl])
            y = 1.0 - a * a
            hx = 0.5 * xg
            u = (y * lax.rsqrt(jnp.maximum(y, SQRT_FLOOR))) * (hx * t_i + hx)
            a_ref[pl.ds(r0, LRU_RB), sl] = a
            u_ref[pl.ds(r0, LRU_RB), sl] = u

    def edge_taps(ref):
        return [ref[k * SUBLANES:k * SUBLANES + LRU_RB, :] for k in range(4)]

    def interior_block(rb, carry):
        taps = [lu_ref[pl.ds(rb * spb - 2 + k, spb)].reshape(LRU_RB, width) for k in range(4)]
        gate_block(pl.multiple_of(rb * LRU_RB, LRU_RB), taps)
        return carry

    gate_block(0, edge_taps(lo_ref))
    lax.fori_loop(1, nblk - 1, interior_block, 0)
    gate_block((nblk - 1) * LRU_RB, edge_taps(hi_ref))

    def slab(jj):
        j = (seg - 1 - jj) if rev else jj
        return j, pl.multiple_of(j * SUBLANES, SUBLANES)

    def seg_totals(jj, hp):
        h, p = hp
        _, r0 = slab(jj)
        a8 = a_ref[pl.ds(r0, SUBLANES), :]
        return a8 * h + u_ref[pl.ds(r0, SUBLANES), :], p * a8

    h_end, p_end = lax.fori_loop(0, seg, seg_totals,
                                 (jnp.zeros((SUBLANES, width), F32), jnp.ones((SUBLANES, width), F32)), unroll=4)

    cur = carry_ref[...]
    for r in (range(SUBLANES - 1, -1, -1) if rev else range(SUBLANES)):
        st_ref[r:r + 1, :] = cur
        cur = p_end[r:r + 1, :] * cur + h_end[r:r + 1, :]
    carry_ref[...] = cur
    hl_ref[...] = cur

    def emit(jj, h):
        j, r0 = slab(jj)
        h = a_ref[pl.ds(r0, SUBLANES), :] * h + u_ref[pl.ds(r0, SUBLANES), :]
        if final:
            o_ref[j] = (h + hb_ref[j]) * jax.nn.gelu(lg_ref[j])
        else:
            o_ref[j] = h
        return h

    lax.fori_loop(0, seg, emit, st_ref[...], unroll=4)


def _lru_call(lu, h0, cw, cb, wg, ba, bx, lam, *, seg, rev, hb=None, lg=None):
    b, n8, _, w = lu.shape
    nt = n8 // seg
    final = hb is not None
    cidx = (lambda i: nt - 1 - i) if rev else (lambda i: i)
    main = lambda: pl.BlockSpec((None, seg, SUBLANES, w), lambda bi, i: (bi, cidx(i), 0, 0))
    vec = lambda a: pl.BlockSpec(a.shape, lambda bi, i: (0,) * a.ndim)
    in_specs = [
        main(),
        pl.BlockSpec((None, 2, SUBLANES, w), lambda bi, i: (bi, jnp.maximum(cidx(i) * (seg // 2) - 1, 0), 0, 0)),
        pl.BlockSpec((None, 1, SUBLANES, w), lambda bi, i: (bi, jnp.minimum((cidx(i) + 1) * seg, n8 - 1), 0, 0)),
        pl.BlockSpec((None, 1, w), lambda bi, i: (bi, 0, 0)),
        vec(cw), vec(cb), vec(wg), vec(ba), vec(bx), vec(lam),
    ]
    args = [lu, lu, lu, h0, cw, cb, wg, ba, bx, lam]
    if final:
        in_specs += [main(), main()]
        args += [hb, lg]
    t = seg * SUBLANES
    kern = functools.partial(_lru_kernel, seg=seg, nt=nt, rev=rev, final=final)
    return pl.pallas_call(
        kern,
        grid=(b, nt),
        in_specs=in_specs,
        out_specs=[main(), pl.BlockSpec((None, 1, w), lambda bi, i: (bi, 0, 0))],
        out_shape=[jax.ShapeDtypeStruct(lu.shape, F32), jax.ShapeDtypeStruct((b, 1, w), F32)],
        scratch_shapes=[pltpu.VMEM((LRU_RB + HALO, w), F32), pltpu.VMEM((LRU_RB + HALO, w), F32),
                        pltpu.VMEM((t, w), F32), pltpu.VMEM((t, w), F32),
                        pltpu.VMEM((SUBLANES, w), F32), pltpu.VMEM((1, w), F32)],
        compiler_params=_cparams(("parallel", "arbitrary")),
        name="lru_rev" if rev else "lru_fwd",
    )(*args)


def _ssd_kernel(*refs, batched, nb, **kw):
    for bi in range(nb):
        _ssd_chunk(*[r.at[bi] if is_b else r for r, is_b in zip(refs, batched)], **kw)


def _ssd_chunk(*refs, nc, rev, mode):
    if mode == "second":
        xc_ref, dt_ref, h0_ref, bias_r_ref, bias_c_ref, alog_r_ref, alog_c_ref, yb_ref, o_ref, hl_ref, hst_ref = refs
    else:
        (xm_ref, xp_ref, xn_ref, dt_ref, h0_ref, cw_ref, cb_ref, bias_r_ref, bias_c_ref,
         alog_r_ref, alog_c_ref) = refs[:11]
        if mode == "first":
            dsk_ref, o_ref, xc_ref, hl_ref, xb_ref, hst_ref = refs[11:]
        else:
            hl_ref, xb_ref, hst_ref = refs[11:]

    t = SSD_CHUNK
    width = SSD_HEADS * SSD_HEAD_DIM
    gn = SSD_STATE
    i = pl.program_id(0)
    c = (nc - 1 - i) if rev else i
    d = 1 if rev else 0
    tl = 0 if rev else t - 1

    @pl.when(i == 0)
    def _():
        hst_ref[...] = h0_ref[...]

    slabs = []
    for s in range((width + 2 * SSD_GROUPS * gn) // LANES):
        ls = slice(s * LANES, (s + 1) * LANES)
        if mode == "second":
            slabs.append(xc_ref[:, ls])
            continue
        xb_ref[s, 0:SUBLANES, :] = jnp.where(c > 0, xp_ref[:, ls], 0.0)
        xb_ref[s, SUBLANES:SUBLANES + t, :] = xm_ref[:, ls]
        xb_ref[s, SUBLANES + t:, :] = jnp.where(c < nc - 1, xn_ref[:, ls], 0.0)
        xc = cb_ref[:, ls]
        for k in range(4):
            xc = xc + cw_ref[k:k + 1, ls] * xb_ref[s, SUBLANES - 2 + k:SUBLANES - 2 + k + t, :]
        slabs.append(_silu(xc))
        if mode == "first":
            xc_ref[:, ls] = slabs[-1].astype(BF16)
    nx = width // LANES
    x_slabs = slabs[:nx]
    bm = slabs[nx:nx + SSD_GROUPS]
    cm = slabs[nx + SSD_GROUPS:]

    ti = lax.broadcasted_iota(jnp.int32, (t, t), 0)
    si = lax.broadcasted_iota(jnp.int32, (t, t), 1)
    inc = (si >= ti) if rev else (si <= ti)
    inc_b = jnp.where(inc, 1.0, 0.0).astype(BF16)
    inc_t_b = jnp.where((ti >= si) if rev else (ti <= si), 1.0, 0.0).astype(BF16)

    dtraw = dt_ref[...]
    dt = jax.nn.softplus(dtraw + bias_r_ref[...])
    da = dt * (-jnp.exp(alog_r_ref[...]))
    cs = sum(_dot(inc_b, p) for p in _split_bf16(da, 3))
    q0 = SSD_HEADS * d
    dtraw_t = dtraw.T[q0:q0 + SSD_HEADS, :]
    dt_t = jax.nn.softplus(dtraw_t + bias_c_ref[q0:q0 + SSD_HEADS, :])
    da_t = dt_t * (-jnp.exp(alog_c_ref[q0:q0 + SSD_HEADS, :]))
    cs_t = sum(_dot(p, inc_t_b) for p in _split_bf16(da_t, 3))
    w1_t = dt_t * jnp.exp(cs_t[:, tl:tl + 1] - cs_t)
    ecs = jnp.exp(cs)
    cs2 = cs * LOG2_E
    lcs2_t = (cs_t - jnp.log(dt_t)) * LOG2_E

    lane = lax.broadcasted_iota(jnp.int32, (t, LANES), 1)
    lo = lane < SSD_HEAD_DIM
    hg = SSD_HEADS // SSD_GROUPS
    gw = hg * SSD_HEAD_DIM

    if mode != "state":
        scores = [lax.dot_general(cm[g].astype(BF16), bm[g].astype(BF16),
                                  (((1,), (1,)), ((), ())), preferred_element_type=F32)
                  for g in range(SSD_GROUPS)]
        z_off = [_dot(cm[g].astype(BF16), hst_ref[:, g * gw:(g + 1) * gw].astype(BF16))
                 for g in range(SSD_GROUPS)]
    bm_t = [bm[g].astype(F32).T for g in range(SSD_GROUPS)]
    keep_lo = jnp.where(lo, 1.0, 0.0).astype(BF16)
    keep_hi = jnp.where(lo, 0.0, 1.0).astype(BF16)

    ys = []
    for pr in range(SSD_HEADS // 2):
        g = (2 * pr) // hg
        xp = x_slabs[pr].astype(BF16)
        rhs = jnp.concatenate([xp * keep_lo, xp * keep_hi], axis=0)
        lhs_s = jnp.concatenate([bm_t[g] * w1_t[2 * pr + e:2 * pr + e + 1, :] for e in range(2)],
                                axis=1).astype(BF16)
        if mode != "state":
            ms = []
            for e in range(2):
                hh = 2 * pr + e
                q = q0 + hh
                lmat = jnp.where(inc, jnp.exp2(cs2[:, q:q + 1] - lcs2_t[hh:hh + 1, :]), 0.0)
                ms.append((scores[g] * lmat).astype(BF16))
            lhs = jnp.concatenate([jnp.concatenate(ms, axis=1), lhs_s], axis=0)
            res = _dot(lhs, rhs)
            y_diag, s_new = res[:t], res[t:]
            e_pair = jnp.where(lo, ecs[:, q0 + 2 * pr:q0 + 2 * pr + 1], ecs[:, q0 + 2 * pr + 1:q0 + 2 * pr + 2])
            col = (pr * LANES) % gw
            ys.append(y_diag + z_off[g][:, col:col + LANES] * e_pair)
        else:
            s_new = _dot(lhs_s, rhs)
        dec_pair = jnp.where(lo[0:1, :], ecs[tl:tl + 1, q0 + 2 * pr:q0 + 2 * pr + 1],
                             ecs[tl:tl + 1, q0 + 2 * pr + 1:q0 + 2 * pr + 2])
        hst_ref[:, pr * LANES:(pr + 1) * LANES] = hst_ref[:, pr * LANES:(pr + 1) * LANES] * dec_pair + s_new

    @pl.when(i == nc - 1)
    def _():
        hl_ref[...] = hst_ref[...]

    if mode == "state":
        return
    y = jnp.concatenate(ys, axis=1)
    if mode == "first":
        o_ref[...] = y + jnp.concatenate(x_slabs, axis=1) * dsk_ref[...]
    else:
        o_ref[...] = y + yb_ref[...]


def _ssd_call(xin, dt, h0, params, *, rev, mode, yb=None, dsk=None):
    cw, cb, bias_r, bias_c, alog_r, alog_c = params
    b, nc, t, cdim = xin.shape
    width = SSD_HEADS * SSD_HEAD_DIM
    cidx = (lambda i: nc - 1 - i) if rev else (lambda i: i)
    tb = t // SUBLANES
    blk = lambda wd: pl.BlockSpec((b, None, t, wd), lambda i: (0, cidx(i), 0, 0))
    prev = pl.BlockSpec((b, None, SUBLANES, cdim), lambda i: (0, jnp.maximum(cidx(i) - 1, 0), tb - 1, 0))
    nxt = pl.BlockSpec((b, None, SUBLANES, cdim), lambda i: (0, jnp.minimum(cidx(i) + 1, nc - 1), 0, 0))
    vec = lambda a: pl.BlockSpec(a.shape, lambda i: (0,) * a.ndim)
    state = lambda: pl.BlockSpec((b, SSD_STATE, width), lambda i: (0, 0, 0))
    small = [bias_r, bias_c, alog_r, alog_c]
    y_shape = jax.ShapeDtypeStruct((b, nc, t, width), F32)
    h_shape = jax.ShapeDtypeStruct((b, SSD_STATE, width), F32)
    scratch = [pltpu.VMEM((b, SSD_STATE, width), F32)]
    if mode == "second":
        in_specs = [blk(cdim), blk(LANES), state()] + [vec(a) for a in small] + [blk(width)]
        args = [xin, dt, h0] + small + [yb]
        batched = [True] * 3 + [False] * 4 + [True]
        out_specs, out_shape = [blk(width), state()], [y_shape, h_shape]
    else:
        in_specs = [blk(cdim), prev, nxt, blk(LANES), state(), vec(cw), vec(cb)] + [vec(a) for a in small]
        args = [xin, xin, xin, dt, h0, cw, cb] + small
        batched = [True] * 5 + [False] * 6
        out_specs, out_shape = [state()], [h_shape]
        scratch = [pltpu.VMEM((b, cdim // LANES, t + 2 * SUBLANES, LANES), F32)] + scratch
        if mode == "first":
            in_specs.append(vec(dsk))
            args.append(dsk)
            batched.append(False)
            out_specs = [blk(width), blk(cdim)] + out_specs
            out_shape = [y_shape, jax.ShapeDtypeStruct((b, nc, t, cdim), BF16)] + out_shape
    batched += [True] * (len(out_specs) + len(scratch))
    kern = functools.partial(_ssd_kernel, batched=tuple(batched), nb=b, nc=nc, rev=rev, mode=mode)
    return pl.pallas_call(
        kern,
        grid=(nc,),
        in_specs=in_specs,
        out_specs=out_specs,
        out_shape=out_shape,
        scratch_shapes=scratch,
        compiler_params=_cparams(("arbitrary",)),
        name=f"ssd_{mode}_{'rev' if rev else 'fwd'}",
    )(*args)


def _outproj_kernel(x_ref, lru_ref, y_ref, z_ref, g_ref, ng_ref, w1_ref, w2_ref, o_ref, x1_ref):
    rows = x1_ref.shape[0] * x1_ref.shape[1]
    flat = lambda ref: ref[...].reshape(rows, ref.shape[2])
    mix = _dot(flat(lru_ref).astype(BF16), w1_ref[...])
    gated = flat(y_ref) * _silu(flat(z_ref))
    ms = jnp.mean(gated * gated, axis=-1, keepdims=True)
    ssd = (gated * lax.rsqrt(ms + EPS) * ng_ref[...]).astype(BF16)
    mix = mix + _dot(ssd, w2_ref[...])
    x1_ref[...] = (flat(x_ref) + g_ref[...] * mix).reshape(x1_ref.shape)
    o_ref[...] = _gather_groups(x1_ref)


def _outproj_call(x, lru, y, z, gate, norm_g, w_out):
    b, n8, _, d = x.shape
    seg = TILE // SUBLANES
    half = w_out.shape[0] // 2
    assert lru.shape[3] == half and y.shape[3] == half
    tok = lambda wd: pl.BlockSpec((None, seg, SUBLANES, wd), lambda bi, i: (bi, i, 0, 0))
    return pl.pallas_call(
        _outproj_kernel,
        grid=(b, n8 // seg),
        in_specs=[tok(d), tok(half),
                  pl.BlockSpec((None, GRID_W, SUBLANES, half), lambda bi, i: (bi, 0, i, 0)),
                  tok(z.shape[3]),
                  pl.BlockSpec((None, 1, d), lambda bi, i: (bi, 0, 0)),
                  pl.BlockSpec(norm_g.shape, lambda bi, i: (0, 0)),
                  pl.BlockSpec((half, d), lambda bi, i: (0, 0)),
                  pl.BlockSpec((half, d), lambda bi, i: (1, 0))],
        out_specs=pl.BlockSpec((None, TILE, d), lambda bi, i: (bi, i, 0)),
        out_shape=jax.ShapeDtypeStruct((b, n8 * SUBLANES, d), F32),
        scratch_shapes=[pltpu.VMEM((seg, SUBLANES, d), F32)],
        compiler_params=_cparams(("parallel", "parallel")),
        name="outproj",
    )(x, lru, y, z, gate, norm_g, w_out, w_out)


FFN_RB = 128


def _ffn_kernel(xm_ref, xp_ref, xn_ref, sh_ref, sc_ref, gt_ref, ng_ref, fg_ref,
                wup_ref, cwv_ref, cwg_ref, cbv_ref, cbg_ref, wdn_ref, o_ref,
                f_ref, uv_ref, ug_ref, act_ref, acc_ref, *, tm, nt, nj):
    i = pl.program_id(1)
    j = pl.program_id(2)
    nslab = uv_ref.shape[0]
    fb = nslab * LANES
    piece = 2 * LANES

    def up_piece(u_ref, half, q):
        col = pl.multiple_of((half * nj + j) * fb + q * piece, piece)
        res = _dot(f_ref[...], wup_ref[:, pl.ds(col, piece)])
        u_ref[2 * q] = res[:, :LANES]
        u_ref[2 * q + 1] = res[:, LANES:]

    def conv_slab(s):
        ls = slice(s * LANES, (s + 1) * LANES)
        for rb in range(tm // FFN_RB):
            r0 = SUBLANES - 1 + rb * FFN_RB
            val, gate = cbv_ref[:, ls].astype(BF16), cbg_ref[:, ls].astype(BF16)
            for k in range(3):
                val = val + cwv_ref[k:k + 1, ls].astype(BF16) * uv_ref[s, r0 + k:r0 + k + FFN_RB, :].astype(BF16)
                gate = gate + cwg_ref[k:k + 1, ls].astype(BF16) * ug_ref[s, r0 + k:r0 + k + FFN_RB, :].astype(BF16)
            act_ref[rb * FFN_RB:(rb + 1) * FFN_RB, ls] = jax.nn.gelu(gate) * val

    @pl.when(j == 0)
    def _():
        ng, sh, sc = ng_ref[...], sh_ref[...], sc_ref[...]
        fp = _norm_mod(xp_ref[...], ng, sh, sc)
        fn = _norm_mod(xn_ref[...], ng, sh, sc)
        f_ref[0:SUBLANES, :] = jnp.where(i > 0, fp, 0.0).astype(BF16)
        f_ref[SUBLANES:SUBLANES + tm, :] = _norm_mod(xm_ref[...], ng, sh, sc).astype(BF16)
        f_ref[SUBLANES + tm:, :] = jnp.where(i < nt - 1, fn, 0.0).astype(BF16)
        acc_ref[...] = jnp.zeros_like(acc_ref)

    for q in range(nslab // 2):
        up_piece(uv_ref, 0, q)
        up_piece(ug_ref, 1, q)
        conv_slab(2 * q)
        conv_slab(2 * q + 1)
    acc_ref[...] += _dot(act_ref[...], wdn_ref[j])

    @pl.when(j == nj - 1)
    def _():
        x2 = xm_ref[...] + gt_ref[...] * acc_ref[...]
        ms = jnp.mean(x2 * x2, axis=-1, keepdims=True)
        o_ref[...] = x2 * lax.rsqrt(ms + EPS) * fg_ref[...]


def _ffn_call(x1, shift, scale, gate, norm_g, final_g, w_up, conv_w, conv_b, w_down, tm, fb):
    b, l, d = x1.shape
    dff = w_down.shape[0]
    nt = l // tm
    nj = dff // fb
    tb = tm // SUBLANES
    nb8 = l // SUBLANES
    row = lambda: pl.BlockSpec((None, 1, d), lambda bi, i, j: (bi, 0, 0))
    vec = lambda: pl.BlockSpec((1, d), lambda bi, i, j: (0, 0))
    w_down = w_down.reshape(nj, fb, d)
    resident = lambda a: pl.BlockSpec(a.shape, lambda bi, i, j: (0,) * a.ndim, pipeline_mode=pl.Buffered(1))
    in_specs = [
        pl.BlockSpec((None, tm, d), lambda bi, i, j: (bi, i, 0)),
        pl.BlockSpec((None, SUBLANES, d), lambda bi, i, j: (bi, jnp.maximum(i * tb - 1, 0), 0)),
        pl.BlockSpec((None, SUBLANES, d), lambda bi, i, j: (bi, jnp.minimum((i + 1) * tb, nb8 - 1), 0)),
        row(), row(), row(), vec(), vec(),
        resident(w_up),
        pl.BlockSpec((3, fb), lambda bi, i, j: (0, j)),
        pl.BlockSpec((3, fb), lambda bi, i, j: (0, nj + j)),
        pl.BlockSpec((1, fb), lambda bi, i, j: (0, j)),
        pl.BlockSpec((1, fb), lambda bi, i, j: (0, nj + j)),
        resident(w_down),
    ]
    kern = functools.partial(_ffn_kernel, tm=tm, nt=nt, nj=nj)
    slab = lambda: pltpu.VMEM((fb // LANES, tm + 2 * SUBLANES, LANES), F32)
    return pl.pallas_call(
        kern,
        grid=(b, nt, nj),
        in_specs=in_specs,
        out_specs=pl.BlockSpec((None, tm, d), lambda bi, i, j: (bi, i, 0)),
        out_shape=jax.ShapeDtypeStruct((b, l, d), F32),
        scratch_shapes=[pltpu.VMEM((tm + 2 * SUBLANES, d), BF16), slab(), slab(),
                        pltpu.VMEM((tm, fb), BF16), pltpu.VMEM((tm, d), F32)],
        compiler_params=_cparams(("parallel", "parallel", "arbitrary")),
        name="ffn",
    )(x1, x1, x1, shift, scale, gate, norm_g, final_g, w_up, conv_w, conv_w, conv_b, conv_b, w_down)


def _block_diag_gates(wa, wx, heads_per_group):
    h, hd, _ = wa.shape
    ng = h // heads_per_group
    eye = jnp.eye(heads_per_group, dtype=wa.dtype)

    def bd(w):
        w = w.reshape(ng, heads_per_group, hd, hd)
        return jnp.einsum('gaij,ab->gaibj', w, eye).reshape(ng, heads_per_group * hd, heads_per_group * hd)

    return (0.5 * jnp.concatenate([bd(wa), bd(wx)], axis=-1)).astype(BF16)


def _pad_lanes(v, n=LANES):
    return jnp.pad(v, ((0, 0), (0, n - v.shape[1])))


def kernel(x, c, ctx, c_ctx, ada_w, ada_b, norm1_g, w_in, lru_conv_w, lru_conv_b, lru_wa, lru_ba, lru_wx, lru_bx,
           lru_lambda, ssd_conv_w, ssd_conv_b, ssd_a_log, ssd_dt_bias, ssd_d, ssd_norm_g, w_out, norm2_g,
           ffn_w_up, ffn_conv_w, ffn_conv_b, ffn_w_down, final_norm_g):
    b, l, d = x.shape
    lctx = ctx.shape[1]
    lw = lru_conv_w.shape[2]
    sw = SSD_HEADS * SSD_HEAD_DIM
    cdim = ssd_conv_w.shape[2]
    rows = l // GRID_W
    assert ada_w.shape[0] == 1, "single layer"
    assert rows == SSD_CHUNK, "an SSD chunk is one column of the latent grid"
    assert lctx % SSD_CHUNK == 0 and l % TILE == 0

    s_in = jnp.zeros((SUBLANES, d), F32).at[:b].set(c).at[b].set(c_ctx)
    mod = _mod_call(s_in, ada_w[0], ada_b)
    mod_lat = [m.reshape(b, 1, d) for m in jnp.split(mod[:b], 6, axis=-1)]
    mod_ctx = [jnp.broadcast_to(m.reshape(1, 1, d), (b, 1, d)) for m in jnp.split(mod[b:b + 1], 6, axis=-1)]
    sh1, sc1, g1, sh2, sc2, g2 = mod_lat
    csh1, csc1 = mod_ctx[0], mod_ctx[1]

    wi = w_in[0].astype(BF16)
    o1, o2, o3, o4 = lw, 2 * lw, 2 * lw + sw, 2 * lw + sw + cdim
    assert o1 % lw == 0 and o2 % sw == 0 and o3 % cdim == 0
    w_lu, w_lg, w_z, w_xbc = (wi, lw, 0), (wi, lw, o1 // lw), (wi, sw, o2 // sw), (wi, cdim, o3 // cdim)
    w_dt = (_pad_lanes(wi[:, o4:]), LANES, 0)
    seg = TILE // SUBLANES
    interleaved = lambda wd: ("groups", (l // SUBLANES, SUBLANES, wd), (seg, SUBLANES, wd), lambda i: (i, 0, 0))
    column = lambda wd: ("groups", (GRID_W, rows, wd), (GRID_W, SUBLANES, wd), lambda i: (0, i, 0))
    lu_l, lg_l, z_l, xbc_l, dt_l, x_g = _inproj_call(x, sh1, sc1, norm1_g, [
        (w_lu,) + interleaved(lw), (w_lg,) + interleaved(lw), (w_z,) + interleaved(sw),
        (w_xbc,) + column(cdim), (w_dt,) + column(LANES)], TILE, emit_x=True)
    cseg = lctx // SUBLANES
    nctx = lctx // SSD_CHUNK
    chunks = lambda wd: ("chunks", (nctx, SSD_CHUNK, wd), (nctx, SSD_CHUNK, wd), lambda i: (0, 0, 0))
    lu_c, xbc_c, dt_c = _inproj_call(ctx, csh1, csc1, norm1_g, [
        (w_lu, "groups", (cseg, SUBLANES, lw), (cseg, SUBLANES, lw), lambda i: (0, 0, 0)),
        (w_xbc,) + chunks(cdim), (w_dt,) + chunks(LANES)], lctx)

    lcw, lcb = lru_conv_w[0], lru_conv_b
    zeros_w = jnp.zeros((b, 1, lw), F32)
    lru_args = []
    for dr in range(2):
        lru_args.append((lcw, lcb, _block_diag_gates(lru_wa[0, dr], lru_wx[0, dr], LRU_GROUP_HEADS),
                         lru_ba[0, dr][None], lru_bx[0, dr][None], lru_lambda[0, dr][None]))
    _, hc_f = _lru_call(lu_c, zeros_w, *lru_args[0], seg=cseg, rev=False)
    _, hc_b = _lru_call(lu_c, zeros_w, *lru_args[1], seg=cseg, rev=True)
    h_b, _ = _lru_call(lu_l, hc_b, *lru_args[1], seg=seg, rev=True)
    lru_out, _ = _lru_call(lu_l, hc_f, *lru_args[0], seg=seg, rev=False, hb=h_b, lg=lg_l)

    ssd_params = (ssd_conv_w[0], ssd_conv_b,
                  _pad_lanes(ssd_dt_bias[0].reshape(1, -1)), _pad_lanes(ssd_dt_bias[0].reshape(1, -1)).T,
                  _pad_lanes(ssd_a_log[0].reshape(1, -1)), _pad_lanes(ssd_a_log[0].reshape(1, -1)).T)
    zero_state = jnp.zeros((b, SSD_STATE, sw), F32)
    (sc_f,) = _ssd_call(xbc_c, dt_c, zero_state, ssd_params, rev=False, mode="state")
    (sc_b,) = _ssd_call(xbc_c, dt_c, zero_state, ssd_params, rev=True, mode="state")
    dsk = jnp.repeat(ssd_d[0], SSD_HEAD_DIM)[None]
    y_b, xc_l, _ = _ssd_call(xbc_l, dt_l, sc_b, ssd_params, rev=True, mode="first", dsk=dsk)
    y_l, _ = _ssd_call(xc_l, dt_l, sc_f, ssd_params, rev=False, mode="second", yb=y_b)

    x1 = _outproj_call(x_g, lru_out, y_l, z_l, g1, ssd_norm_g, w_out[0].astype(BF16))
    return _ffn_call(x1, sh2, sc2, g2, norm2_g, final_norm_g[None], ffn_w_up[0].astype(BF16), ffn_conv_w[0],
                     ffn_conv_b, ffn_w_down[0].astype(BF16), TILE, FFN_BLOCK)
```

```python
import functools

import jax
import jax.numpy as jnp
from jax import lax
from jax.experimental import pallas as pl
from jax.experimental.pallas import tpu as pltpu

F32 = jnp.float32
BF16 = jnp.bfloat16

EPS = 1e-6
GRID_W = 64
LRU_C = 8.0
SSD_HEADS = 16
SSD_HEAD_DIM = 64
SSD_GROUPS = 2
SSD_STATE = 128
SSD_CHUNK = 128
SUBLANES = 8
LANES = 128
VMEM_LIMIT = 56 * 1024 * 1024
TILE = SUBLANES * GRID_W
MXU_WIDTH = 2 * LANES
FFN_BLOCK = 4 * MXU_WIDTH
FFN_TOKENS = 2 * TILE
MOD_BLOCK = 6 * MXU_WIDTH
LRU_GROUP_HEADS = 4


def _cparams(sem):
    return pltpu.CompilerParams(dimension_semantics=sem, vmem_limit_bytes=VMEM_LIMIT)


def _split_bf16(v, terms):
    parts = []
    rem = v
    for _ in range(terms):
        p = rem.astype(BF16)
        parts.append(p)
        rem = rem - p.astype(F32)
    return parts


def _dot(a, b):
    return jnp.dot(a, b, preferred_element_type=F32)


def _silu(x):
    h = 0.5 * x
    return h * jnp.tanh(h) + h


def _gather_groups(ref):
    return jnp.concatenate([ref[:, s, :] for s in range(SUBLANES)], axis=0)


def _mod_kernel(s_ref, w_ref, b_ref, o_ref):
    s = _silu(s_ref[...])
    s_hi, s_lo = _split_bf16(s, 2)
    w_hi, w_lo = _split_bf16(w_ref[...], 2)
    acc = _dot(s_hi, w_hi) + _dot(s_hi, w_lo) + _dot(s_lo, w_hi)
    o_ref[...] = acc + b_ref[...]


def _mod_call(s, w, b):
    rows, d = s.shape
    n = w.shape[1]
    nb = MOD_BLOCK
    return pl.pallas_call(
        _mod_kernel,
        grid=(n // nb,),
        in_specs=[pl.BlockSpec((rows, d), lambda j: (0, 0)),
                  pl.BlockSpec((d, nb), lambda j: (0, j)),
                  pl.BlockSpec((1, nb), lambda j: (0, j))],
        out_specs=pl.BlockSpec((rows, nb), lambda j: (0, j)),
        out_shape=jax.ShapeDtypeStruct((rows, n), F32),
        compiler_params=_cparams(("parallel",)),
        name="mod",
    )(s, w, b)


def _norm_mod(x, g, shift, scale):
    ms = jnp.mean(x * x, axis=-1, keepdims=True)
    y = x * lax.rsqrt(ms + EPS) * g
    return y * (1.0 + scale) + shift


def _inproj_kernel(x_ref, sh_ref, sc_ref, g_ref, *refs, kinds):
    n = len(kinds)
    w_refs, o_refs = refs[:n], refs[n:2 * n]
    ngrp, seg, d = x_ref.shape
    g, sh, sc = g_ref[...], sh_ref[...], sc_ref[...]
    lhs = {}
    if any(k != "groups" for k in kinds):
        lhs["rows"] = _norm_mod(x_ref[...].reshape(ngrp * seg, d), g, sh, sc).astype(BF16)
    if "groups" in kinds:
        xg = jnp.concatenate([x_ref[:, m, :] for m in range(seg)], axis=0)
        lhs["groups"] = _norm_mod(xg, g, sh, sc).astype(BF16)
        if len(refs) > 2 * n:
            refs[2 * n][...] = xg.reshape(seg, ngrp, d)
    piece = 2 * LANES
    for w_ref, o_ref, kind in zip(w_refs, o_refs, kinds):
        cols = w_ref.shape[1]
        for c0 in range(0, cols, piece):
            cs = slice(c0, min(c0 + piece, cols))
            if kind == "groups":
                res = _dot(lhs["groups"], w_ref[:, cs])
                o_ref[:, :, cs] = res.reshape(seg, ngrp, res.shape[1])
                continue
            res = _dot(lhs["rows"], w_ref[:, cs])
            rows = o_ref.shape[1]
            for ci in range(o_ref.shape[0]):
                o_ref[ci, :, cs] = res[ci * rows:(ci + 1) * rows, :]


def _inproj_call(x, shift, scale, g, outs, tm, emit_x=False):
    b, l, d = x.shape
    seg = tm // SUBLANES
    x = x.reshape(b, l // seg, seg, d)
    row = lambda: pl.BlockSpec((None, 1, d), lambda bi, i: (bi, 0, 0))
    in_specs = [pl.BlockSpec((None, SUBLANES, seg, d), lambda bi, i: (bi, i, 0, 0)), row(), row(),
                pl.BlockSpec((1, d), lambda bi, i: (0, 0))]
    in_specs += [pl.BlockSpec((d, o[0][1]), functools.partial(lambda bi, i, cb: (0, cb), cb=o[0][2])) for o in outs]
    out_specs, out_shape = [], []
    for _, _, shape, block, imap in outs:
        out_specs.append(pl.BlockSpec((None,) + block, functools.partial(lambda bi, i, f: (bi,) + f(i), f=imap)))
        out_shape.append(jax.ShapeDtypeStruct((b,) + shape, F32))
    if emit_x:
        out_specs.append(pl.BlockSpec((None, seg, SUBLANES, d), lambda bi, i: (bi, i, 0, 0)))
        out_shape.append(jax.ShapeDtypeStruct((b, l // SUBLANES, SUBLANES, d), F32))
    kern = functools.partial(_inproj_kernel, kinds=tuple(o[1] for o in outs))
    return pl.pallas_call(
        kern,
        grid=(b, l // tm),
        in_specs=in_specs,
        out_specs=out_specs,
        out_shape=out_shape,
        compiler_params=_cparams(("parallel", "parallel")),
        name="inproj",
    )(x, shift, scale, g, *[o[0][0] for o in outs])


LRU_RB = 64
SQRT_FLOOR = 1e-30
LOG2_E = 1.4426950408889634
HALO = 3 * SUBLANES


def _lru_kernel(*refs, seg, nt, rev, final):
    (lu_ref, prev_ref, next_ref, h0_ref, cw_ref, cb_ref, wg_ref, ba_ref, bx_ref, lam_ref) = refs[:10]
    pos = 10
    if final:
        hb_ref, lg_ref = refs[pos:pos + 2]
        pos += 2
    o_ref, hl_ref = refs[pos:pos + 2]
    lo_ref, hi_ref, a_ref, u_ref, st_ref, carry_ref = refs[pos + 2:]

    t = seg * SUBLANES
    i = pl.program_id(1)
    c = (nt - 1 - i) if rev else i

    @pl.when(i == 0)
    def _():
        carry_ref[...] = h0_ref[...]

    width = cw_ref.shape[1]
    row = lax.broadcasted_iota(jnp.int32, (SUBLANES, width), 0)

    spb = LRU_RB // SUBLANES
    nblk = seg // spb
    assert nblk >= 2

    def before(own, other):
        return jnp.where(row == 0, pltpu.roll(jnp.where(c > 0, other, 0.0), 1, 0), pltpu.roll(own, 1, 0))

    lo_ref[0:SUBLANES, :] = before(lu_ref[seg - 2], prev_ref[0])
    lo_ref[SUBLANES:2 * SUBLANES, :] = before(lu_ref[seg - 1], prev_ref[1])
    lo_ref[2 * SUBLANES:, :] = lu_ref[0:spb + 1].reshape((spb + 1) * SUBLANES, width)
    hi_ref[0:(spb + 2) * SUBLANES, :] = lu_ref[seg - spb - 2:seg].reshape((spb + 2) * SUBLANES, width)
    hi_ref[(spb + 2) * SUBLANES:, :] = jnp.where(
        row == SUBLANES - 1, pltpu.roll(jnp.where(c < nt - 1, next_ref[0], 0.0), SUBLANES - 1, 0),
        pltpu.roll(lu_ref[0], SUBLANES - 1, 0))

    cw = cw_ref[...]
    cb = cb_ref[...]
    half_ba = 0.5 * ba_ref[...]
    half_bx = 0.5 * bx_ref[...]
    log_decay = -LRU_C * jax.nn.softplus(-lam_ref[...])
    c2 = log_decay * (0.5 * LOG2_E)
    ngroups = wg_ref.shape[0]
    gw = width // ngroups

    def gate_block(r0, taps):
        xc = cb
        for k in range(4):
            xc = xc + cw[k:k + 1, :] * taps[k]
        for g in range(ngroups):
            sl = slice(g * gw, (g + 1) * gw)
            xg = xc[:, sl]
            pre = _dot(xg.astype(BF16), wg_ref[g])
            t_r = jnp.tanh(pre[:, :gw] + half_ba[:, sl])
            t_i = jnp.tanh(pre[:, gw:] + half_bx[:, sl])
            a = jnp.exp2(t_r * c2[:, sl] + c2[:, sl])
            y = 1.0 - a * a
            hx = 0.5 * xg
            u = (y * lax.rsqrt(jnp.maximum(y, SQRT_FLOOR))) * (hx * t_i + hx)
            a_ref[pl.ds(r0, LRU_RB), sl] = a
            u_ref[pl.ds(r0, LRU_RB), sl] = u

    def edge_taps(ref):
        return [ref[k * SUBLANES:k * SUBLANES + LRU_RB, :] for k in range(4)]

    def interior_block(rb, carry):
        taps = [lu_ref[pl.ds(rb * spb - 2 + k, spb)].reshape(LRU_RB, width) for k in range(4)]
        gate_block(pl.multiple_of(rb * LRU_RB, LRU_RB), taps)
        return carry

    gate_block(0, edge_taps(lo_ref))
    lax.fori_loop(1, nblk - 1, interior_block, 0)
    gate_block((nblk - 1) * LRU_RB, edge_taps(hi_ref))

    def slab(jj):
        j = (seg - 1 - jj) if rev else jj
        return j, pl.multiple_of(j * SUBLANES, SUBLANES)

    def seg_totals(jj, hp):
        h, p = hp
        _, r0 = slab(jj)
        a8 = a_ref[pl.ds(r0, SUBLANES), :]
        return a8 * h + u_ref[pl.ds(r0, SUBLANES), :], p * a8

    h_end, p_end = lax.fori_loop(0, seg, seg_totals,
                                 (jnp.zeros((SUBLANES, width), F32), jnp.ones((SUBLANES, width), F32)), unroll=4)

    cur = carry_ref[...]
    for r in (range(SUBLANES - 1, -1, -1) if rev else range(SUBLANES)):
        st_ref[r:r + 1, :] = cur
        cur = p_end[r:r + 1, :] * cur + h_end[r:r + 1, :]
    carry_ref[...] = cur
    hl_ref[...] = cur

    def emit(jj, h):
        j, r0 = slab(jj)
        h = a_ref[pl.ds(r0, SUBLANES), :] * h + u_ref[pl.ds(r0, SUBLANES), :]
        if final:
            o_ref[j] = (h + hb_ref[j]) * jax.nn.gelu(lg_ref[j])
        else:
            o_ref[j] = h
        return h

    lax.fori_loop(0, seg, emit, st_ref[...], unroll=4)


def _lru_call(lu, h0, cw, cb, wg, ba, bx, lam, *, seg, rev, hb=None, lg=None):
    b, n8, _, w = lu.shape
    nt = n8 // seg
    final = hb is not None
    cidx = (lambda i: nt - 1 - i) if rev else (lambda i: i)
    main = lambda: pl.BlockSpec((None, seg, SUBLANES, w), lambda bi, i: (bi, cidx(i), 0, 0))
    vec = lambda a: pl.BlockSpec(a.shape, lambda bi, i: (0,) * a.ndim)
    in_specs = [
        main(),
        pl.BlockSpec((None, 2, SUBLANES, w), lambda bi, i: (bi, jnp.maximum(cidx(i) * (seg // 2) - 1, 0), 0, 0)),
        pl.BlockSpec((None, 1, SUBLANES, w), lambda bi, i: (bi, jnp.minimum((cidx(i) + 1) * seg, n8 - 1), 0, 0)),
        pl.BlockSpec((None, 1, w), lambda bi, i: (bi, 0, 0)),
        vec(cw), vec(cb), vec(wg), vec(ba), vec(bx), vec(lam),
    ]
    args = [lu, lu, lu, h0, cw, cb, wg, ba, bx, lam]
    if final:
        in_specs += [main(), main()]
        args += [hb, lg]
    t = seg * SUBLANES
    kern = functools.partial(_lru_kernel, seg=seg, nt=nt, rev=rev, final=final)
    return pl.pallas_call(
        kern,
        grid=(b, nt),
        in_specs=in_specs,
        out_specs=[main(), pl.BlockSpec((None, 1, w), lambda bi, i: (bi, 0, 0))],
        out_shape=[jax.ShapeDtypeStruct(lu.shape, F32), jax.ShapeDtypeStruct((b, 1, w), F32)],
        scratch_shapes=[pltpu.VMEM((LRU_RB + HALO, w), F32), pltpu.VMEM((LRU_RB + HALO, w), F32),
                        pltpu.VMEM((t, w), F32), pltpu.VMEM((t, w), F32),
                        pltpu.VMEM((SUBLANES, w), F32), pltpu.VMEM((1, w), F32)],
        compiler_params=_cparams(("parallel", "arbitrary")),
        name="lru_rev" if rev else "lru_fwd",
    )(*args)


def _ssd_kernel(*refs, batched, nb, **kw):
    for bi in range(nb):
        _ssd_chunk(*[r.at[bi] if is_b else r for r, is_b in zip(refs, batched)], **kw)


def _ssd_chunk(*refs, nc, rev, mode):
    if mode == "second":
        xc_ref, dt_ref, h0_ref, bias_r_ref, bias_c_ref, alog_r_ref, alog_c_ref, yb_ref, o_ref, hl_ref, hst_ref = refs
    else:
        (xm_ref, xp_ref, xn_ref, dt_ref, h0_ref, cw_ref, cb_ref, bias_r_ref, bias_c_ref,
         alog_r_ref, alog_c_ref) = refs[:11]
        if mode == "first":
            dsk_ref, o_ref, xc_ref, hl_ref, xb_ref, hst_ref = refs[11:]
        else:
            hl_ref, xb_ref, hst_ref = refs[11:]

    t = SSD_CHUNK
    width = SSD_HEADS * SSD_HEAD_DIM
    gn = SSD_STATE
    i = pl.program_id(0)
    c = (nc - 1 - i) if rev else i
    d = 1 if rev else 0
    tl = 0 if rev else t - 1

    @pl.when(i == 0)
    def _():
        hst_ref[...] = h0_ref[...]

    slabs = []
    for s in range((width + 2 * SSD_GROUPS * gn) // LANES):
        ls = slice(s * LANES, (s + 1) * LANES)
        if mode == "second":
            slabs.append(xc_ref[:, ls])
            continue
        xb_ref[s, 0:SUBLANES, :] = jnp.where(c > 0, xp_ref[:, ls], 0.0)
        xb_ref[s, SUBLANES:SUBLANES + t, :] = xm_ref[:, ls]
        xb_ref[s, SUBLANES + t:, :] = jnp.where(c < nc - 1, xn_ref[:, ls], 0.0)
        xc = cb_ref[:, ls]
        for k in range(4):
            xc = xc + cw_ref[k:k + 1, ls] * xb_ref[s, SUBLANES - 2 + k:SUBLANES - 2 + k + t, :]
        slabs.append(_silu(xc))
        if mode == "first":
            xc_ref[:, ls] = slabs[-1].astype(BF16)
    nx = width // LANES
    x_slabs = slabs[:nx]
    bm = slabs[nx:nx + SSD_GROUPS]
    cm = slabs[nx + SSD_GROUPS:]

    ti = lax.broadcasted_iota(jnp.int32, (t, t), 0)
    si = lax.broadcasted_iota(jnp.int32, (t, t), 1)
    inc = (si >= ti) if rev else (si <= ti)
    inc_b = jnp.where(inc, 1.0, 0.0).astype(BF16)
    inc_t_b = jnp.where((ti >= si) if rev else (ti <= si), 1.0, 0.0).astype(BF16)

    dtraw = dt_ref[...]
    dt = jax.nn.softplus(dtraw + bias_r_ref[...])
    da = dt * (-jnp.exp(alog_r_ref[...]))
    cs = sum(_dot(inc_b, p) for p in _split_bf16(da, 3))
    q0 = SSD_HEADS * d
    dtraw_t = dtraw.T[q0:q0 + SSD_HEADS, :]
    dt_t = jax.nn.softplus(dtraw_t + bias_c_ref[q0:q0 + SSD_HEADS, :])
    da_t = dt_t * (-jnp.exp(alog_c_ref[q0:q0 + SSD_HEADS, :]))
    cs_t = sum(_dot(p, inc_t_b) for p in _split_bf16(da_t, 3))
    w1_t = dt_t * jnp.exp(cs_t[:, tl:tl + 1] - cs_t)
    ecs = jnp.exp(cs)
    cs2 = cs * LOG2_E
    lcs2_t = (cs_t - jnp.log(dt_t)) * LOG2_E

    lane = lax.broadcasted_iota(jnp.int32, (t, LANES), 1)
    lo = lane < SSD_HEAD_DIM
    hg = SSD_HEADS // SSD_GROUPS
    gw = hg * SSD_HEAD_DIM

    if mode != "state":
        scores = [lax.dot_general(cm[g].astype(BF16), bm[g].astype(BF16),
                                  (((1,), (1,)), ((), ())), preferred_element_type=F32)
                  for g in range(SSD_GROUPS)]
        z_off = [_dot(cm[g].astype(BF16), hst_ref[:, g * gw:(g + 1) * gw].astype(BF16))
                 for g in range(SSD_GROUPS)]
    bm_t = [bm[g].astype(F32).T for g in range(SSD_GROUPS)]
    keep_lo = jnp.where(lo, 1.0, 0.0).astype(BF16)
    keep_hi = jnp.where(lo, 0.0, 1.0).astype(BF16)

    ys = []
    for pr in range(SSD_HEADS // 2):
        g = (2 * pr) // hg
        xp = x_slabs[pr].astype(BF16)
        rhs = jnp.concatenate([xp * keep_lo, xp * keep_hi], axis=0)
        lhs_s = jnp.concatenate([bm_t[g] * w1_t[2 * pr + e:2 * pr + e + 1, :] for e in range(2)],
                                axis=1).astype(BF16)
        if mode != "state":
            ms = []
            for e in range(2):
                hh = 2 * pr + e
                q = q0 + hh
                lmat = jnp.where(inc, jnp.exp2(cs2[:, q:q + 1] - lcs2_t[hh:hh + 1, :]), 0.0)
                ms.append((scores[g] * lmat).astype(BF16))
            lhs = jnp.concatenate([jnp.concatenate(ms, axis=1), lhs_s], axis=0)
            res = _dot(lhs, rhs)
            y_diag, s_new = res[:t], res[t:]
            e_pair = jnp.where(lo, ecs[:, q0 + 2 * pr:q0 + 2 * pr + 1], ecs[:, q0 + 2 * pr + 1:q0 + 2 * pr + 2])
            col = (pr * LANES) % gw
            ys.append(y_diag + z_off[g][:, col:col + LANES] * e_pair)
        else:
            s_new = _dot(lhs_s, rhs)
        dec_pair = jnp.where(lo[0:1, :], ecs[tl:tl + 1, q0 + 2 * pr:q0 + 2 * pr + 1],
                             ecs[tl:tl + 1, q0 + 2 * pr + 1:q0 + 2 * pr + 2])
        hst_ref[:, pr * LANES:(pr + 1) * LANES] = hst_ref[:, pr * LANES:(pr + 1) * LANES] * dec_pair + s_new

    @pl.when(i == nc - 1)
    def _():
        hl_ref[...] = hst_ref[...]

    if mode == "state":
        return
    y = jnp.concatenate(ys, axis=1)
    if mode == "first":
        o_ref[...] = y + jnp.concatenate(x_slabs, axis=1) * dsk_ref[...]
    else:
        o_ref[...] = y + yb_ref[...]


def _ssd_call(xin, dt, h0, params, *, rev, mode, yb=None, dsk=None):
    cw, cb, bias_r, bias_c, alog_r, alog_c = params
    b, nc, t, cdim = xin.shape
    width = SSD_HEADS * SSD_HEAD_DIM
    cidx = (lambda i: nc - 1 - i) if rev else (lambda i: i)
    tb = t // SUBLANES
    blk = lambda wd: pl.BlockSpec((b, None, t, wd), lambda i: (0, cidx(i), 0, 0))
    prev = pl.BlockSpec((b, None, SUBLANES, cdim), lambda i: (0, jnp.maximum(cidx(i) - 1, 0), tb - 1, 0))
    nxt = pl.BlockSpec((b, None, SUBLANES, cdim), lambda i: (0, jnp.minimum(cidx(i) + 1, nc - 1), 0, 0))
    vec = lambda a: pl.BlockSpec(a.shape, lambda i: (0,) * a.ndim)
    state = lambda: pl.BlockSpec((b, SSD_STATE, width), lambda i: (0, 0, 0))
    small = [bias_r, bias_c, alog_r, alog_c]
    y_shape = jax.ShapeDtypeStruct((b, nc, t, width), F32)
    h_shape = jax.ShapeDtypeStruct((b, SSD_STATE, width), F32)
    scratch = [pltpu.VMEM((b, SSD_STATE, width), F32)]
    if mode == "second":
        in_specs = [blk(cdim), blk(LANES), state()] + [vec(a) for a in small] + [blk(width)]
        args = [xin, dt, h0] + small + [yb]
        batched = [True] * 3 + [False] * 4 + [True]
        out_specs, out_shape = [blk(width), state()], [y_shape, h_shape]
    else:
        in_specs = [blk(cdim), prev, nxt, blk(LANES), state(), vec(cw), vec(cb)] + [vec(a) for a in small]
        args = [xin, xin, xin, dt, h0, cw, cb] + small
        batched = [True] * 5 + [False] * 6
        out_specs, out_shape = [state()], [h_shape]
        scratch = [pltpu.VMEM((b, cdim // LANES, t + 2 * SUBLANES, LANES), F32)] + scratch
        if mode == "first":
            in_specs.append(vec(dsk))
            args.append(dsk)
            batched.append(False)
            out_specs = [blk(width), blk(cdim)] + out_specs
            out_shape = [y_shape, jax.ShapeDtypeStruct((b, nc, t, cdim), BF16)] + out_shape
    batched += [True] * (len(out_specs) + len(scratch))
    kern = functools.partial(_ssd_kernel, batched=tuple(batched), nb=b, nc=nc, rev=rev, mode=mode)
    return pl.pallas_call(
        kern,
        grid=(nc,),
        in_specs=in_specs,
        out_specs=out_specs,
        out_shape=out_shape,
        scratch_shapes=scratch,
        compiler_params=_cparams(("arbitrary",)),
        name=f"ssd_{mode}_{'rev' if rev else 'fwd'}",
    )(*args)


def _outproj_kernel(x_ref, lru_ref, y_ref, z_ref, g_ref, ng_ref, w1_ref, w2_ref, o_ref, x1_ref):
    rows = x1_ref.shape[0] * x1_ref.shape[1]
    flat = lambda ref: ref[...].reshape(rows, ref.shape[2])
    mix = _dot(flat(lru_ref).astype(BF16), w1_ref[...])
    gated = flat(y_ref) * _silu(flat(z_ref))
    ms = jnp.mean(gated * gated, axis=-1, keepdims=True)
    ssd = (gated * lax.rsqrt(ms + EPS) * ng_ref[...]).astype(BF16)
    mix = mix + _dot(ssd, w2_ref[...])
    x1_ref[...] = (flat(x_ref) + g_ref[...] * mix).reshape(x1_ref.shape)
    o_ref[...] = _gather_groups(x1_ref)


def _outproj_call(x, lru, y, z, gate, norm_g, w_out):
    b, n8, _, d = x.shape
    seg = TILE // SUBLANES
    half = w_out.shape[0] // 2
    assert lru.shape[3] == half and y.shape[3] == half
    tok = lambda wd: pl.BlockSpec((None, seg, SUBLANES, wd), lambda bi, i: (bi, i, 0, 0))
    return pl.pallas_call(
        _outproj_kernel,
        grid=(b, n8 // seg),
        in_specs=[tok(d), tok(half),
                  pl.BlockSpec((None, GRID_W, SUBLANES, half), lambda bi, i: (bi, 0, i, 0)),
                  tok(z.shape[3]),
                  pl.BlockSpec((None, 1, d), lambda bi, i: (bi, 0, 0)),
                  pl.BlockSpec(norm_g.shape, lambda bi, i: (0, 0)),
                  pl.BlockSpec((half, d), lambda bi, i: (0, 0)),
                  pl.BlockSpec((half, d), lambda bi, i: (1, 0))],
        out_specs=pl.BlockSpec((None, TILE, d), lambda bi, i: (bi, i, 0)),
        out_shape=jax.ShapeDtypeStruct((b, n8 * SUBLANES, d), F32),
        scratch_shapes=[pltpu.VMEM((seg, SUBLANES, d), F32)],
        compiler_params=_cparams(("parallel", "parallel")),
        name="outproj",
    )(x, lru, y, z, gate, norm_g, w_out, w_out)


FFN_RB = 128


def _ffn_kernel(xm_ref, xp_ref, xn_ref, sh_ref, sc_ref, gt_ref, ng_ref, fg_ref,
                wup_ref, cwv_ref, cwg_ref, cbv_ref, cbg_ref, wdn_ref, o_ref,
                f_ref, uv_ref, ug_ref, act_ref, acc_ref, *, tm, nt, nj):
    i = pl.program_id(1)
    j = pl.program_id(2)
    nslab = uv_ref.shape[0]
    fb = nslab * LANES
    piece = 2 * LANES

    def up_piece(u_ref, half, q):
        col = pl.multiple_of((half * nj + j) * fb + q * piece, piece)
        res = _dot(f_ref[...], wup_ref[:, pl.ds(col, piece)])
        u_ref[2 * q] = res[:, :LANES]
        u_ref[2 * q + 1] = res[:, LANES:]

    def conv_slab(s):
        ls = slice(s * LANES, (s + 1) * LANES)
        for rb in range(tm // FFN_RB):
            r0 = SUBLANES - 1 + rb * FFN_RB
            val, gate = cbv_ref[:, ls].astype(BF16), cbg_ref[:, ls].astype(BF16)
            for k in range(3):
                val = val + cwv_ref[k:k + 1, ls].astype(BF16) * uv_ref[s, r0 + k:r0 + k + FFN_RB, :].astype(BF16)
                gate = gate + cwg_ref[k:k + 1, ls].astype(BF16) * ug_ref[s, r0 + k:r0 + k + FFN_RB, :].astype(BF16)
            act_ref[rb * FFN_RB:(rb + 1) * FFN_RB, ls] = jax.nn.gelu(gate) * val

    @pl.when(j == 0)
    def _():
        ng, sh, sc = ng_ref[...], sh_ref[...], sc_ref[...]
        fp = _norm_mod(xp_ref[...], ng, sh, sc)
        fn = _norm_mod(xn_ref[...], ng, sh, sc)
        f_ref[0:SUBLANES, :] = jnp.where(i > 0, fp, 0.0).astype(BF16)
        f_ref[SUBLANES:SUBLANES + tm, :] = _norm_mod(xm_ref[...], ng, sh, sc).astype(BF16)
        f_ref[SUBLANES + tm:, :] = jnp.where(i < nt - 1, fn, 0.0).astype(BF16)
        acc_ref[...] = jnp.zeros_like(acc_ref)

    for q in range(nslab // 2):
        up_piece(uv_ref, 0, q)
        up_piece(ug_ref, 1, q)
        conv_slab(2 * q)
        conv_slab(2 * q + 1)
    acc_ref[...] += _dot(act_ref[...], wdn_ref[j])

    @pl.when(j == nj - 1)
    def _():
        x2 = xm_ref[...] + gt_ref[...] * acc_ref[...]
        ms = jnp.mean(x2 * x2, axis=-1, keepdims=True)
        o_ref[...] = x2 * lax.rsqrt(ms + EPS) * fg_ref[...]


def _ffn_call(x1, shift, scale, gate, norm_g, final_g, w_up, conv_w, conv_b, w_down, tm, fb):
    b, l, d = x1.shape
    dff = w_down.shape[0]
    nt = l // tm
    nj = dff // fb
    tb = tm // SUBLANES
    nb8 = l // SUBLANES
    row = lambda: pl.BlockSpec((None, 1, d), lambda bi, i, j: (bi, 0, 0))
    vec = lambda: pl.BlockSpec((1, d), lambda bi, i, j: (0, 0))
    w_down = w_down.reshape(nj, fb, d)
    resident = lambda a: pl.BlockSpec(a.shape, lambda bi, i, j: (0,) * a.ndim, pipeline_mode=pl.Buffered(1))
    in_specs = [
        pl.BlockSpec((None, tm, d), lambda bi, i, j: (bi, i, 0)),
        pl.BlockSpec((None, SUBLANES, d), lambda bi, i, j: (bi, jnp.maximum(i * tb - 1, 0), 0)),
        pl.BlockSpec((None, SUBLANES, d), lambda bi, i, j: (bi, jnp.minimum((i + 1) * tb, nb8 - 1), 0)),
        row(), row(), row(), vec(), vec(),
        resident(w_up),
        pl.BlockSpec((3, fb), lambda bi, i, j: (0, j)),
        pl.BlockSpec((3, fb), lambda bi, i, j: (0, nj + j)),
        pl.BlockSpec((1, fb), lambda bi, i, j: (0, j)),
        pl.BlockSpec((1, fb), lambda bi, i, j: (0, nj + j)),
        resident(w_down),
    ]
    kern = functools.partial(_ffn_kernel, tm=tm, nt=nt, nj=nj)
    slab = lambda: pltpu.VMEM((fb // LANES, tm + 2 * SUBLANES, LANES), F32)
    return pl.pallas_call(
        kern,
        grid=(b, nt, nj),
        in_specs=in_specs,
        out_specs=pl.BlockSpec((None, tm, d), lambda bi, i, j: (bi, i, 0)),
        out_shape=jax.ShapeDtypeStruct((b, l, d), F32),
        scratch_shapes=[pltpu.VMEM((tm + 2 * SUBLANES, d), BF16), slab(), slab(),
                        pltpu.VMEM((tm, fb), BF16), pltpu.VMEM((tm, d), F32)],
        compiler_params=_cparams(("parallel", "parallel", "arbitrary")),
        name="ffn",
    )(x1, x1, x1, shift, scale, gate, norm_g, final_g, w_up, conv_w, conv_w, conv_b, conv_b, w_down)


def _block_diag_gates(wa, wx, heads_per_group):
    h, hd, _ = wa.shape
    ng = h // heads_per_group
    eye = jnp.eye(heads_per_group, dtype=wa.dtype)

    def bd(w):
        w = w.reshape(ng, heads_per_group, hd, hd)
        return jnp.einsum('gaij,ab->gaibj', w, eye).reshape(ng, heads_per_group * hd, heads_per_group * hd)

    return (0.5 * jnp.concatenate([bd(wa), bd(wx)], axis=-1)).astype(BF16)


def _pad_lanes(v, n=LANES):
    return jnp.pad(v, ((0, 0), (0, n - v.shape[1])))


def kernel(x, c, ctx, c_ctx, ada_w, ada_b, norm1_g, w_in, lru_conv_w, lru_conv_b, lru_wa, lru_ba, lru_wx, lru_bx,
           lru_lambda, ssd_conv_w, ssd_conv_b, ssd_a_log, ssd_dt_bias, ssd_d, ssd_norm_g, w_out, norm2_g,
           ffn_w_up, ffn_conv_w, ffn_conv_b, ffn_w_down, final_norm_g):
    b, l, d = x.shape
    lctx = ctx.shape[1]
    lw = lru_conv_w.shape[2]
    sw = SSD_HEADS * SSD_HEAD_DIM
    cdim = ssd_conv_w.shape[2]
    rows = l // GRID_W
    assert ada_w.shape[0] == 1, "single layer"
    assert rows == SSD_CHUNK, "an SSD chunk is one column of the latent grid"
    assert lctx % SSD_CHUNK == 0 and l % TILE == 0

    s_in = jnp.zeros((SUBLANES, d), F32).at[:b].set(c).at[b].set(c_ctx)
    mod = _mod_call(s_in, ada_w[0], ada_b)
    mod_lat = [m.reshape(b, 1, d) for m in jnp.split(mod[:b], 6, axis=-1)]
    mod_ctx = [jnp.broadcast_to(m.reshape(1, 1, d), (b, 1, d)) for m in jnp.split(mod[b:b + 1], 6, axis=-1)]
    sh1, sc1, g1, sh2, sc2, g2 = mod_lat
    csh1, csc1 = mod_ctx[0], mod_ctx[1]

    wi = w_in[0].astype(BF16)
    o1, o2, o3, o4 = lw, 2 * lw, 2 * lw + sw, 2 * lw + sw + cdim
    assert o1 % lw == 0 and o2 % sw == 0 and o3 % cdim == 0
    w_lu, w_lg, w_z, w_xbc = (wi, lw, 0), (wi, lw, o1 // lw), (wi, sw, o2 // sw), (wi, cdim, o3 // cdim)
    w_dt = (_pad_lanes(wi[:, o4:]), LANES, 0)
    seg = TILE // SUBLANES
    interleaved = lambda wd: ("groups", (l // SUBLANES, SUBLANES, wd), (seg, SUBLANES, wd), lambda i: (i, 0, 0))
    column = lambda wd: ("groups", (GRID_W, rows, wd), (GRID_W, SUBLANES, wd), lambda i: (0, i, 0))
    lu_l, lg_l, z_l, xbc_l, dt_l, x_g = _inproj_call(x, sh1, sc1, norm1_g, [
        (w_lu,) + interleaved(lw), (w_lg,) + interleaved(lw), (w_z,) + interleaved(sw),
        (w_xbc,) + column(cdim), (w_dt,) + column(LANES)], TILE, emit_x=True)
    cseg = lctx // SUBLANES
    nctx = lctx // SSD_CHUNK
    chunks = lambda wd: ("chunks", (nctx, SSD_CHUNK, wd), (nctx, SSD_CHUNK, wd), lambda i: (0, 0, 0))
    lu_c, xbc_c, dt_c = _inproj_call(ctx, csh1, csc1, norm1_g, [
        (w_lu, "groups", (cseg, SUBLANES, lw), (cseg, SUBLANES, lw), lambda i: (0, 0, 0)),
        (w_xbc,) + chunks(cdim), (w_dt,) + chunks(LANES)], lctx)

    lcw, lcb = lru_conv_w[0], lru_conv_b
    zeros_w = jnp.zeros((b, 1, lw), F32)
    lru_args = []
    for dr in range(2):
        lru_args.append((lcw, lcb, _block_diag_gates(lru_wa[0, dr], lru_wx[0, dr], LRU_GROUP_HEADS),
                         lru_ba[0, dr][None], lru_bx[0, dr][None], lru_lambda[0, dr][None]))
    _, hc_f = _lru_call(lu_c, zeros_w, *lru_args[0], seg=cseg, rev=False)
    _, hc_b = _lru_call(lu_c, zeros_w, *lru_args[1], seg=cseg, rev=True)
    h_b, _ = _lru_call(lu_l, hc_b, *lru_args[1], seg=seg, rev=True)
    lru_out, _ = _lru_call(lu_l, hc_f, *lru_args[0], seg=seg, rev=False, hb=h_b, lg=lg_l)

    ssd_params = (ssd_conv_w[0], ssd_conv_b,
                  _pad_lanes(ssd_dt_bias[0].reshape(1, -1)), _pad_lanes(ssd_dt_bias[0].reshape(1, -1)).T,
                  _pad_lanes(ssd_a_log[0].reshape(1, -1)), _pad_lanes(ssd_a_log[0].reshape(1, -1)).T)
    zero_state = jnp.zeros((b, SSD_STATE, sw), F32)
    (sc_f,) = _ssd_call(xbc_c, dt_c, zero_state, ssd_params, rev=False, mode="state")
    (sc_b,) = _ssd_call(xbc_c, dt_c, zero_state, ssd_params, rev=True, mode="state")
    dsk = jnp.repeat(ssd_d[0], SSD_HEAD_DIM)[None]
    y_b, xc_l, _ = _ssd_call(xbc_l, dt_l, sc_b, ssd_params, rev=True, mode="first", dsk=dsk)
    y_l, _ = _ssd_call(xc_l, dt_l, sc_f, ssd_params, rev=False, mode="second", yb=y_b)

    x1 = _outproj_call(x_g, lru_out, y_l, z_l, g1, ssd_norm_g, w_out[0].astype(BF16))
    return _ffn_call(x1, sh2, sc2, g2, norm2_g, final_norm_g[None], ffn_w_up[0].astype(BF16), ffn_conv_w[0],
                     ffn_conv_b, ffn_w_down[0].astype(BF16), FFN_TOKENS, FFN_BLOCK)
```

```python
import functools

import jax
import jax.numpy as jnp
from jax import lax
from jax.experimental import pallas as pl
from jax.experimental.pallas import tpu as pltpu

F32 = jnp.float32
BF16 = jnp.bfloat16

EPS = 1e-6
GRID_W = 64
LRU_C = 8.0
SSD_HEADS = 16
SSD_HEAD_DIM = 64
SSD_GROUPS = 2
SSD_STATE = 128
SSD_CHUNK = 128
SUBLANES = 8
LANES = 128
VMEM_LIMIT = 56 * 1024 * 1024
TILE = SUBLANES * GRID_W
MXU_WIDTH = 2 * LANES
FFN_BLOCK = 4 * MXU_WIDTH
FFN_TOKENS = 2 * TILE
MOD_BLOCK = 6 * MXU_WIDTH
LRU_GROUP_HEADS = 4


def _cparams(sem):
    return pltpu.CompilerParams(dimension_semantics=sem, vmem_limit_bytes=VMEM_LIMIT)


def _split_bf16(v, terms):
    parts = []
    rem = v
    for _ in range(terms):
        p = rem.astype(BF16)
        parts.append(p)
        rem = rem - p.astype(F32)
    return parts


def _dot(a, b):
    return jnp.dot(a, b, preferred_element_type=F32)


def _silu(x):
    h = 0.5 * x
    return h * jnp.tanh(h) + h


def _gather_groups(ref):
    return jnp.concatenate([ref[:, s, :] for s in range(SUBLANES)], axis=0)


def _mod_kernel(s_ref, w_ref, b_ref, o_ref):
    s = _silu(s_ref[...])
    s_hi, s_lo = _split_bf16(s, 2)
    w_hi, w_lo = _split_bf16(w_ref[...], 2)
    acc = _dot(s_hi, w_hi) + _dot(s_hi, w_lo) + _dot(s_lo, w_hi)
    o_ref[...] = acc + b_ref[...]


def _mod_call(s, w, b):
    rows, d = s.shape
    n = w.shape[1]
    nb = MOD_BLOCK
    return pl.pallas_call(
        _mod_kernel,
        grid=(n // nb,),
        in_specs=[pl.BlockSpec((rows, d), lambda j: (0, 0)),
                  pl.BlockSpec((d, nb), lambda j: (0, j)),
                  pl.BlockSpec((1, nb), lambda j: (0, j))],
        out_specs=pl.BlockSpec((rows, nb), lambda j: (0, j)),
        out_shape=jax.ShapeDtypeStruct((rows, n), F32),
        compiler_params=_cparams(("parallel",)),
        name="mod",
    )(s, w, b)


def _norm_mod(x, g, shift, scale):
    ms = jnp.mean(x * x, axis=-1, keepdims=True)
    y = x * lax.rsqrt(ms + EPS) * g
    return y * (1.0 + scale) + shift


def _inproj_kernel(x_ref, sh_ref, sc_ref, g_ref, *refs, kinds):
    n = len(kinds)
    w_refs, o_refs = refs[:n], refs[n:2 * n]
    ngrp, seg, d = x_ref.shape
    g, sh, sc = g_ref[...], sh_ref[...], sc_ref[...]
    lhs = {}
    if any(k != "groups" for k in kinds):
        lhs["rows"] = _norm_mod(x_ref[...].reshape(ngrp * seg, d), g, sh, sc).astype(BF16)
    if "groups" in kinds:
        xg = jnp.concatenate([x_ref[:, m, :] for m in range(seg)], axis=0)
        lhs["groups"] = _norm_mod(xg, g, sh, sc).astype(BF16)
        if len(refs) > 2 * n:
            refs[2 * n][...] = xg.reshape(seg, ngrp, d)
    piece = 2 * LANES
    for w_ref, o_ref, kind in zip(w_refs, o_refs, kinds):
        cols = w_ref.shape[1]
        for c0 in range(0, cols, piece):
            cs = slice(c0, min(c0 + piece, cols))
            if kind == "groups":
                res = _dot(lhs["groups"], w_ref[:, cs])
                o_ref[:, :, cs] = res.reshape(seg, ngrp, res.shape[1])
                continue
            res = _dot(lhs["rows"], w_ref[:, cs])
            rows = o_ref.shape[1]
            for ci in range(o_ref.shape[0]):
                o_ref[ci, :, cs] = res[ci * rows:(ci + 1) * rows, :]


def _inproj_call(x, shift, scale, g, outs, tm, emit_x=False):
    b, l, d = x.shape
    seg = tm // SUBLANES
    x = x.reshape(b, l // seg, seg, d)
    row = lambda: pl.BlockSpec((None, 1, d), lambda bi, i: (bi, 0, 0))
    in_specs = [pl.BlockSpec((None, SUBLANES, seg, d), lambda bi, i: (bi, i, 0, 0)), row(), row(),
                pl.BlockSpec((1, d), lambda bi, i: (0, 0))]
    in_specs += [pl.BlockSpec((d, o[0][1]), functools.partial(lambda bi, i, cb: (0, cb), cb=o[0][2])) for o in outs]
    out_specs, out_shape = [], []
    for _, _, shape, block, imap in outs:
        out_specs.append(pl.BlockSpec((None,) + block, functools.partial(lambda bi, i, f: (bi,) + f(i), f=imap)))
        out_shape.append(jax.ShapeDtypeStruct((b,) + shape, F32))
    if emit_x:
        out_specs.append(pl.BlockSpec((None, seg, SUBLANES, d), lambda bi, i: (bi, i, 0, 0)))
        out_shape.append(jax.ShapeDtypeStruct((b, l // SUBLANES, SUBLANES, d), F32))
    kern = functools.partial(_inproj_kernel, kinds=tuple(o[1] for o in outs))
    return pl.pallas_call(
        kern,
        grid=(b, l // tm),
        in_specs=in_specs,
        out_specs=out_specs,
        out_shape=out_shape,
        compiler_params=_cparams(("parallel", "parallel")),
        name="inproj",
    )(x, shift, scale, g, *[o[0][0] for o in outs])


LRU_RB = 128
SQRT_FLOOR = 1e-30
LOG2_E = 1.4426950408889634
HALO = 3 * SUBLANES


def _lru_kernel(*refs, seg, nt, rev, final):
    (lu_ref, prev_ref, next_ref, h0_ref, cw_ref, cb_ref, wg_ref, ba_ref, bx_ref, lam_ref) = refs[:10]
    pos = 10
    if final:
        hb_ref, lg_ref = refs[pos:pos + 2]
        pos += 2
    o_ref, hl_ref = refs[pos:pos + 2]
    lo_ref, hi_ref, a_ref, u_ref, st_ref, carry_ref = refs[pos + 2:]

    t = seg * SUBLANES
    i = pl.program_id(1)
    c = (nt - 1 - i) if rev else i

    @pl.when(i == 0)
    def _():
        carry_ref[...] = h0_ref[...]

    width = cw_ref.shape[1]
    row = lax.broadcasted_iota(jnp.int32, (SUBLANES, width), 0)

    spb = LRU_RB // SUBLANES
    nblk = seg // spb
    assert nblk >= 2

    def before(own, other):
        return jnp.where(row == 0, pltpu.roll(jnp.where(c > 0, other, 0.0), 1, 0), pltpu.roll(own, 1, 0))

    lo_ref[0:SUBLANES, :] = before(lu_ref[seg - 2], prev_ref[0])
    lo_ref[SUBLANES:2 * SUBLANES, :] = before(lu_ref[seg - 1], prev_ref[1])
    lo_ref[2 * SUBLANES:, :] = lu_ref[0:spb + 1].reshape((spb + 1) * SUBLANES, width)
    hi_ref[0:(spb + 2) * SUBLANES, :] = lu_ref[seg - spb - 2:seg].reshape((spb + 2) * SUBLANES, width)
    hi_ref[(spb + 2) * SUBLANES:, :] = jnp.where(
        row == SUBLANES - 1, pltpu.roll(jnp.where(c < nt - 1, next_ref[0], 0.0), SUBLANES - 1, 0),
        pltpu.roll(lu_ref[0], SUBLANES - 1, 0))

    cw = cw_ref[...]
    cb = cb_ref[...]
    half_ba = 0.5 * ba_ref[...]
    half_bx = 0.5 * bx_ref[...]
    log_decay = -LRU_C * jax.nn.softplus(-lam_ref[...])
    c2 = log_decay * (0.5 * LOG2_E)
    ngroups = wg_ref.shape[0]
    gw = width // ngroups

    def gate_block(r0, taps):
        xc = cb
        for k in range(4):
            xc = xc + cw[k:k + 1, :] * taps[k]
        for g in range(ngroups):
            sl = slice(g * gw, (g + 1) * gw)
            xg = xc[:, sl]
            pre = _dot(xg.astype(BF16), wg_ref[g])
            t_r = jnp.tanh(pre[:, :gw] + half_ba[:, sl])
            t_i = jnp.tanh(pre[:, gw:] + half_bx[:, sl])
            a = jnp.exp2(t_r * c2[:, sl] + c2[:, sl])
            y = 1.0 - a * a
            hx = 0.5 * xg
            u = (y * lax.rsqrt(jnp.maximum(y, SQRT_FLOOR))) * (hx * t_i + hx)
            a_ref[pl.ds(r0, LRU_RB), sl] = a
            u_ref[pl.ds(r0, LRU_RB), sl] = u

    def edge_taps(ref):
        return [ref[k * SUBLANES:k * SUBLANES + LRU_RB, :] for k in range(4)]

    def interior_block(rb, carry):
        taps = [lu_ref[pl.ds(rb * spb - 2 + k, spb)].reshape(LRU_RB, width) for k in range(4)]
        gate_block(pl.multiple_of(rb * LRU_RB, LRU_RB), taps)
        return carry

    gate_block(0, edge_taps(lo_ref))
    lax.fori_loop(1, nblk - 1, interior_block, 0)
    gate_block((nblk - 1) * LRU_RB, edge_taps(hi_ref))

    def slab(jj):
        j = (seg - 1 - jj) if rev else jj
        return j, pl.multiple_of(j * SUBLANES, SUBLANES)

    def seg_totals(jj, hp):
        h, p = hp
        _, r0 = slab(jj)
        a8 = a_ref[pl.ds(r0, SUBLANES), :]
        return a8 * h + u_ref[pl.ds(r0, SUBLANES), :], p * a8

    h_end, p_end = lax.fori_loop(0, seg, seg_totals,
                                 (jnp.zeros((SUBLANES, width), F32), jnp.ones((SUBLANES, width), F32)), unroll=4)

    cur = carry_ref[...]
    for r in (range(SUBLANES - 1, -1, -1) if rev else range(SUBLANES)):
        st_ref[r:r + 1, :] = cur
        cur = p_end[r:r + 1, :] * cur + h_end[r:r + 1, :]
    carry_ref[...] = cur
    hl_ref[...] = cur

    def emit(jj, h):
        j, r0 = slab(jj)
        h = a_ref[pl.ds(r0, SUBLANES), :] * h + u_ref[pl.ds(r0, SUBLANES), :]
        if final:
            o_ref[j] = (h + hb_ref[j]) * jax.nn.gelu(lg_ref[j])
        else:
            o_ref[j] = h
        return h

    lax.fori_loop(0, seg, emit, st_ref[...], unroll=4)


def _lru_call(lu, h0, cw, cb, wg, ba, bx, lam, *, seg, rev, hb=None, lg=None):
    b, n8, _, w = lu.shape
    nt = n8 // seg
    final = hb is not None
    cidx = (lambda i: nt - 1 - i) if rev else (lambda i: i)
    main = lambda: pl.BlockSpec((None, seg, SUBLANES, w), lambda bi, i: (bi, cidx(i), 0, 0))
    vec = lambda a: pl.BlockSpec(a.shape, lambda bi, i: (0,) * a.ndim)
    in_specs = [
        main(),
        pl.BlockSpec((None, 2, SUBLANES, w), lambda bi, i: (bi, jnp.maximum(cidx(i) * (seg // 2) - 1, 0), 0, 0)),
        pl.BlockSpec((None, 1, SUBLANES, w), lambda bi, i: (bi, jnp.minimum((cidx(i) + 1) * seg, n8 - 1), 0, 0)),
        pl.BlockSpec((None, 1, w), lambda bi, i: (bi, 0, 0)),
        vec(cw), vec(cb), vec(wg), vec(ba), vec(bx), vec(lam),
    ]
    args = [lu, lu, lu, h0, cw, cb, wg, ba, bx, lam]
    if final:
        in_specs += [main(), main()]
        args += [hb, lg]
    t = seg * SUBLANES
    kern = functools.partial(_lru_kernel, seg=seg, nt=nt, rev=rev, final=final)
    return pl.pallas_call(
        kern,
        grid=(b, nt),
        in_specs=in_specs,
        out_specs=[main(), pl.BlockSpec((None, 1, w), lambda bi, i: (bi, 0, 0))],
        out_shape=[jax.ShapeDtypeStruct(lu.shape, F32), jax.ShapeDtypeStruct((b, 1, w), F32)],
        scratch_shapes=[pltpu.VMEM((LRU_RB + HALO, w), F32), pltpu.VMEM((LRU_RB + HALO, w), F32),
                        pltpu.VMEM((t, w), F32), pltpu.VMEM((t, w), F32),
                        pltpu.VMEM((SUBLANES, w), F32), pltpu.VMEM((1, w), F32)],
        compiler_params=_cparams(("parallel", "arbitrary")),
        name="lru_rev" if rev else "lru_fwd",
    )(*args)


def _ssd_kernel(*refs, batched, nb, **kw):
    for bi in range(nb):
        _ssd_chunk(*[r.at[bi] if is_b else r for r, is_b in zip(refs, batched)], **kw)


def _ssd_chunk(*refs, nc, rev, mode):
    if mode == "second":
        xc_ref, dt_ref, h0_ref, bias_r_ref, bias_c_ref, alog_r_ref, alog_c_ref, yb_ref, o_ref, hl_ref, hst_ref = refs
    else:
        (xm_ref, xp_ref, xn_ref, dt_ref, h0_ref, cw_ref, cb_ref, bias_r_ref, bias_c_ref,
         alog_r_ref, alog_c_ref) = refs[:11]
        if mode == "first":
            dsk_ref, o_ref, xc_ref, hl_ref, xb_ref, hst_ref = refs[11:]
        else:
            hl_ref, xb_ref, hst_ref = refs[11:]

    t = SSD_CHUNK
    width = SSD_HEADS * SSD_HEAD_DIM
    gn = SSD_STATE
    i = pl.program_id(0)
    c = (nc - 1 - i) if rev else i
    d = 1 if rev else 0
    tl = 0 if rev else t - 1

    @pl.when(i == 0)
    def _():
        hst_ref[...] = h0_ref[...]

    slabs = []
    for s in range((width + 2 * SSD_GROUPS * gn) // LANES):
        ls = slice(s * LANES, (s + 1) * LANES)
        if mode == "second":
            slabs.append(xc_ref[:, ls])
            continue
        xb_ref[s, 0:SUBLANES, :] = jnp.where(c > 0, xp_ref[:, ls], 0.0)
        xb_ref[s, SUBLANES:SUBLANES + t, :] = xm_ref[:, ls]
        xb_ref[s, SUBLANES + t:, :] = jnp.where(c < nc - 1, xn_ref[:, ls], 0.0)
        xc = cb_ref[:, ls]
        for k in range(4):
            xc = xc + cw_ref[k:k + 1, ls] * xb_ref[s, SUBLANES - 2 + k:SUBLANES - 2 + k + t, :]
        slabs.append(_silu(xc))
        if mode == "first":
            xc_ref[:, ls] = slabs[-1].astype(BF16)
    nx = width // LANES
    x_slabs = slabs[:nx]
    bm = slabs[nx:nx + SSD_GROUPS]
    cm = slabs[nx + SSD_GROUPS:]

    ti = lax.broadcasted_iota(jnp.int32, (t, t), 0)
    si = lax.broadcasted_iota(jnp.int32, (t, t), 1)
    inc = (si >= ti) if rev else (si <= ti)
    inc_b = jnp.where(inc, 1.0, 0.0).astype(BF16)
    inc_t_b = jnp.where((ti >= si) if rev else (ti <= si), 1.0, 0.0).astype(BF16)

    dtraw = dt_ref[...]
    dt = jax.nn.softplus(dtraw + bias_r_ref[...])
    da = dt * (-jnp.exp(alog_r_ref[...]))
    cs = sum(_dot(inc_b, p) for p in _split_bf16(da, 3))
    q0 = SSD_HEADS * d
    dtraw_t = dtraw.T[q0:q0 + SSD_HEADS, :]
    dt_t = jax.nn.softplus(dtraw_t + bias_c_ref[q0:q0 + SSD_HEADS, :])
    da_t = dt_t * (-jnp.exp(alog_c_ref[q0:q0 + SSD_HEADS, :]))
    cs_t = sum(_dot(p, inc_t_b) for p in _split_bf16(da_t, 3))
    w1_t = dt_t * jnp.exp(cs_t[:, tl:tl + 1] - cs_t)
    ecs = jnp.exp(cs)
    cs2 = cs * LOG2_E
    lcs2_t = (cs_t - jnp.log(dt_t)) * LOG2_E

    lane = lax.broadcasted_iota(jnp.int32, (t, LANES), 1)
    lo = lane < SSD_HEAD_DIM
    hg = SSD_HEADS // SSD_GROUPS
    gw = hg * SSD_HEAD_DIM

    if mode != "state":
        scores = [lax.dot_general(cm[g].astype(BF16), bm[g].astype(BF16),
                                  (((1,), (1,)), ((), ())), preferred_element_type=F32)
                  for g in range(SSD_GROUPS)]
        z_off = [_dot(cm[g].astype(BF16), hst_ref[:, g * gw:(g + 1) * gw].astype(BF16))
                 for g in range(SSD_GROUPS)]
    bm_t = [bm[g].astype(F32).T for g in range(SSD_GROUPS)]
    keep_lo = jnp.where(lo, 1.0, 0.0).astype(BF16)
    keep_hi = jnp.where(lo, 0.0, 1.0).astype(BF16)

    ys = []
    for pr in range(SSD_HEADS // 2):
        g = (2 * pr) // hg
        xp = x_slabs[pr].astype(BF16)
        rhs = jnp.concatenate([xp * keep_lo, xp * keep_hi], axis=0)
        lhs_s = jnp.concatenate([bm_t[g] * w1_t[2 * pr + e:2 * pr + e + 1, :] for e in range(2)],
                                axis=1).astype(BF16)
        if mode != "state":
            ms = []
            for e in range(2):
                hh = 2 * pr + e
                q = q0 + hh
                lmat = jnp.where(inc, jnp.exp2(cs2[:, q:q + 1] - lcs2_t[hh:hh + 1, :]), 0.0)
                ms.append((scores[g] * lmat).astype(BF16))
            lhs = jnp.concatenate([jnp.concatenate(ms, axis=1), lhs_s], axis=0)
            res = _dot(lhs, rhs)
            y_diag, s_new = res[:t], res[t:]
            e_pair = jnp.where(lo, ecs[:, q0 + 2 * pr:q0 + 2 * pr + 1], ecs[:, q0 + 2 * pr + 1:q0 + 2 * pr + 2])
            col = (pr * LANES) % gw
            ys.append(y_diag + z_off[g][:, col:col + LANES] * e_pair)
        else:
            s_new = _dot(lhs_s, rhs)
        dec_pair = jnp.where(lo[0:1, :], ecs[tl:tl + 1, q0 + 2 * pr:q0 + 2 * pr + 1],
                             ecs[tl:tl + 1, q0 + 2 * pr + 1:q0 + 2 * pr + 2])
        hst_ref[:, pr * LANES:(pr + 1) * LANES] = hst_ref[:, pr * LANES:(pr + 1) * LANES] * dec_pair + s_new

    @pl.when(i == nc - 1)
    def _():
        hl_ref[...] = hst_ref[...]

    if mode == "state":
        return
    y = jnp.concatenate(ys, axis=1)
    if mode == "first":
        o_ref[...] = y + jnp.concatenate(x_slabs, axis=1) * dsk_ref[...]
    else:
        o_ref[...] = y + yb_ref[...]


def _ssd_call(xin, dt, h0, params, *, rev, mode, yb=None, dsk=None):
    cw, cb, bias_r, bias_c, alog_r, alog_c = params
    b, nc, t, cdim = xin.shape
    width = SSD_HEADS * SSD_HEAD_DIM
    cidx = (lambda i: nc - 1 - i) if rev else (lambda i: i)
    tb = t // SUBLANES
    blk = lambda wd: pl.BlockSpec((b, None, t, wd), lambda i: (0, cidx(i), 0, 0))
    prev = pl.BlockSpec((b, None, SUBLANES, cdim), lambda i: (0, jnp.maximum(cidx(i) - 1, 0), tb - 1, 0))
    nxt = pl.BlockSpec((b, None, SUBLANES, cdim), lambda i: (0, jnp.minimum(cidx(i) + 1, nc - 1), 0, 0))
    vec = lambda a: pl.BlockSpec(a.shape, lambda i: (0,) * a.ndim)
    state = lambda: pl.BlockSpec((b, SSD_STATE, width), lambda i: (0, 0, 0))
    small = [bias_r, bias_c, alog_r, alog_c]
    y_shape = jax.ShapeDtypeStruct((b, nc, t, width), F32)
    h_shape = jax.ShapeDtypeStruct((b, SSD_STATE, width), F32)
    scratch = [pltpu.VMEM((b, SSD_STATE, width), F32)]
    if mode == "second":
        in_specs = [blk(cdim), blk(LANES), state()] + [vec(a) for a in small] + [blk(width)]
        args = [xin, dt, h0] + small + [yb]
        batched = [True] * 3 + [False] * 4 + [True]
        out_specs, out_shape = [blk(width), state()], [y_shape, h_shape]
    else:
        in_specs = [blk(cdim), prev, nxt, blk(LANES), state(), vec(cw), vec(cb)] + [vec(a) for a in small]
        args = [xin, xin, xin, dt, h0, cw, cb] + small
        batched = [True] * 5 + [False] * 6
        out_specs, out_shape = [state()], [h_shape]
        scratch = [pltpu.VMEM((b, cdim // LANES, t + 2 * SUBLANES, LANES), F32)] + scratch
        if mode == "first":
            in_specs.append(vec(dsk))
            args.append(dsk)
            batched.append(False)
            out_specs = [blk(width), blk(cdim)] + out_specs
            out_shape = [y_shape, jax.ShapeDtypeStruct((b, nc, t, cdim), BF16)] + out_shape
    batched += [True] * (len(out_specs) + len(scratch))
    kern = functools.partial(_ssd_kernel, batched=tuple(batched), nb=b, nc=nc, rev=rev, mode=mode)
    return pl.pallas_call(
        kern,
        grid=(nc,),
        in_specs=in_specs,
        out_specs=out_specs,
        out_shape=out_shape,
        scratch_shapes=scratch,
        compiler_params=_cparams(("arbitrary",)),
        name=f"ssd_{mode}_{'rev' if rev else 'fwd'}",
    )(*args)


def _outproj_kernel(x_ref, lru_ref, y_ref, z_ref, g_ref, ng_ref, w1_ref, w2_ref, o_ref, x1_ref):
    rows = x1_ref.shape[0] * x1_ref.shape[1]
    flat = lambda ref: ref[...].reshape(rows, ref.shape[2])
    mix = _dot(flat(lru_ref).astype(BF16), w1_ref[...])
    gated = flat(y_ref) * _silu(flat(z_ref))
    ms = jnp.mean(gated * gated, axis=-1, keepdims=True)
    ssd = (gated * lax.rsqrt(ms + EPS) * ng_ref[...]).astype(BF16)
    mix = mix + _dot(ssd, w2_ref[...])
    x1_ref[...] = (flat(x_ref) + g_ref[...] * mix).reshape(x1_ref.shape)
    o_ref[...] = _gather_groups(x1_ref)


def _outproj_call(x, lru, y, z, gate, norm_g, w_out):
    b, n8, _, d = x.shape
    seg = TILE // SUBLANES
    half = w_out.shape[0] // 2
    assert lru.shape[3] == half and y.shape[3] == half
    tok = lambda wd: pl.BlockSpec((None, seg, SUBLANES, wd), lambda bi, i: (bi, i, 0, 0))
    return pl.pallas_call(
        _outproj_kernel,
        grid=(b, n8 // seg),
        in_specs=[tok(d), tok(half),
                  pl.BlockSpec((None, GRID_W, SUBLANES, half), lambda bi, i: (bi, 0, i, 0)),
                  tok(z.shape[3]),
                  pl.BlockSpec((None, 1, d), lambda bi, i: (bi, 0, 0)),
                  pl.BlockSpec(norm_g.shape, lambda bi, i: (0, 0)),
                  pl.BlockSpec((half, d), lambda bi, i: (0, 0)),
                  pl.BlockSpec((half, d), lambda bi, i: (1, 0))],
        out_specs=pl.BlockSpec((None, TILE, d), lambda bi, i: (bi, i, 0)),
        out_shape=jax.ShapeDtypeStruct((b, n8 * SUBLANES, d), F32),
        scratch_shapes=[pltpu.VMEM((seg, SUBLANES, d), F32)],
        compiler_params=_cparams(("parallel", "parallel")),
        name="outproj",
    )(x, lru, y, z, gate, norm_g, w_out, w_out)


FFN_RB = 128


def _ffn_kernel(xm_ref, xp_ref, xn_ref, sh_ref, sc_ref, gt_ref, ng_ref, fg_ref,
                wup_ref, cwv_ref, cwg_ref, cbv_ref, cbg_ref, wdn_ref, o_ref,
                f_ref, uv_ref, ug_ref, act_ref, acc_ref, *, tm, nt, nj):
    i = pl.program_id(1)
    j = pl.program_id(2)
    nslab = uv_ref.shape[0]
    fb = nslab * LANES
    piece = 2 * LANES

    def up_piece(u_ref, half, q):
        col = pl.multiple_of((half * nj + j) * fb + q * piece, piece)
        res = _dot(f_ref[...], wup_ref[:, pl.ds(col, piece)])
        u_ref[2 * q] = res[:, :LANES]
        u_ref[2 * q + 1] = res[:, LANES:]

    def conv_slab(s):
        ls = slice(s * LANES, (s + 1) * LANES)
        for rb in range(tm // FFN_RB):
            r0 = SUBLANES - 1 + rb * FFN_RB
            val, gate = cbv_ref[:, ls].astype(BF16), cbg_ref[:, ls].astype(BF16)
            for k in range(3):
                val = val + cwv_ref[k:k + 1, ls].astype(BF16) * uv_ref[s, r0 + k:r0 + k + FFN_RB, :].astype(BF16)
                gate = gate + cwg_ref[k:k + 1, ls].astype(BF16) * ug_ref[s, r0 + k:r0 + k + FFN_RB, :].astype(BF16)
            act_ref[rb * FFN_RB:(rb + 1) * FFN_RB, ls] = jax.nn.gelu(gate) * val

    @pl.when(j == 0)
    def _():
        ng, sh, sc = ng_ref[...], sh_ref[...], sc_ref[...]
        fp = _norm_mod(xp_ref[...], ng, sh, sc)
        fn = _norm_mod(xn_ref[...], ng, sh, sc)
        f_ref[0:SUBLANES, :] = jnp.where(i > 0, fp, 0.0).astype(BF16)
        f_ref[SUBLANES:SUBLANES + tm, :] = _norm_mod(xm_ref[...], ng, sh, sc).astype(BF16)
        f_ref[SUBLANES + tm:, :] = jnp.where(i < nt - 1, fn, 0.0).astype(BF16)
        acc_ref[...] = jnp.zeros_like(acc_ref)

    for q in range(nslab // 2):
        up_piece(uv_ref, 0, q)
        up_piece(ug_ref, 1, q)
        conv_slab(2 * q)
        conv_slab(2 * q + 1)
    acc_ref[...] += _dot(act_ref[...], wdn_ref[j])

    @pl.when(j == nj - 1)
    def _():
        x2 = xm_ref[...] + gt_ref[...] * acc_ref[...]
        ms = jnp.mean(x2 * x2, axis=-1, keepdims=True)
        o_ref[...] = x2 * lax.rsqrt(ms + EPS) * fg_ref[...]


def _ffn_call(x1, shift, scale, gate, norm_g, final_g, w_up, conv_w, conv_b, w_down, tm, fb):
    b, l, d = x1.shape
    dff = w_down.shape[0]
    nt = l // tm
    nj = dff // fb
    tb = tm // SUBLANES
    nb8 = l // SUBLANES
    row = lambda: pl.BlockSpec((None, 1, d), lambda bi, i, j: (bi, 0, 0))
    vec = lambda: pl.BlockSpec((1, d), lambda bi, i, j: (0, 0))
    w_down = w_down.reshape(nj, fb, d)
    resident = lambda a: pl.BlockSpec(a.shape, lambda bi, i, j: (0,) * a.ndim, pipeline_mode=pl.Buffered(1))
    in_specs = [
        pl.BlockSpec((None, tm, d), lambda bi, i, j: (bi, i, 0)),
        pl.BlockSpec((None, SUBLANES, d), lambda bi, i, j: (bi, jnp.maximum(i * tb - 1, 0), 0)),
        pl.BlockSpec((None, SUBLANES, d), lambda bi, i, j: (bi, jnp.minimum((i + 1) * tb, nb8 - 1), 0)),
        row(), row(), row(), vec(), vec(),
        resident(w_up),
        pl.BlockSpec((3, fb), lambda bi, i, j: (0, j)),
        pl.BlockSpec((3, fb), lambda bi, i, j: (0, nj + j)),
        pl.BlockSpec((1, fb), lambda bi, i, j: (0, j)),
        pl.BlockSpec((1, fb), lambda bi, i, j: (0, nj + j)),
        resident(w_down),
    ]
    kern = functools.partial(_ffn_kernel, tm=tm, nt=nt, nj=nj)
    slab = lambda: pltpu.VMEM((fb // LANES, tm + 2 * SUBLANES, LANES), F32)
    return pl.pallas_call(
        kern,
        grid=(b, nt, nj),
        in_specs=in_specs,
        out_specs=pl.BlockSpec((None, tm, d), lambda bi, i, j: (bi, i, 0)),
        out_shape=jax.ShapeDtypeStruct((b, l, d), F32),
        scratch_shapes=[pltpu.VMEM((tm + 2 * SUBLANES, d), BF16), slab(), slab(),
                        pltpu.VMEM((tm, fb), BF16), pltpu.VMEM((tm, d), F32)],
        compiler_params=_cparams(("parallel", "parallel", "arbitrary")),
        name="ffn",
    )(x1, x1, x1, shift, scale, gate, norm_g, final_g, w_up, conv_w, conv_w, conv_b, conv_b, w_down)


def _block_diag_gates(wa, wx, heads_per_group):
    h, hd, _ = wa.shape
    ng = h // heads_per_group
    eye = jnp.eye(heads_per_group, dtype=wa.dtype)

    def bd(w):
        w = w.reshape(ng, heads_per_group, hd, hd)
        return jnp.einsum('gaij,ab->gaibj', w, eye).reshape(ng, heads_per_group * hd, heads_per_group * hd)

    return (0.5 * jnp.concatenate([bd(wa), bd(wx)], axis=-1)).astype(BF16)


def _pad_lanes(v, n=LANES):
    return jnp.pad(v, ((0, 0), (0, n - v.shape[1])))


def kernel(x, c, ctx, c_ctx, ada_w, ada_b, norm1_g, w_in, lru_conv_w, lru_conv_b, lru_wa, lru_ba, lru_wx, lru_bx,
           lru_lambda, ssd_conv_w, ssd_conv_b, ssd_a_log, ssd_dt_bias, ssd_d, ssd_norm_g, w_out, norm2_g,
           ffn_w_up, ffn_conv_w, ffn_conv_b, ffn_w_down, final_norm_g):
    b, l, d = x.shape
    lctx = ctx.shape[1]
    lw = lru_conv_w.shape[2]
    sw = SSD_HEADS * SSD_HEAD_DIM
    cdim = ssd_conv_w.shape[2]
    rows = l // GRID_W
    assert ada_w.shape[0] == 1, "single layer"
    assert rows == SSD_CHUNK, "an SSD chunk is one column of the latent grid"
    assert lctx % SSD_CHUNK == 0 and l % TILE == 0

    s_in = jnp.zeros((SUBLANES, d), F32).at[:b].set(c).at[b].set(c_ctx)
    mod = _mod_call(s_in, ada_w[0], ada_b)
    mod_lat = [m.reshape(b, 1, d) for m in jnp.split(mod[:b], 6, axis=-1)]
    mod_ctx = [jnp.broadcast_to(m.reshape(1, 1, d), (b, 1, d)) for m in jnp.split(mod[b:b + 1], 6, axis=-1)]
    sh1, sc1, g1, sh2, sc2, g2 = mod_lat
    csh1, csc1 = mod_ctx[0], mod_ctx[1]

    wi = w_in[0].astype(BF16)
    o1, o2, o3, o4 = lw, 2 * lw, 2 * lw + sw, 2 * lw + sw + cdim
    assert o1 % lw == 0 and o2 % sw == 0 and o3 % cdim == 0
    w_lu, w_lg, w_z, w_xbc = (wi, lw, 0), (wi, lw, o1 // lw), (wi, sw, o2 // sw), (wi, cdim, o3 // cdim)
    w_dt = (_pad_lanes(wi[:, o4:]), LANES, 0)
    seg = TILE // SUBLANES
    interleaved = lambda wd: ("groups", (l // SUBLANES, SUBLANES, wd), (seg, SUBLANES, wd), lambda i: (i, 0, 0))
    column = lambda wd: ("groups", (GRID_W, rows, wd), (GRID_W, SUBLANES, wd), lambda i: (0, i, 0))
    lu_l, lg_l, z_l, xbc_l, dt_l, x_g = _inproj_call(x, sh1, sc1, norm1_g, [
        (w_lu,) + interleaved(lw), (w_lg,) + interleaved(lw), (w_z,) + interleaved(sw),
        (w_xbc,) + column(cdim), (w_dt,) + column(LANES)], TILE, emit_x=True)
    cseg = lctx // SUBLANES
    nctx = lctx // SSD_CHUNK
    chunks = lambda wd: ("chunks", (nctx, SSD_CHUNK, wd), (nctx, SSD_CHUNK, wd), lambda i: (0, 0, 0))
    lu_c, xbc_c, dt_c = _inproj_call(ctx, csh1, csc1, norm1_g, [
        (w_lu, "groups", (cseg, SUBLANES, lw), (cseg, SUBLANES, lw), lambda i: (0, 0, 0)),
        (w_xbc,) + chunks(cdim), (w_dt,) + chunks(LANES)], lctx)

    lcw, lcb = lru_conv_w[0], lru_conv_b
    zeros_w = jnp.zeros((b, 1, lw), F32)
    lru_args = []
    for dr in range(2):
        lru_args.append((lcw, lcb, _block_diag_gates(lru_wa[0, dr], lru_wx[0, dr], LRU_GROUP_HEADS),
                         lru_ba[0, dr][None], lru_bx[0, dr][None], lru_lambda[0, dr][None]))
    _, hc_f = _lru_call(lu_c, zeros_w, *lru_args[0], seg=cseg, rev=False)
    _, hc_b = _lru_call(lu_c, zeros_w, *lru_args[1], seg=cseg, rev=True)
    h_b, _ = _lru_call(lu_l, hc_b, *lru_args[1], seg=seg, rev=True)
    lru_out, _ = _lru_call(lu_l, hc_f, *lru_args[0], seg=seg, rev=False, hb=h_b, lg=lg_l)

    ssd_params = (ssd_conv_w[0], ssd_conv_b,
                  _pad_lanes(ssd_dt_bias[0].reshape(1, -1)), _pad_lanes(ssd_dt_bias[0].reshape(1, -1)).T,
                  _pad_lanes(ssd_a_log[0].reshape(1, -1)), _pad_lanes(ssd_a_log[0].reshape(1, -1)).T)
    zero_state = jnp.zeros((b, SSD_STATE, sw), F32)
    (sc_f,) = _ssd_call(xbc_c, dt_c, zero_state, ssd_params, rev=False, mode="state")
    (sc_b,) = _ssd_call(xbc_c, dt_c, zero_state, ssd_params, rev=True, mode="state")
    dsk = jnp.repeat(ssd_d[0], SSD_HEAD_DIM)[None]
    y_b, xc_l, _ = _ssd_call(xbc_l, dt_l, sc_b, ssd_params, rev=True, mode="first", dsk=dsk)
    y_l, _ = _ssd_call(xc_l, dt_l, sc_f, ssd_params, rev=False, mode="second", yb=y_b)

    x1 = _outproj_call(x_g, lru_out, y_l, z_l, g1, ssd_norm_g, w_out[0].astype(BF16))
    return _ffn_call(x1, sh2, sc2, g2, norm2_g, final_norm_g[None], ffn_w_up[0].astype(BF16), ffn_conv_w[0],
                     ffn_conv_b, ffn_w_down[0].astype(BF16), FFN_TOKENS, FFN_BLOCK)
```
